```python
import math
import jax, jax.numpy as jnp
from jax import lax
import numpy as np

D_MODEL = 1024
BATCH = 16
SEQ = 256
DEPTH = 4
DEC_BATCH = 4
DEC_SEQ = 2048
PAST_LEN = 512

GRID_W = 64
HEAD_DIM = 64
N_HEADS = 8
N_KV_HEADS = 2
GROUP = N_HEADS // N_KV_HEADS
ATTN_W = N_HEADS * HEAD_DIM
KV_W = N_KV_HEADS * HEAD_DIM
WINDOW = 128
BLOCK = 128
HY_W = 512
FILT_EMB = 33
FILT_BANDS = (FILT_EMB - 1) // 2
FILT_HIDDEN = 64
D_FF = 2816
CONV_W = 3
ROPE_BASE = 10000.0
LN_EPS = 1e-5
DEEPNORM_ALPHA = (2 * DEPTH) ** 0.25
DEEPNORM_BETA = (8 * DEPTH) ** -0.25
IN_COLS = ATTN_W + 2 * KV_W + 3 * HY_W + 2 * D_MODEL
SPLITS = [ATTN_W, ATTN_W + KV_W, ATTN_W + 2 * KV_W, ATTN_W + 2 * KV_W + 3 * HY_W,
          ATTN_W + 2 * KV_W + 3 * HY_W + D_MODEL]

kernel_name = 'hybrid_dit_swa_hyena_deepnorm_step'


def layer_norm(x, g, b):
    xf = x.astype(jnp.float32)
    mu = jnp.mean(xf, axis=-1, keepdims=True)
    var = jnp.mean(jnp.square(xf - mu), axis=-1, keepdims=True)
    y = (xf - mu) * lax.rsqrt(var + LN_EPS) * g.astype(jnp.float32) + b.astype(jnp.float32)
    return y.astype(x.dtype)


def centred_dwconv(x, w, b):
    L = x.shape[1]
    xp = jnp.pad(x, ((0, 0), (1, 1), (0, 0)))
    return xp[:, :L] * w[0] + xp[:, 1:L + 1] * w[1] + xp[:, 2:] * w[2] + b


def axial_rope(x):
    L = x.shape[1]
    rows = L // GRID_W
    r, col = jnp.meshgrid(jnp.arange(rows), jnp.arange(GRID_W), indexing='ij')
    pos = jnp.stack([r.reshape(-1), col.reshape(-1)], axis=-1).astype(jnp.float32)
    half = HEAD_DIM // 2
    inv_freq = 1.0 / (ROPE_BASE ** (jnp.arange(0, half, 2, dtype=jnp.float32) / half))
    ang = pos[:, :, None] * inv_freq
    ang = jnp.stack([ang, ang], axis=-2).reshape(L, HEAD_DIM)
    cos = jnp.cos(ang)[None, :, None, :]
    sin = jnp.sin(ang)[None, :, None, :]
    xf = x.astype(jnp.float32)
    xa = xf.reshape(x.shape[:-1] + (2, 2, HEAD_DIM // 4))
    rot = jnp.stack([-xa[..., 1, :], xa[..., 0, :]], axis=-2).reshape(x.shape)
    return (xf * cos + rot * sin).astype(x.dtype)


def softmax_with_sink(logits, sink):
    s = sink[None, :, :, None, None]
    m = jnp.maximum(jnp.max(logits, axis=-1, keepdims=True), s)
    e = jnp.exp(logits - m)
    return e / (jnp.sum(e, axis=-1, keepdims=True) + jnp.exp(s - m))


def context_attention(q, k, v, sink):
    B, Lc = q.shape[:2]
    nb = Lc // BLOCK
    qb = q.reshape(B, nb, BLOCK, N_KV_HEADS, GROUP, HEAD_DIM).transpose(1, 0, 2, 3, 4, 5)
    s_sink = sink.reshape(N_KV_HEADS, GROUP).astype(jnp.float32)
    kf = k.astype(jnp.float32)
    vf = v.astype(jnp.float32)
    scale = HEAD_DIM ** -0.5

    def one_block(qblk):
        logits = jnp.einsum('bqkgd,bskd->bkgqs', qblk.astype(jnp.float32), kf) * scale
        p = softmax_with_sink(logits, s_sink)
        return jnp.einsum('bkgqs,bskd->bqkgd', p, vf)

    out = lax.map(one_block, qb)
    return out.transpose(1, 0, 2, 3, 4, 5).reshape(B, Lc, ATTN_W).astype(q.dtype)


def latent_attention(q, k, v, k_ctx, v_ctx, sink):
    B, L = q.shape[:2]
    Lc = k_ctx.shape[1]
    nb = L // BLOCK
    span = BLOCK + 2 * WINDOW
    qb = q.reshape(B, nb, BLOCK, N_KV_HEADS, GROUP, HEAD_DIM).transpose(1, 0, 2, 3, 4, 5)
    kp = jnp.pad(k.astype(jnp.float32), ((0, 0), (WINDOW, WINDOW), (0, 0), (0, 0)))
    vp = jnp.pad(v.astype(jnp.float32), ((0, 0), (WINDOW, WINDOW), (0, 0), (0, 0)))
    kc = k_ctx.astype(jnp.float32)
    vc = v_ctx.astype(jnp.float32)
    s_sink = sink.reshape(N_KV_HEADS, GROUP).astype(jnp.float32)
    scale = HEAD_DIM ** -0.5

    def one_block(args):
        bi, qblk = args
        start = bi * BLOCK
        kb = lax.dynamic_slice_in_dim(kp, start, span, axis=1)
        vb = lax.dynamic_slice_in_dim(vp, start, span, axis=1)
        qpos = start + jnp.arange(BLOCK)
        kpos = start - WINDOW + jnp.arange(span)
        mask = (jnp.abs(qpos[:, None] - kpos[None, :]) <= WINDOW) & (kpos >= 0)[None, :] & (kpos < L)[None, :]
        qf = qblk.astype(jnp.float32)
        l_ctx = jnp.einsum('bqkgd,bskd->bkgqs', qf, kc) * scale
        l_band = jnp.einsum('bqkgd,bskd->bkgqs', qf, kb) * scale
        l_band = jnp.where(mask[None, None, None], l_band, -jnp.inf)
        p = softmax_with_sink(jnp.concatenate([l_ctx, l_band], axis=-1), s_sink)
        return (jnp.einsum('bkgqs,bskd->bqkgd', p[..., :Lc], vc)
                + jnp.einsum('bkgqs,bskd->bqkgd', p[..., Lc:], vb))

    out = lax.map(one_block, (jnp.arange(nb), qb))
    return out.transpose(1, 0, 2, 3, 4, 5).reshape(B, L, ATTN_W).astype(q.dtype)


def hyena_decay_rates():
    target = 1e-2
    max_decay = math.log(target) / 0.3
    min_decay = math.log(target) / 1.5
    return jnp.abs(jnp.linspace(min_decay, max_decay, HY_W, dtype=jnp.float32))


def hyena_filters(L, lp):
    t = jnp.linspace(0.0, 1.0, L, dtype=jnp.float32)[:, None]
    w = 2.0 * math.pi * jnp.arange(L, dtype=jnp.float32) / L
    f = jnp.linspace(1e-4, FILT_BANDS - 1, FILT_BANDS, dtype=jnp.float32)
    zr = w[:, None] * f[None, :]
    feats = jnp.concatenate([t, jnp.cos(zr), -jnp.sin(zr)], axis=-1)
    h = jnp.sin(lp['filt_freq1'].astype(jnp.float32) * (feats @ lp['filt_w1'].astype(jnp.float32) + lp['filt_b1'].astype(jnp.float32)))
    h = jnp.sin(lp['filt_freq2'].astype(jnp.float32) * (h @ lp['filt_w2'].astype(jnp.float32) + lp['filt_b2'].astype(jnp.float32)))
    h = (h @ lp['filt_w3'].astype(jnp.float32)).reshape(L, 2, HY_W)
    window = jnp.exp(-t * hyena_decay_rates()[None, :])
    return h * window[:, None, :]


def bidir_long_conv(u, h, bias):
    L, C = u.shape[1], u.shape[2]
    g = jnp.concatenate([h[:, 0], jnp.zeros((1, C), jnp.float32), h[:0:-1, 1]], axis=0)
    U = jnp.fft.rfft(u, n=2 * L, axis=1)
    G = jnp.fft.rfft(g, n=2 * L, axis=0)
    y = jnp.fft.irfft(U * G[None], n=2 * L, axis=1)[:, :L]
    return y + u * bias


def hyena_branch(z, lp):
    z = centred_dwconv(z, lp['hy_conv_w'], lp['hy_conv_b'])
    x0, x1, u = jnp.split(z, 3, axis=-1)
    h = hyena_filters(z.shape[1], lp)
    y = bidir_long_conv((x1 * u).astype(jnp.float32), h, lp['hy_bias'].astype(jnp.float32))
    return x0 * y.astype(z.dtype)


def conv_ffn(h, lp):
    gate, val = jnp.split(h @ lp['w_up'], 2, axis=-1)
    gate = centred_dwconv(gate, lp['ffn_conv_w'], lp['ffn_conv_b'])
    return (jax.nn.gelu(gate, approximate=False) * val) @ lp['w_down']


def trunk_layer(x, cond, lp, ctx_k=None, ctx_v=None):
    mod = jax.nn.silu(cond) @ lp['w_ada'] + lp['b_ada']
    sh1, sc1, g1, sh2, sc2, g2 = jnp.split(mod[:, None, :], 6, axis=-1)
    B, L, _ = x.shape
    h = x * (1.0 + sc1) + sh1
    q, k, v, z_hy, gate_a, gate_b = jnp.split(h @ lp['w_in'], SPLITS, axis=-1)
    q = q.reshape(B, L, N_HEADS, HEAD_DIM)
    k = k.reshape(B, L, N_KV_HEADS, HEAD_DIM)
    v = v.reshape(B, L, N_KV_HEADS, HEAD_DIM)
    if ctx_k is None:
        attn = context_attention(q, k, v, lp['attn_sink'])
    else:
        attn = latent_attention(axial_rope(q), axial_rope(k), v, ctx_k, ctx_v, lp['attn_sink'])
    hy = hyena_branch(z_hy, lp)
    merged = jax.nn.sigmoid(gate_a) * (attn @ lp['w_pa']) + jax.nn.sigmoid(gate_b) * (hy @ lp['w_pb'])
    x = layer_norm(DEEPNORM_ALPHA * x + g1 * (merged @ lp['w_out']), lp['ln1_g'], lp['ln1_b'])
    ffn = conv_ffn(x * (1.0 + sc2) + sh2, lp)
    x = layer_norm(DEEPNORM_ALPHA * x + g2 * ffn, lp['ln2_g'], lp['ln2_b'])
    return x, k, v


def setup_inputs(seed: int = 0) -> dict:
    key = jax.random.key(seed)
    ks = jax.random.split(key, 32)

    def nrm(k, shape, scale):
        return jax.random.normal(k, shape, jnp.float32) * scale

    return {
        'x_prompt': nrm(ks[0], (BATCH, SEQ, D_MODEL), 1.0),
        'x_sample': nrm(ks[1], (DEC_BATCH, DEC_SEQ, D_MODEL), 1.0),
        'cache_k': nrm(ks[2], (DEC_BATCH, DEPTH, PAST_LEN, N_KV_HEADS, HEAD_DIM), 1.0),
        'cache_v': nrm(ks[3], (DEC_BATCH, DEPTH, PAST_LEN, N_KV_HEADS, HEAD_DIM), 1.0),
        'c': nrm(ks[4], (DEC_BATCH, D_MODEL), 1.0),
        'c_ctx': nrm(ks[5], (D_MODEL,), 1.0),
        'w_ada': nrm(ks[6], (DEPTH, D_MODEL, 6 * D_MODEL), D_MODEL ** -0.5),
        'b_ada': nrm(ks[7], (DEPTH, 6 * D_MODEL), 0.01),
        'w_in': nrm(ks[8], (DEPTH, D_MODEL, IN_COLS), D_MODEL ** -0.5),
        'attn_sink': nrm(ks[9], (DEPTH, N_HEADS), 1.0),
        'hy_conv_w': nrm(ks[10], (DEPTH, CONV_W, 3 * HY_W), CONV_W ** -0.5),
        'hy_conv_b': nrm(ks[11], (DEPTH, 3 * HY_W), 0.01),
        'filt_w1': nrm(ks[12], (DEPTH, FILT_EMB, FILT_HIDDEN), FILT_EMB ** -0.5),
        'filt_b1': nrm(ks[13], (DEPTH, FILT_HIDDEN), 0.01),
        'filt_freq1': 1.0 + nrm(ks[14], (DEPTH, FILT_HIDDEN), 0.01),
        'filt_w2': nrm(ks[15], (DEPTH, FILT_HIDDEN, FILT_HIDDEN), FILT_HIDDEN ** -0.5),
        'filt_b2': nrm(ks[16], (DEPTH, FILT_HIDDEN), 0.01),
        'filt_freq2': 1.0 + nrm(ks[17], (DEPTH, FILT_HIDDEN), 0.01),
        'filt_w3': nrm(ks[18], (DEPTH, FILT_HIDDEN, 2 * HY_W), FILT_HIDDEN ** -0.5),
        'hy_bias': nrm(ks[19], (DEPTH, HY_W), 1.0),
        'w_pa': nrm(ks[20], (DEPTH, ATTN_W, D_MODEL), ATTN_W ** -0.5),
        'w_pb': nrm(ks[21], (DEPTH, HY_W, D_MODEL), HY_W ** -0.5),
        'w_out': nrm(ks[22], (DEPTH, D_MODEL, D_MODEL), D_MODEL ** -0.5 * DEEPNORM_BETA),
        'ln1_g': 1.0 + nrm(ks[23], (DEPTH, D_MODEL), 0.01),
        'ln1_b': nrm(ks[24], (DEPTH, D_MODEL), 0.01),
        'w_up': nrm(ks[25], (DEPTH, D_MODEL, 2 * D_FF), D_MODEL ** -0.5),
        'ffn_conv_w': nrm(ks[26], (DEPTH, CONV_W, D_FF), CONV_W ** -0.5),
        'ffn_conv_b': nrm(ks[27], (DEPTH, D_FF), 0.01),
        'w_down': nrm(ks[28], (DEPTH, D_FF, D_MODEL), D_FF ** -0.5 * DEEPNORM_BETA),
        'ln2_g': 1.0 + nrm(ks[29], (DEPTH, D_MODEL), 0.01),
        'ln2_b': nrm(ks[30], (DEPTH, D_MODEL), 0.01),
    }


def reference(x_prompt, x_sample, cache_k, cache_v, c, c_ctx, w_ada, b_ada, w_in, attn_sink,
              hy_conv_w, hy_conv_b, filt_w1, filt_b1, filt_freq1, filt_w2, filt_b2, filt_freq2,
              filt_w3, hy_bias, w_pa, w_pb, w_out, ln1_g, ln1_b, w_up, ffn_conv_w, ffn_conv_b,
              w_down, ln2_g, ln2_b):
    y_prompt = x_prompt
    y_sample = x_sample
    keys, values = [], []
    for l in range(DEPTH):
        lp = {
            'w_ada': w_ada[l], 'b_ada': b_ada[l], 'w_in': w_in[l], 'attn_sink': attn_sink[l],
            'hy_conv_w': hy_conv_w[l], 'hy_conv_b': hy_conv_b[l],
            'filt_w1': filt_w1[l], 'filt_b1': filt_b1[l], 'filt_freq1': filt_freq1[l],
            'filt_w2': filt_w2[l], 'filt_b2': filt_b2[l], 'filt_freq2': filt_freq2[l],
            'filt_w3': filt_w3[l], 'hy_bias': hy_bias[l],
            'w_pa': w_pa[l], 'w_pb': w_pb[l], 'w_out': w_out[l], 'ln1_g': ln1_g[l], 'ln1_b': ln1_b[l],
            'w_up': w_up[l], 'ffn_conv_w': ffn_conv_w[l], 'ffn_conv_b': ffn_conv_b[l],
            'w_down': w_down[l], 'ln2_g': ln2_g[l], 'ln2_b': ln2_b[l],
        }
        y_prompt, k_l, v_l = trunk_layer(y_prompt, c_ctx[None, :], lp)
        keys.append(k_l)
        values.append(v_l)
        y_sample, _, _ = trunk_layer(y_sample, c, lp, cache_k[:, l], cache_v[:, l])
    new_cache_k = jnp.stack(keys, axis=1)
    new_cache_v = jnp.stack(values, axis=1)
    return (y_prompt, y_sample, new_cache_k, new_cache_v)
```

```python
import functools
import math

import jax
import jax.numpy as jnp
from jax import lax
from jax.experimental import pallas as pl
from jax.experimental.pallas import tpu as pltpu

D_MODEL = 1024
BATCH = 16
SEQ = 256
DEPTH = 4
DEC_BATCH = 4
DEC_SEQ = 2048
PAST_LEN = 512
GRID_W = 64
HEAD_DIM = 64
N_HEADS = 8
N_KV_HEADS = 2
GROUP = N_HEADS // N_KV_HEADS
ATTN_W = N_HEADS * HEAD_DIM
KV_W = N_KV_HEADS * HEAD_DIM
WINDOW = 128
QBLOCK = 128
HY_W = 512
FILT_EMB = 33
FILT_EMB_PAD = 40
FILT_BANDS = (FILT_EMB - 1) // 2
FILT_HIDDEN = 64
D_FF = 2816
ROPE_BASE = 10000.0
LN_EPS = 1e-5
DEEPNORM_ALPHA = (2 * DEPTH) ** 0.25

N_CTX = BATCH * SEQ
N_LAT = DEC_BATCH * DEC_SEQ
N_TOK = N_CTX + N_LAT
N_MOD_ROWS = 8
QKV_W = ATTN_W + 2 * KV_W
BAND = QBLOCK + 2 * WINDOW
MASK_VALUE = -1e30

F32 = jnp.float32
BF16 = jnp.bfloat16

VMEM_LIMIT_BYTES = 56 * 1024 * 1024
FF_CHUNK = 256
DFT_CHUNK = {SEQ: 256, DEC_SEQ: 512}


def _params(*semantics):
    return pltpu.CompilerParams(dimension_semantics=semantics, vmem_limit_bytes=VMEM_LIMIT_BYTES)


def _dot(a, b):
    return jnp.dot(a, b, preferred_element_type=F32)


def _dot_nt(a, b):
    return lax.dot_general(a, b, (((1,), (1,)), ((), ())), preferred_element_type=F32)


def _mod_row(tile_rows):
    n_ctx_tiles = N_CTX // tile_rows
    tiles_per_seq = DEC_SEQ // tile_rows

    def fn(m):
        return jnp.where(m < n_ctx_tiles, 0, 1 + (m - n_ctx_tiles) // tiles_per_seq)

    return fn


def _mod_spec(layer, tile_rows):
    grp = _mod_row(tile_rows)
    return pl.BlockSpec((None, None, 1, 6 * D_MODEL), lambda m, *_: (layer, grp(m), 0, 0))


def _layer_spec(shape, layer):
    zeros = (0,) * len(shape)
    return pl.BlockSpec((None,) + tuple(shape), lambda *_: (layer,) + zeros)


def _layer_norm(y, g, b):
    mu = jnp.mean(y, axis=-1, keepdims=True)
    yc = y - mu
    var = jnp.mean(yc * yc, axis=-1, keepdims=True)
    return yc * lax.rsqrt(var + LN_EPS) * g + b


def _mod_kernel(cond_ref, w_ref, b_ref, o_ref):
    c = cond_ref[...]
    s = (c * jax.nn.sigmoid(c)).astype(BF16)
    o_ref[...] = _dot(s, w_ref[...].astype(BF16)) + b_ref[...]


def _modulation(cond, w_ada, b_ada):
    n_col = 6 * D_MODEL // D_MODEL
    return pl.pallas_call(
        _mod_kernel,
        out_shape=jax.ShapeDtypeStruct((DEPTH, N_MOD_ROWS, 6 * D_MODEL), F32),
        grid=(DEPTH, n_col),
        in_specs=[
            pl.BlockSpec((N_MOD_ROWS, D_MODEL), lambda l, j: (0, 0)),
            pl.BlockSpec((None, D_MODEL, D_MODEL), lambda l, j: (l, 0, j)),
            pl.BlockSpec((None, 1, D_MODEL), lambda l, j: (l, 0, j)),
        ],
        out_specs=pl.BlockSpec((None, N_MOD_ROWS, D_MODEL), lambda l, j: (l, 0, j)),
        compiler_params=_params("parallel", "parallel"),
        name="modulation",
    )(cond, w_ada, b_ada.reshape(DEPTH, 1, 6 * D_MODEL))


PRO_TILE = 1024


def _prologue_kernel(xp_ref, xs_ref, mod_ref, x_ref, h_ref):
    m = pl.program_id(0)

    def emit(x):
        x_ref[...] = x
        sh = mod_ref[:, 0:D_MODEL]
        sc = mod_ref[:, D_MODEL:2 * D_MODEL]
        h_ref[...] = (x * (1.0 + sc) + sh).astype(BF16)

    @pl.when(m < N_CTX // PRO_TILE)
    def _():
        emit(xp_ref[...])

    @pl.when(m >= N_CTX // PRO_TILE)
    def _():
        emit(xs_ref[...])


def _prologue(xp, xs, mods):
    nct = N_CTX // PRO_TILE
    return pl.pallas_call(
        _prologue_kernel,
        out_shape=(jax.ShapeDtypeStruct((N_TOK, D_MODEL), F32),
                   jax.ShapeDtypeStruct((N_TOK, D_MODEL), BF16)),
        grid=(N_TOK // PRO_TILE,),
        in_specs=[
            pl.BlockSpec((PRO_TILE, D_MODEL), lambda m: (jnp.minimum(m, nct - 1), 0)),
            pl.BlockSpec((PRO_TILE, D_MODEL), lambda m: (jnp.maximum(m - nct, 0), 0)),
            _mod_spec(0, PRO_TILE),
        ],
        out_specs=(pl.BlockSpec((PRO_TILE, D_MODEL), lambda m: (m, 0)),
                   pl.BlockSpec((PRO_TILE, D_MODEL), lambda m: (m, 0))),
        compiler_params=_params("parallel"),
        name="prologue",
    )(xp, xs, mods)


QKV_TILE = 1024
ROPE_LANES = 2 * HEAD_DIM


def _qkv_kernel(h_ref, w_ref, cos_ref, sin_ref, q_ref, k_ref, v_ref, kf_ref, vf_ref):
    h = h_ref[...]
    cos = cos_ref[...]
    sin = sin_ref[...]
    lane = lax.broadcasted_iota(jnp.int32, (QKV_TILE, ROPE_LANES), 1)
    first_half = (lane & (HEAD_DIM // 4)) == 0

    def rope(x):
        partner = jnp.where(first_half, pltpu.roll(x, ROPE_LANES - HEAD_DIM // 4, 1),
                            pltpu.roll(x, HEAD_DIM // 4, 1))
        return x * cos + partner * sin

    scale = HEAD_DIM ** -0.5
    for j in range(ATTN_W // 256):
        qq = _dot(h, w_ref[:, 256 * j:256 * (j + 1)])
        for i in range(2):
            c0 = 256 * j + ROPE_LANES * i
            q_ref[:, c0:c0 + ROPE_LANES] = (
                rope(qq[:, ROPE_LANES * i:ROPE_LANES * (i + 1)]) * scale).astype(BF16)
    kv = _dot(h, w_ref[:, ATTN_W:QKV_W])
    k = kv[:, 0:KV_W]
    v = kv[:, KV_W:2 * KV_W]
    kf_ref[...] = k
    vf_ref[...] = v
    k_ref[...] = rope(k).astype(BF16)
    v_ref[...] = v.astype(BF16)


def _qkv_proj(h, w_qkv, cos_tab, sin_tab):
    nct = N_CTX // QKV_TILE
    tps = DEC_SEQ // QKV_TILE
    tab = lambda m: (jnp.where(m < nct, 0, 1 + (m - nct) % tps), 0, 0)
    row = lambda m: (m, 0)
    return pl.pallas_call(
        _qkv_kernel,
        out_shape=(jax.ShapeDtypeStruct((N_TOK, ATTN_W), BF16),
                   jax.ShapeDtypeStruct((N_TOK, KV_W), BF16),
                   jax.ShapeDtypeStruct((N_TOK, KV_W), BF16),
                   jax.ShapeDtypeStruct((N_TOK, KV_W), F32),
                   jax.ShapeDtypeStruct((N_TOK, KV_W), F32)),
        grid=(N_TOK // QKV_TILE,),
        in_specs=[
            pl.BlockSpec((QKV_TILE, D_MODEL), row),
            pl.BlockSpec((D_MODEL, QKV_W), lambda m: (0, 0)),
            pl.BlockSpec((None, QKV_TILE, ROPE_LANES), tab),
            pl.BlockSpec((None, QKV_TILE, ROPE_LANES), tab),
        ],
        out_specs=(pl.BlockSpec((QKV_TILE, ATTN_W), row),
                   pl.BlockSpec((QKV_TILE, KV_W), row),
                   pl.BlockSpec((QKV_TILE, KV_W), row),
                   pl.BlockSpec((QKV_TILE, KV_W), row),
                   pl.BlockSpec((QKV_TILE, KV_W), row)),
        compiler_params=_params("parallel"),
        name="qkv_proj",
    )(h, w_qkv, cos_tab, sin_tab)


def _rope_tables():
    rows = DEC_SEQ // GRID_W
    r, col = jnp.meshgrid(jnp.arange(rows), jnp.arange(GRID_W), indexing='ij')
    pos = jnp.stack([r.reshape(-1), col.reshape(-1)], axis=-1).astype(F32)
    half = HEAD_DIM // 2
    inv_freq = 1.0 / (ROPE_BASE ** (jnp.arange(0, half, 2, dtype=F32) / half))
    ang = pos[:, :, None] * inv_freq
    ang = jnp.stack([ang, ang], axis=-2).reshape(DEC_SEQ, HEAD_DIM)
    cos = jnp.tile(jnp.cos(ang), (1, ROPE_LANES // HEAD_DIM))
    sin = jnp.tile(jnp.sin(ang), (1, ROPE_LANES // HEAD_DIM))
    lane = jnp.arange(ROPE_LANES)
    sin = jnp.where((lane & (HEAD_DIM // 4)) == 0, -sin, sin)
    n_slab = DEC_SEQ // QKV_TILE
    cos = jnp.concatenate([jnp.ones((1, QKV_TILE, ROPE_LANES), F32),
                           cos.reshape(n_slab, QKV_TILE, ROPE_LANES)], axis=0)
    sin = jnp.concatenate([jnp.zeros((1, QKV_TILE, ROPE_LANES), F32),
                           sin.reshape(n_slab, QKV_TILE, ROPE_LANES)], axis=0)
    return cos, sin


HY_TILE = 2048


def _seq_edges(m, tile_rows, n_cols):
    last = jnp.where(m < N_CTX // tile_rows, SEQ - 1, DEC_SEQ - 1)
    pos = lax.broadcasted_iota(jnp.int32, (tile_rows, n_cols), 0) & last
    return pos == 0, pos == last


def _dwconv3(z, is_first, is_last, w_ref, b_ref, c0, n_cols):
    rows = z.shape[0]
    prev = jnp.where(is_first, 0.0, pltpu.roll(z, 1, 0))
    nxt = jnp.where(is_last, 0.0, pltpu.roll(z, rows - 1, 0))
    cs = slice(c0, c0 + n_cols)
    return prev * w_ref[0:1, cs] + z * w_ref[1:2, cs] + nxt * w_ref[2:3, cs] + b_ref[:, cs]


def _hy_kernel(h_ref, w_ref, cw_ref, cb_ref, x0_ref, vh_ref):
    is_first, is_last = _seq_edges(pl.program_id(0), HY_TILE, 256)
    h = h_ref[...]

    def proj_conv(c0):
        z = _dot(h, w_ref[:, c0:c0 + 256])
        return _dwconv3(z, is_first, is_last, cw_ref, cb_ref, c0, 256)

    for j in range(HY_W // 256):
        cs = slice(256 * j, 256 * (j + 1))
        x0_ref[:, cs] = proj_conv(256 * j).astype(BF16)
        x1 = proj_conv(HY_W + 256 * j)
        u = proj_conv(2 * HY_W + 256 * j)
        vh_ref[:, cs] = (x1 * u).astype(BF16)


def _hy_proj(h, w_hy, conv_w, conv_b, layer):
    row = lambda m: (m, 0)
    return pl.pallas_call(
        _hy_kernel,
        out_shape=(jax.ShapeDtypeStruct((N_TOK, HY_W), BF16),
                   jax.ShapeDtypeStruct((N_TOK, HY_W), BF16)),
        grid=(N_TOK // HY_TILE,),
        in_specs=[
            pl.BlockSpec((HY_TILE, D_MODEL), row),
            pl.BlockSpec((D_MODEL, 3 * HY_W), lambda m: (0, 0)),
            _layer_spec((3, 3 * HY_W), layer),
            _layer_spec((1, 3 * HY_W), layer),
        ],
        out_specs=(pl.BlockSpec((HY_TILE, HY_W), row), pl.BlockSpec((HY_TILE, HY_W), row)),
        compiler_params=_params("parallel"),
        name="hyena_proj",
    )(h, w_hy, conv_w, conv_b)


def _filter_kernel(feat_ref, t_ref, dec_ref, w1_ref, b1_ref, f1_ref, w2_ref, b2_ref, f2_ref,
                   w3_ref, o_ref):
    hp = lax.Precision.HIGHEST
    a = jnp.dot(feat_ref[...], w1_ref[...], precision=hp, preferred_element_type=F32) + b1_ref[...]
    a = jnp.sin(f1_ref[...] * a)
    a = jnp.dot(a, w2_ref[...], precision=hp, preferred_element_type=F32) + b2_ref[...]
    a = jnp.sin(f2_ref[...] * a)
    hh = jnp.dot(a, w3_ref[...], precision=hp, preferred_element_type=F32)
    window = jnp.exp(-t_ref[...] * dec_ref[...])
    h_fwd = hh[:, 0:HY_W] * window
    h_bwd = hh[:, HY_W:2 * HY_W] * window
    row = lax.broadcasted_iota(jnp.int32, h_bwd.shape, 0)
    h_bwd = jnp.where(row == 0, 0.0, h_bwd)
    o_ref[:, 0:HY_W] = (h_fwd + h_bwd).astype(BF16)
    o_ref[:, HY_W:2 * HY_W] = (h_fwd - h_bwd).astype(BF16)


def _filters(seq, w1, b1, f1, w2, b2, f2, w3):
    t = jnp.linspace(0.0, 1.0, seq, dtype=F32)[:, None]
    w = 2.0 * math.pi * jnp.arange(seq, dtype=F32) / seq
    f = jnp.linspace(1e-4, FILT_BANDS - 1, FILT_BANDS, dtype=F32)
    zr = w[:, None] * f[None, :]
    feats = jnp.concatenate([t, jnp.cos(zr), -jnp.sin(zr),
                             jnp.zeros((seq, FILT_EMB_PAD - FILT_EMB), F32)], axis=-1)
    target = 1e-2
    decay = jnp.abs(jnp.linspace(math.log(target) / 1.5, math.log(target) / 0.3, HY_W,
                                 dtype=F32))[None, :]
    w1p = jnp.pad(w1, ((0, 0), (0, FILT_EMB_PAD - FILT_EMB), (0, 0)))
    vec = lambda a: a.reshape(DEPTH, 1, FILT_HIDDEN)
    return pl.pallas_call(
        _filter_kernel,
        out_shape=jax.ShapeDtypeStruct((DEPTH, seq, 2 * HY_W), BF16),
        grid=(DEPTH,),
        in_specs=[
            pl.BlockSpec((seq, FILT_EMB_PAD), lambda l: (0, 0)),
            pl.BlockSpec((seq, 1), lambda l: (0, 0)),
            pl.BlockSpec((1, HY_W), lambda l: (0, 0)),
            pl.BlockSpec((None, FILT_EMB_PAD, FILT_HIDDEN), lambda l: (l, 0, 0)),
            pl.BlockSpec((None, 1, FILT_HIDDEN), lambda l: (l, 0, 0)),
            pl.BlockSpec((None, 1, FILT_HIDDEN), lambda l: (l, 0, 0)),
            pl.BlockSpec((None, FILT_HIDDEN, FILT_HIDDEN), lambda l: (l, 0, 0)),
            pl.BlockSpec((None, 1, FILT_HIDDEN), lambda l: (l, 0, 0)),
            pl.BlockSpec((None, 1, FILT_HIDDEN), lambda l: (l, 0, 0)),
            pl.BlockSpec((None, FILT_HIDDEN, 2 * HY_W), lambda l: (l, 0, 0)),
        ],
        out_specs=pl.BlockSpec((None, seq, 2 * HY_W), lambda l: (l, 0, 0)),
        compiler_params=_params("parallel"),
        name=f"hyena_filter_{seq}",
    )(feats, t, decay, w1p, vec(b1), vec(f1), w2, vec(b2), vec(f2), w3)


def _dft_tables(seq):
    fc = DFT_CHUNK[seq]
    period = 4 * seq
    f = jnp.arange(seq, dtype=jnp.int32)[:, None]
    t = jnp.arange(seq, dtype=jnp.int32)[None, :]
    k = ((2 * f + 1) * t) & (period - 1)
    ang = k.astype(F32) * (2.0 * math.pi / period)
    cos = jnp.cos(ang).reshape(seq // fc, fc, seq)
    msin = (-jnp.sin(ang)).reshape(seq // fc, fc, seq)
    fwd = jnp.concatenate([cos, msin], axis=1)
    inv = jnp.swapaxes(fwd, 1, 2) * (1.0 / seq)
    return fwd.astype(BF16), inv.astype(BF16)


def _spectrum_kernel(a_ref, hs_ref, g_ref, *, fc):
    g_ref[0:fc, :] = _dot(a_ref[0:fc, :], hs_ref[:, 0:HY_W])
    g_ref[fc:2 * fc, :] = _dot(a_ref[fc:2 * fc, :], hs_ref[:, HY_W:2 * HY_W])


def _filter_spectrum(seq, fwd, hs):
    fc = DFT_CHUNK[seq]
    nch = seq // fc
    return pl.pallas_call(
        functools.partial(_spectrum_kernel, fc=fc),
        out_shape=jax.ShapeDtypeStruct((DEPTH, nch, 2 * fc, HY_W), F32),
        grid=(DEPTH, nch),
        in_specs=[
            pl.BlockSpec((None, 2 * fc, seq), lambda l, c: (c, 0, 0)),
            pl.BlockSpec((None, seq, 2 * HY_W), lambda l, c: (l, 0, 0)),
        ],
        out_specs=pl.BlockSpec((None, None, 2 * fc, HY_W), lambda l, c: (l, c, 0, 0)),
        compiler_params=_params("parallel", "parallel"),
        name=f"hyena_spectrum_{seq}",
    )(fwd, hs)


def _longconv_kernel(v_ref, a_ref, bt_ref, g_ref, bias_ref, o_ref, y_ref, acc_ref, *, fc):
    c = pl.program_id(1)
    v = v_ref[...]
    spec = _dot(a_ref[...], v)
    ur = spec[0:fc, :]
    ui = spec[fc:2 * fc, :]
    gr = g_ref[0:fc, :]
    gi = g_ref[fc:2 * fc, :]
    y_ref[0:fc, :] = (ur * gr - ui * gi).astype(BF16)
    y_ref[fc:2 * fc, :] = (ur * gi + ui * gr).astype(BF16)
    contrib = _dot(bt_ref[...], y_ref[...])

    @pl.when(c == 0)
    def _():
        acc_ref[...] = contrib

    @pl.when(c > 0)
    def _():
        acc_ref[...] += contrib

    @pl.when(c == pl.num_programs(1) - 1)
    def _():
        o_ref[...] = (acc_ref[...] + bias_ref[...] * v.astype(F32)).astype(BF16)


def _long_conv(seq, n_seq, row_block0, vh, fwd, inv, spectrum, hy_bias, layer):
    fc = DFT_CHUNK[seq]
    nch = seq // fc
    return pl.pallas_call(
        functools.partial(_longconv_kernel, fc=fc),
        out_shape=jax.ShapeDtypeStruct((n_seq * seq, HY_W), BF16),
        grid=(n_seq, nch),
        in_specs=[
            pl.BlockSpec((seq, HY_W), lambda b, c: (row_block0 + b, 0)),
            pl.BlockSpec((None, 2 * fc, seq), lambda b, c: (c, 0, 0)),
            pl.BlockSpec((None, seq, 2 * fc), lambda b, c: (c, 0, 0)),
            pl.BlockSpec((None, None, 2 * fc, HY_W), lambda b, c: (layer, c, 0, 0)),
            pl.BlockSpec((None, 1, HY_W), lambda b, c: (layer, 0, 0)),
        ],
        out_specs=pl.BlockSpec((seq, HY_W), lambda b, c: (b, 0)),
        scratch_shapes=[pltpu.VMEM((2 * fc, HY_W), BF16), pltpu.VMEM((seq, HY_W), F32)],
        compiler_params=_params("parallel", "arbitrary"),
        name=f"hyena_longconv_{seq}",
    )(vh, fwd, inv, spectrum, hy_bias.reshape(DEPTH, 1, HY_W))


def _softmax_pv(parts, sink):
    m = sink
    for lg, _ in parts:
        m = jnp.maximum(m, jnp.max(lg, axis=-1, keepdims=True))
    den = jnp.exp(sink - m)
    out = None
    for lg, val in parts:
        e = jnp.exp(lg - m)
        den = den + jnp.sum(e, axis=-1, keepdims=True)
        pv = _dot(e.astype(BF16), val)
        out = pv if out is None else out + pv
    return out / den


def _ctx_attn_kernel(sink_ref, q_ref, k_ref, v_ref, o_ref, *, layer):
    q = q_ref[...]
    k = k_ref[...]
    v = v_ref[...]
    outs = []
    for hd in range(N_HEADS):
        j = hd // GROUP
        qh = q[:, HEAD_DIM * hd:HEAD_DIM * (hd + 1)]
        kj = k[:, HEAD_DIM * j:HEAD_DIM * (j + 1)]
        vj = v[:, HEAD_DIM * j:HEAD_DIM * (j + 1)]
        outs.append(_softmax_pv([(_dot_nt(qh, kj), vj)], sink_ref[layer, hd]))
    o_ref[...] = jnp.concatenate(outs, axis=-1).astype(BF16)


def _ctx_attention(sink, q, k, v, layer):
    row = lambda b: (b, 0)
    return pl.pallas_call(
        functools.partial(_ctx_attn_kernel, layer=layer),
        out_shape=jax.ShapeDtypeStruct((N_CTX, ATTN_W), BF16),
        grid=(BATCH,),
        in_specs=[
            pl.BlockSpec(memory_space=pltpu.SMEM),
            pl.BlockSpec((SEQ, ATTN_W), row),
            pl.BlockSpec((SEQ, KV_W), row),
            pl.BlockSpec((SEQ, KV_W), row),
        ],
        out_specs=pl.BlockSpec((SEQ, ATTN_W), row),
        compiler_params=_params("parallel"),
        name="context_attention",
    )(sink, q, k, v)


def _lat_attn_kernel(sink_ref, q_ref, k_ref, v_ref, ck_ref, cv_ref, o_ref, *, layer):
    i = pl.program_id(1)
    start = i * QBLOCK
    ws = pl.multiple_of(jnp.clip(start - WINDOW, 0, DEC_SEQ - BAND), QBLOCK)
    rows = GROUP * QBLOCK
    r = lax.broadcasted_iota(jnp.int32, (rows, BAND), 0)
    qpos = start + (r & (QBLOCK - 1))
    kpos = ws + lax.broadcasted_iota(jnp.int32, (rows, BAND), 1)
    in_window = jnp.abs(qpos - kpos) <= WINDOW
    seg = lax.broadcasted_iota(jnp.int32, (rows, 1), 0) // QBLOCK

    q = q_ref[...]
    kb = k_ref[pl.ds(ws, BAND), :]
    vb = v_ref[pl.ds(ws, BAND), :]
    kc = ck_ref[...].astype(BF16)
    vc = cv_ref[...].astype(BF16)
    outs = []
    for j in range(N_KV_HEADS):
        heads = range(GROUP * j, GROUP * (j + 1))
        q4 = jnp.concatenate([q[:, HEAD_DIM * hd:HEAD_DIM * (hd + 1)] for hd in heads], axis=0)
        sink = jnp.zeros((rows, 1), F32)
        for g, hd in enumerate(heads):
            sink = jnp.where(seg == g, sink_ref[layer, hd], sink)
        cols = slice(HEAD_DIM * j, HEAD_DIM * (j + 1))
        l_ctx = _dot_nt(q4, kc[:, cols])
        l_band = jnp.where(in_window, _dot_nt(q4, kb[:, cols]), MASK_VALUE)
        o4 = _softmax_pv([(l_ctx, vc[:, cols]), (l_band, vb[:, cols])], sink)
        outs.extend(o4[QBLOCK * g:QBLOCK * (g + 1), :] for g in range(GROUP))
    o_ref[...] = jnp.concatenate(outs, axis=-1).astype(BF16)


def _lat_attention(sink, q, k, v, cache_k, cache_v, layer):
    nqb = DEC_SEQ // QBLOCK
    ctx_blocks = N_CTX // DEC_SEQ
    return pl.pallas_call(
        functools.partial(_lat_attn_kernel, layer=layer),
        out_shape=jax.ShapeDtypeStruct((N_LAT, ATTN_W), BF16),
        grid=(DEC_BATCH, nqb),
        in_specs=[
            pl.BlockSpec(memory_space=pltpu.SMEM),
            pl.BlockSpec((QBLOCK, ATTN_W), lambda b, i: (N_CTX // QBLOCK + b * nqb + i, 0)),
            pl.BlockSpec((DEC_SEQ, KV_W), lambda b, i: (ctx_blocks + b, 0)),
            pl.BlockSpec((DEC_SEQ, KV_W), lambda b, i: (ctx_blocks + b, 0)),
            pl.BlockSpec((None, None, PAST_LEN, KV_W), lambda b, i: (b, layer, 0, 0)),
            pl.BlockSpec((None, None, PAST_LEN, KV_W), lambda b, i: (b, layer, 0, 0)),
        ],
        out_specs=pl.BlockSpec((QBLOCK, ATTN_W), lambda b, i: (b * nqb + i, 0)),
        compiler_params=_params("parallel", "parallel"),
        name="latent_attention",
    )(sink, q, k, v, cache_k, cache_v)


MERGE_TILE = 512


def _merge_kernel(h_ref, atc_ref, atl_ref, ybc_ref, ybl_ref, x0_ref, x_ref, mod_ref,
                  wga_ref, wgb_ref, wpa_ref, wpb_ref, wo_ref, g_ref, b_ref,
                  xo_ref, ho_ref, mg_ref):
    is_ctx = pl.program_id(0) < N_CTX // MERGE_TILE
    h = h_ref[...]
    attn = jnp.where(is_ctx, atc_ref[...], atl_ref[...])
    conv = jnp.where(is_ctx, ybc_ref[...], ybl_ref[...])
    hy = (x0_ref[...].astype(F32) * conv.astype(F32)).astype(BF16)
    for j in range(D_MODEL // 256):
        cs = slice(256 * j, 256 * (j + 1))
        ga = jax.nn.sigmoid(_dot(h, wga_ref[:, cs]))
        gb = jax.nn.sigmoid(_dot(h, wgb_ref[:, cs]))
        mg_ref[:, cs] = (ga * _dot(attn, wpa_ref[:, cs]) + gb * _dot(hy, wpb_ref[:, cs])).astype(BF16)
    sub = _dot(mg_ref[...], wo_ref[...])
    g1 = mod_ref[:, 2 * D_MODEL:3 * D_MODEL]
    sh2 = mod_ref[:, 3 * D_MODEL:4 * D_MODEL]
    sc2 = mod_ref[:, 4 * D_MODEL:5 * D_MODEL]
    x = _layer_norm(DEEPNORM_ALPHA * x_ref[...] + g1 * sub, g_ref[...], b_ref[...])
    xo_ref[...] = x
    ho_ref[...] = (x * (1.0 + sc2) + sh2).astype(BF16)


def _merge(h, atc, atl, ybc, ybl, x0, x, mods, wga, wgb, wpa, wpb, wo, ln_g, ln_b, layer):
    t = MERGE_TILE
    nct = N_CTX // t
    row = lambda m: (m, 0)
    ctx_row = lambda m: (jnp.minimum(m, nct - 1), 0)
    lat_row = lambda m: (jnp.maximum(m - nct, 0), 0)
    const = lambda m: (0, 0)
    return pl.pallas_call(
        _merge_kernel,
        out_shape=(jax.ShapeDtypeStruct((N_TOK, D_MODEL), F32),
                   jax.ShapeDtypeStruct((N_TOK, D_MODEL), BF16)),
        grid=(N_TOK // t,),
        in_specs=[
            pl.BlockSpec((t, D_MODEL), row),
            pl.BlockSpec((t, ATTN_W), ctx_row),
            pl.BlockSpec((t, ATTN_W), lat_row),
            pl.BlockSpec((t, HY_W), ctx_row),
            pl.BlockSpec((t, HY_W), lat_row),
            pl.BlockSpec((t, HY_W), row),
            pl.BlockSpec((t, D_MODEL), row),
            _mod_spec(layer, t),
            pl.BlockSpec((D_MODEL, D_MODEL), const),
            pl.BlockSpec((D_MODEL, D_MODEL), const),
            pl.BlockSpec((ATTN_W, D_MODEL), const),
            pl.BlockSpec((HY_W, D_MODEL), const),
            pl.BlockSpec((D_MODEL, D_MODEL), const),
            _layer_spec((1, D_MODEL), layer),
            _layer_spec((1, D_MODEL), layer),
        ],
        out_specs=(pl.BlockSpec((t, D_MODEL), row), pl.BlockSpec((t, D_MODEL), row)),
        scratch_shapes=[pltpu.VMEM((t, D_MODEL), BF16)],
        compiler_params=_params("parallel"),
        name="merge_ln1",
    )(h, atc, atl, ybc, ybl, x0, x, mods, wga, wgb, wpa, wpb, wo, ln_g, ln_b)


UP_TILE = 2048


def _up_kernel(h_ref, wg_ref, wv_ref, cw_ref, cb_ref, o_ref):
    is_first, is_last = _seq_edges(pl.program_id(0), UP_TILE, FF_CHUNK)
    h = h_ref[...]
    gate = _dwconv3(_dot(h, wg_ref[...]), is_first, is_last, cw_ref, cb_ref, 0, FF_CHUNK)
    gelu = 0.5 * gate * (1.0 + lax.erf(gate * math.sqrt(0.5)))
    o_ref[...] = (gelu * _dot(h, wv_ref[...])).astype(BF16)


def _ffn_up(h, w_up, conv_w, conv_b, layer):
    nff = D_FF // FF_CHUNK
    return pl.pallas_call(
        _up_kernel,
        out_shape=jax.ShapeDtypeStruct((N_TOK, D_FF), BF16),
        grid=(N_TOK // UP_TILE, nff),
        in_specs=[
            pl.BlockSpec((UP_TILE, D_MODEL), lambda m, f: (m, 0)),
            pl.BlockSpec((D_MODEL, FF_CHUNK), lambda m, f: (0, f)),
            pl.BlockSpec((D_MODEL, FF_CHUNK), lambda m, f: (0, nff + f)),
            pl.BlockSpec((None, 3, FF_CHUNK), lambda m, f: (layer, 0, f)),
            pl.BlockSpec((None, 1, FF_CHUNK), lambda m, f: (layer, 0, f)),
        ],
        out_specs=pl.BlockSpec((UP_TILE, FF_CHUNK), lambda m, f: (m, f)),
        compiler_params=_params("parallel", "parallel"),
        name="ffn_up",
    )(h, w_up, w_up, conv_w, conv_b)


DOWN_TILE = 512


def _down_kernel(a_ref, w_ref, x_ref, mod_ref, nmod_ref, g_ref, b_ref, xo_ref, ho_ref):
    sub = _dot(a_ref[...], w_ref[...])
    g2 = mod_ref[:, 5 * D_MODEL:6 * D_MODEL]
    x = _layer_norm(DEEPNORM_ALPHA * x_ref[...] + g2 * sub, g_ref[...], b_ref[...])
    xo_ref[...] = x
    sh = nmod_ref[:, 0:D_MODEL]
    sc = nmod_ref[:, D_MODEL:2 * D_MODEL]
    ho_ref[...] = (x * (1.0 + sc) + sh).astype(BF16)


def _ffn_down(act, w_down, x, mods, ln_g, ln_b, layer):
    t = DOWN_TILE
    row = lambda m: (m, 0)
    next_layer = min(layer + 1, DEPTH - 1)
    return pl.pallas_call(
        _down_kernel,
        out_shape=(jax.ShapeDtypeStruct((N_TOK, D_MODEL), F32),
                   jax.ShapeDtypeStruct((N_TOK, D_MODEL), BF16)),
        grid=(N_TOK // t,),
        in_specs=[
            pl.BlockSpec((t, D_FF), row),
            pl.BlockSpec((D_FF, D_MODEL), lambda m: (0, 0)),
            pl.BlockSpec((t, D_MODEL), row),
            _mod_spec(layer, t),
            _mod_spec(next_layer, t),
            _layer_spec((1, D_MODEL), layer),
            _layer_spec((1, D_MODEL), layer),
        ],
        out_specs=(pl.BlockSpec((t, D_MODEL), row), pl.BlockSpec((t, D_MODEL), row)),
        compiler_params=_params("parallel"),
        name="ffn_down_ln2",
    )(act, w_down, x, mods, mods, ln_g, ln_b)


def kernel(x_prompt, x_sample, cache_k, cache_v, c, c_ctx, w_ada, b_ada, w_in, attn_sink,
           hy_conv_w, hy_conv_b, filt_w1, filt_b1, filt_freq1, filt_w2, filt_b2, filt_freq2,
           filt_w3, hy_bias, w_pa, w_pb, w_out, ln1_g, ln1_b, w_up, ffn_conv_w, ffn_conv_b,
           w_down, ln2_g, ln2_b):
    cond = jnp.concatenate([c_ctx[None, :], c,
                            jnp.zeros((N_MOD_ROWS - 1 - DEC_BATCH, D_MODEL), F32)], axis=0)
    mods = _modulation(cond, w_ada, b_ada).reshape(DEPTH, N_MOD_ROWS, 1, 6 * D_MODEL)

    x, h = _prologue(x_prompt.reshape(N_CTX, D_MODEL), x_sample.reshape(N_LAT, D_MODEL), mods)

    cos_tab, sin_tab = _rope_tables()
    conv_args = {}
    for seq in (SEQ, DEC_SEQ):
        fwd, inv = _dft_tables(seq)
        hs = _filters(seq, filt_w1, filt_b1, filt_freq1, filt_w2, filt_b2, filt_freq2, filt_w3)
        conv_args[seq] = (fwd, inv, _filter_spectrum(seq, fwd, hs))

    cache_k = cache_k.reshape(DEC_BATCH, DEPTH, PAST_LEN, KV_W)
    cache_v = cache_v.reshape(DEC_BATCH, DEPTH, PAST_LEN, KV_W)
    hy_conv_b3 = hy_conv_b.reshape(DEPTH, 1, 3 * HY_W)
    ffn_conv_b3 = ffn_conv_b.reshape(DEPTH, 1, D_FF)
    ln1_g3, ln1_b3 = ln1_g.reshape(DEPTH, 1, D_MODEL), ln1_b.reshape(DEPTH, 1, D_MODEL)
    ln2_g3, ln2_b3 = ln2_g.reshape(DEPTH, 1, D_MODEL), ln2_b.reshape(DEPTH, 1, D_MODEL)

    hy0 = QKV_W
    ga0 = hy0 + 3 * HY_W
    gb0 = ga0 + D_MODEL
    keys, values = [], []
    for l in range(DEPTH):
        w_in_l = w_in[l]
        w_qkv = w_in_l[:, 0:hy0].astype(BF16)
        w_hy = w_in_l[:, hy0:ga0].astype(BF16)
        w_ga = w_in_l[:, ga0:gb0].astype(BF16)
        w_gb = w_in_l[:, gb0:].astype(BF16)

        q, k, v, kf, vf = _qkv_proj(h, w_qkv, cos_tab, sin_tab)
        keys.append(kf[:N_CTX])
        values.append(vf[:N_CTX])
        x0, vh = _hy_proj(h, w_hy, hy_conv_w, hy_conv_b3, l)

        ybc = _long_conv(SEQ, BATCH, 0, vh, *conv_args[SEQ], hy_bias, l)
        ybl = _long_conv(DEC_SEQ, DEC_BATCH, N_CTX // DEC_SEQ, vh, *conv_args[DEC_SEQ], hy_bias, l)
        atc = _ctx_attention(attn_sink, q, k, v, l)
        atl = _lat_attention(attn_sink, q, k, v, cache_k, cache_v, l)

        x, h2 = _merge(h, atc, atl, ybc, ybl, x0, x, mods, w_ga, w_gb,
                       w_pa[l].astype(BF16), w_pb[l].astype(BF16), w_out[l].astype(BF16),
                       ln1_g3, ln1_b3, l)
        act = _ffn_up(h2, w_up[l].astype(BF16), ffn_conv_w, ffn_conv_b3, l)
        x, h = _ffn_down(act, w_down[l].astype(BF16), x, mods, ln2_g3, ln2_b3, l)

    y_prompt = x[:N_CTX].reshape(BATCH, SEQ, D_MODEL)
    y_sample = x[N_CTX:].reshape(DEC_BATCH, DEC_SEQ, D_MODEL)
    to_cache = lambda ts: jnp.stack(ts, axis=1).reshape(BATCH, DEPTH, SEQ, N_KV_HEADS, HEAD_DIM)
    new_k = to_cache([t.reshape(BATCH, SEQ, KV_W) for t in keys])
    new_v = to_cache([t.reshape(BATCH, SEQ, KV_W) for t in values])
    return (y_prompt, y_sample, new_k, new_v)
```

```python
import functools
import math

import jax
import jax.numpy as jnp
from jax import lax
from jax.experimental import pallas as pl
from jax.experimental.pallas import tpu as pltpu

D_MODEL = 1024
BATCH = 16
SEQ = 256
DEPTH = 4
DEC_BATCH = 4
DEC_SEQ = 2048
PAST_LEN = 512
GRID_W = 64
HEAD_DIM = 64
N_HEADS = 8
N_KV_HEADS = 2
GROUP = N_HEADS // N_KV_HEADS
ATTN_W = N_HEADS * HEAD_DIM
KV_W = N_KV_HEADS * HEAD_DIM
WINDOW = 128
QBLOCK = 128
HY_W = 512
FILT_EMB = 33
FILT_EMB_PAD = 40
FILT_BANDS = (FILT_EMB - 1) // 2
FILT_HIDDEN = 64
D_FF = 2816
ROPE_BASE = 10000.0
LN_EPS = 1e-5
DEEPNORM_ALPHA = (2 * DEPTH) ** 0.25

N_CTX = BATCH * SEQ
N_LAT = DEC_BATCH * DEC_SEQ
N_TOK = N_CTX + N_LAT
N_MOD_ROWS = 8
QKV_W = ATTN_W + 2 * KV_W
BAND = QBLOCK + 2 * WINDOW
MASK_VALUE = -1e30

F32 = jnp.float32
BF16 = jnp.bfloat16

VMEM_LIMIT_BYTES = 56 * 1024 * 1024
MXU_COLS = 256
ROW_TILE = 512
SUB_ROWS = SEQ
HALO = 16
DFT_CHUNK = {SEQ: 256, DEC_SEQ: 512}
RESIDENT = dict(pipeline_mode=pl.Buffered(1))


def _params(*semantics):
    return pltpu.CompilerParams(dimension_semantics=semantics, vmem_limit_bytes=VMEM_LIMIT_BYTES)


def _dot(a, b):
    return jnp.dot(a, b, preferred_element_type=F32)


def _dot_nt(a, b):
    return lax.dot_general(a, b, (((1,), (1,)), ((), ())), preferred_element_type=F32)


def _mod_row(tile_rows):
    n_ctx_tiles = N_CTX // tile_rows
    tiles_per_seq = DEC_SEQ // tile_rows

    def fn(m):
        return jnp.where(m < n_ctx_tiles, 0, 1 + (m - n_ctx_tiles) // tiles_per_seq)

    return fn


def _mod_spec(layer, tile_rows):
    grp = _mod_row(tile_rows)
    return pl.BlockSpec((None, None, 1, 6 * D_MODEL), lambda m, *_: (layer, grp(m), 0, 0))


def _layer_spec(shape, layer):
    zeros = (0,) * len(shape)
    return pl.BlockSpec((None,) + tuple(shape), lambda *_: (layer,) + zeros)


def _resident_spec(shape, block_index=None):
    block_index = block_index or (0,) * len(shape)
    return pl.BlockSpec(tuple(shape), lambda *_: block_index, **RESIDENT)


def _layer_norm(y, g, b):
    mu = jnp.mean(y, axis=-1, keepdims=True)
    yc = y - mu
    var = jnp.mean(yc * yc, axis=-1, keepdims=True)
    return yc * lax.rsqrt(var + LN_EPS) * g + b


def _tile_edges(m, tile_rows):
    n_ctx_tiles = N_CTX // tile_rows
    tiles_per_seq = DEC_SEQ // tile_rows
    is_ctx = m < n_ctx_tiles
    lat_pos = (m - n_ctx_tiles) % tiles_per_seq
    return (is_ctx, jnp.logical_or(is_ctx, lat_pos == 0),
            jnp.logical_or(is_ctx, lat_pos == tiles_per_seq - 1))


def _halo_specs(tile_rows, n_cols):
    tb = tile_rows // HALO
    last = N_TOK // HALO - 1
    return (pl.BlockSpec((HALO, n_cols), lambda m: (jnp.maximum(m * tb - 1, 0), 0)),
            pl.BlockSpec((HALO, n_cols), lambda m: (jnp.minimum((m + 1) * tb, last), 0)))


def _fill_extended(hx_ref, hp_ref, h_ref, hn_ref):
    rows = h_ref.shape[0]
    hx_ref[0:HALO, :] = hp_ref[...]
    hx_ref[HALO:HALO + rows, :] = h_ref[...]
    hx_ref[HALO + rows:2 * HALO + rows, :] = hn_ref[...]


def _sub_edges(j, n_sub, is_ctx, seq_start, seq_end):
    return (seq_start if j == 0 else is_ctx), (seq_end if j == n_sub - 1 else is_ctx)


def _dwconv3(z, zero_top, zero_bot, w_ref, b_ref, cs):
    n = z.shape[0] - 2 * HALO
    top = jnp.where(zero_top, 0.0, z[HALO - 8:HALO])
    bot = jnp.where(zero_bot, 0.0, z[HALO + n:HALO + n + 8])
    z = jnp.concatenate([z[0:HALO - 8], top, z[HALO:HALO + n], bot, z[HALO + n + 8:]], axis=0)
    prev = pltpu.roll(z, 1, 0)[HALO:HALO + n]
    nxt = pltpu.roll(z, z.shape[0] - 1, 0)[HALO:HALO + n]
    return (prev * w_ref[0:1, cs] + z[HALO:HALO + n] * w_ref[1:2, cs] + nxt * w_ref[2:3, cs]
            + b_ref[:, cs])


def _mod_kernel(cond_ref, w_ref, b_ref, o_ref):
    c = cond_ref[...]
    s = (c * jax.nn.sigmoid(c)).astype(BF16)
    o_ref[...] = _dot(s, w_ref[...].astype(BF16)) + b_ref[...]


def _modulation(cond, w_ada, b_ada):
    n_col = 6 * D_MODEL // D_MODEL
    return pl.pallas_call(
        _mod_kernel,
        out_shape=jax.ShapeDtypeStruct((DEPTH, N_MOD_ROWS, 6 * D_MODEL), F32),
        grid=(DEPTH, n_col),
        in_specs=[
            pl.BlockSpec((N_MOD_ROWS, D_MODEL), lambda l, j: (0, 0)),
            pl.BlockSpec((None, D_MODEL, D_MODEL), lambda l, j: (l, 0, j)),
            pl.BlockSpec((None, 1, D_MODEL), lambda l, j: (l, 0, j)),
        ],
        out_specs=pl.BlockSpec((None, N_MOD_ROWS, D_MODEL), lambda l, j: (l, 0, j)),
        compiler_params=_params("parallel", "parallel"),
        name="modulation",
    )(cond, w_ada, b_ada.reshape(DEPTH, 1, 6 * D_MODEL))


PRO_TILE = 1024


def _prologue_kernel(xp_ref, xs_ref, mod_ref, x_ref, h_ref):
    m = pl.program_id(0)

    def emit(x):
        x_ref[...] = x
        sh = mod_ref[:, 0:D_MODEL]
        sc = mod_ref[:, D_MODEL:2 * D_MODEL]
        h_ref[...] = (x * (1.0 + sc) + sh).astype(BF16)

    @pl.when(m < N_CTX // PRO_TILE)
    def _():
        emit(xp_ref[...])

    @pl.when(m >= N_CTX // PRO_TILE)
    def _():
        emit(xs_ref[...])


def _prologue(xp, xs, mods):
    nct = N_CTX // PRO_TILE
    return pl.pallas_call(
        _prologue_kernel,
        out_shape=(jax.ShapeDtypeStruct((N_TOK, D_MODEL), F32),
                   jax.ShapeDtypeStruct((N_TOK, D_MODEL), BF16)),
        grid=(N_TOK // PRO_TILE,),
        in_specs=[
            pl.BlockSpec((PRO_TILE, D_MODEL), lambda m: (jnp.minimum(m, nct - 1), 0)),
            pl.BlockSpec((PRO_TILE, D_MODEL), lambda m: (jnp.maximum(m - nct, 0), 0)),
            _mod_spec(0, PRO_TILE),
        ],
        out_specs=(pl.BlockSpec((PRO_TILE, D_MODEL), lambda m: (m, 0)),
                   pl.BlockSpec((PRO_TILE, D_MODEL), lambda m: (m, 0))),
        compiler_params=_params("parallel"),
        name="prologue",
    )(xp, xs, mods)


ROPE_LANES = 2 * HEAD_DIM


def _in_proj_kernel(h_ref, hp_ref, hn_ref, wqkv_ref, why_ref, cos_ref, sin_ref, cw_ref, cb_ref,
                    q_ref, k_ref, v_ref, kf_ref, vf_ref, x0_ref, vh_ref, hx_ref):
    n_sub = ROW_TILE // SUB_ROWS
    ext = SUB_ROWS + 2 * HALO
    is_ctx, seq_start, seq_end = _tile_edges(pl.program_id(0), ROW_TILE)
    _fill_extended(hx_ref, hp_ref, h_ref, hn_ref)
    lane = lax.broadcasted_iota(jnp.int32, (SUB_ROWS, ROPE_LANES), 1)
    first_half = (lane & (HEAD_DIM // 4)) == 0
    scale = HEAD_DIM ** -0.5
    for j in range(n_sub):
        rs = slice(SUB_ROWS * j, SUB_ROWS * (j + 1))
        cos = cos_ref[rs, :]
        sin = sin_ref[rs, :]

        def rope(x):
            partner = jnp.where(first_half, pltpu.roll(x, ROPE_LANES - HEAD_DIM // 4, 1),
                                pltpu.roll(x, HEAD_DIM // 4, 1))
            return x * cos + partner * sin

        hm = h_ref[rs, :]
        for jj in range(ATTN_W // MXU_COLS):
            qq = _dot(hm, wqkv_ref[:, MXU_COLS * jj:MXU_COLS * (jj + 1)])
            for i in range(MXU_COLS // ROPE_LANES):
                c0 = MXU_COLS * jj + ROPE_LANES * i
                q_ref[rs, c0:c0 + ROPE_LANES] = (
                    rope(qq[:, ROPE_LANES * i:ROPE_LANES * (i + 1)]) * scale).astype(BF16)
        kv = _dot(hm, wqkv_ref[:, ATTN_W:QKV_W])
        k = kv[:, 0:KV_W]
        v = kv[:, KV_W:2 * KV_W]
        kf_ref[rs, :] = k
        vf_ref[rs, :] = v
        k_ref[rs, :] = rope(k).astype(BF16)
        v_ref[rs, :] = v.astype(BF16)

        zero_top, zero_bot = _sub_edges(j, n_sub, is_ctx, seq_start, seq_end)
        hx = hx_ref[SUB_ROWS * j:SUB_ROWS * j + ext, :]

        def proj_conv(c0):
            cs = slice(c0, c0 + MXU_COLS)
            return _dwconv3(_dot(hx, why_ref[:, cs]), zero_top, zero_bot, cw_ref, cb_ref, cs)

        for jj in range(HY_W // MXU_COLS):
            c0 = MXU_COLS * jj
            x0_ref[rs, c0:c0 + MXU_COLS] = proj_conv(c0).astype(BF16)
            x1 = proj_conv(HY_W + c0)
            u = proj_conv(2 * HY_W + c0)
            vh_ref[rs, c0:c0 + MXU_COLS] = (x1 * u).astype(BF16)


def _in_proj(h, w_qkv, w_hy, cos_tab, sin_tab, conv_w, conv_b, layer):
    t = ROW_TILE
    nct = N_CTX // t
    tps = DEC_SEQ // t
    tab = lambda m: (jnp.where(m < nct, 0, 1 + (m - nct) % tps), 0, 0)
    row = lambda m: (m, 0)
    bf = lambda w: jax.ShapeDtypeStruct((N_TOK, w), BF16)
    return pl.pallas_call(
        _in_proj_kernel,
        out_shape=(bf(ATTN_W), bf(KV_W), bf(KV_W),
                   jax.ShapeDtypeStruct((N_TOK, KV_W), F32),
                   jax.ShapeDtypeStruct((N_TOK, KV_W), F32),
                   bf(HY_W), bf(HY_W)),
        grid=(N_TOK // t,),
        in_specs=[
            pl.BlockSpec((t, D_MODEL), row),
            *_halo_specs(t, D_MODEL),
            _resident_spec((D_MODEL, QKV_W)),
            _resident_spec((D_MODEL, 3 * HY_W)),
            pl.BlockSpec((None, t, ROPE_LANES), tab),
            pl.BlockSpec((None, t, ROPE_LANES), tab),
            _layer_spec((3, 3 * HY_W), layer),
            _layer_spec((1, 3 * HY_W), layer),
        ],
        out_specs=(pl.BlockSpec((t, ATTN_W), row),
                   pl.BlockSpec((t, KV_W), row),
                   pl.BlockSpec((t, KV_W), row),
                   pl.BlockSpec((t, KV_W), row),
                   pl.BlockSpec((t, KV_W), row),
                   pl.BlockSpec((t, HY_W), row),
                   pl.BlockSpec((t, HY_W), row)),
        scratch_shapes=[pltpu.VMEM((t + 2 * HALO, D_MODEL), BF16)],
        compiler_params=_params("parallel"),
        name="in_proj",
    )(h, h, h, w_qkv, w_hy, cos_tab, sin_tab, conv_w, conv_b)


def _rope_tables():
    rows = DEC_SEQ // GRID_W
    r, col = jnp.meshgrid(jnp.arange(rows), jnp.arange(GRID_W), indexing='ij')
    pos = jnp.stack([r.reshape(-1), col.reshape(-1)], axis=-1).astype(F32)
    half = HEAD_DIM // 2
    inv_freq = 1.0 / (ROPE_BASE ** (jnp.arange(0, half, 2, dtype=F32) / half))
    ang = pos[:, :, None] * inv_freq
    ang = jnp.stack([ang, ang], axis=-2).reshape(DEC_SEQ, HEAD_DIM)
    cos = jnp.tile(jnp.cos(ang), (1, ROPE_LANES // HEAD_DIM))
    sin = jnp.tile(jnp.sin(ang), (1, ROPE_LANES // HEAD_DIM))
    lane = jnp.arange(ROPE_LANES)
    sin = jnp.where((lane & (HEAD_DIM // 4)) == 0, -sin, sin)
    n_slab = DEC_SEQ // ROW_TILE
    cos = jnp.concatenate([jnp.ones((1, ROW_TILE, ROPE_LANES), F32),
                           cos.reshape(n_slab, ROW_TILE, ROPE_LANES)], axis=0)
    sin = jnp.concatenate([jnp.zeros((1, ROW_TILE, ROPE_LANES), F32),
                           sin.reshape(n_slab, ROW_TILE, ROPE_LANES)], axis=0)
    return cos, sin


def _filter_kernel(feat_ref, t_ref, dec_ref, w1_ref, b1_ref, f1_ref, w2_ref, b2_ref, f2_ref,
                   w3_ref, o_ref):
    hp = lax.Precision.HIGHEST
    a = jnp.dot(feat_ref[...], w1_ref[...], precision=hp, preferred_element_type=F32) + b1_ref[...]
    a = jnp.sin(f1_ref[...] * a)
    a = jnp.dot(a, w2_ref[...], precision=hp, preferred_element_type=F32) + b2_ref[...]
    a = jnp.sin(f2_ref[...] * a)
    hh = jnp.dot(a, w3_ref[...], precision=hp, preferred_element_type=F32)
    window = jnp.exp(-t_ref[...] * dec_ref[...])
    h_fwd = hh[:, 0:HY_W] * window
    h_bwd = hh[:, HY_W:2 * HY_W] * window
    row = lax.broadcasted_iota(jnp.int32, h_bwd.shape, 0)
    h_bwd = jnp.where(row == 0, 0.0, h_bwd)
    o_ref[:, 0:HY_W] = (h_fwd + h_bwd).astype(BF16)
    o_ref[:, HY_W:2 * HY_W] = (h_fwd - h_bwd).astype(BF16)


def _filters(seq, w1, b1, f1, w2, b2, f2, w3):
    t = jnp.linspace(0.0, 1.0, seq, dtype=F32)[:, None]
    w = 2.0 * math.pi * jnp.arange(seq, dtype=F32) / seq
    f = jnp.linspace(1e-4, FILT_BANDS - 1, FILT_BANDS, dtype=F32)
    zr = w[:, None] * f[None, :]
    feats = jnp.concatenate([t, jnp.cos(zr), -jnp.sin(zr),
                             jnp.zeros((seq, FILT_EMB_PAD - FILT_EMB), F32)], axis=-1)
    target = 1e-2
    decay = jnp.abs(jnp.linspace(math.log(target) / 1.5, math.log(target) / 0.3, HY_W,
                                 dtype=F32))[None, :]
    w1p = jnp.pad(w1, ((0, 0), (0, FILT_EMB_PAD - FILT_EMB), (0, 0)))
    vec = lambda a: a.reshape(DEPTH, 1, FILT_HIDDEN)
    return pl.pallas_call(
        _filter_kernel,
        out_shape=jax.ShapeDtypeStruct((DEPTH, seq, 2 * HY_W), BF16),
        grid=(DEPTH,),
        in_specs=[
            pl.BlockSpec((seq, FILT_EMB_PAD), lambda l: (0, 0)),
            pl.BlockSpec((seq, 1), lambda l: (0, 0)),
            pl.BlockSpec((1, HY_W), lambda l: (0, 0)),
            pl.BlockSpec((None, FILT_EMB_PAD, FILT_HIDDEN), lambda l: (l, 0, 0)),
            pl.BlockSpec((None, 1, FILT_HIDDEN), lambda l: (l, 0, 0)),
            pl.BlockSpec((None, 1, FILT_HIDDEN), lambda l: (l, 0, 0)),
            pl.BlockSpec((None, FILT_HIDDEN, FILT_HIDDEN), lambda l: (l, 0, 0)),
            pl.BlockSpec((None, 1, FILT_HIDDEN), lambda l: (l, 0, 0)),
            pl.BlockSpec((None, 1, FILT_HIDDEN), lambda l: (l, 0, 0)),
            pl.BlockSpec((None, FILT_HIDDEN, 2 * HY_W), lambda l: (l, 0, 0)),
        ],
        out_specs=pl.BlockSpec((None, seq, 2 * HY_W), lambda l: (l, 0, 0)),
        compiler_params=_params("parallel"),
        name=f"hyena_filter_{seq}",
    )(feats, t, decay, w1p, vec(b1), vec(f1), w2, vec(b2), vec(f2), w3)


def _dft_tables(seq):
    fc = DFT_CHUNK[seq]
    period = 4 * seq
    f = jnp.arange(seq, dtype=jnp.int32)[:, None]
    t = jnp.arange(seq, dtype=jnp.int32)[None, :]
    k = ((2 * f + 1) * t) & (period - 1)
    ang = k.astype(F32) * (2.0 * math.pi / period)
    cos = jnp.cos(ang).reshape(seq // fc, fc, seq)
    msin = (-jnp.sin(ang)).reshape(seq // fc, fc, seq)
    fwd = jnp.concatenate([cos, msin], axis=1)
    inv = jnp.swapaxes(fwd, 1, 2) * (1.0 / seq)
    return fwd.astype(BF16), inv.astype(BF16)


def _spectrum_kernel(a_ref, hs_ref, g_ref, *, fc):
    g_ref[0:fc, :] = _dot(a_ref[0:fc, :], hs_ref[:, 0:HY_W])
    g_ref[fc:2 * fc, :] = _dot(a_ref[fc:2 * fc, :], hs_ref[:, HY_W:2 * HY_W])


def _filter_spectrum(seq, fwd, hs):
    fc = DFT_CHUNK[seq]
    nch = seq // fc
    return pl.pallas_call(
        functools.partial(_spectrum_kernel, fc=fc),
        out_shape=jax.ShapeDtypeStruct((DEPTH, nch, 2 * fc, HY_W), F32),
        grid=(DEPTH, nch),
        in_specs=[
            pl.BlockSpec((None, 2 * fc, seq), lambda l, c: (c, 0, 0)),
            pl.BlockSpec((None, seq, 2 * HY_W), lambda l, c: (l, 0, 0)),
        ],
        out_specs=pl.BlockSpec((None, None, 2 * fc, HY_W), lambda l, c: (l, c, 0, 0)),
        compiler_params=_params("parallel", "parallel"),
        name=f"hyena_spectrum_{seq}",
    )(fwd, hs)


def _longconv_kernel(v_ref, a_ref, bt_ref, g_ref, bias_ref, o_ref, y_ref, acc_ref, *, fc):
    c = pl.program_id(1)
    v = v_ref[...]
    spec = _dot(a_ref[...], v)
    ur = spec[0:fc, :]
    ui = spec[fc:2 * fc, :]
    gr = g_ref[0:fc, :]
    gi = g_ref[fc:2 * fc, :]
    y_ref[0:fc, :] = (ur * gr - ui * gi).astype(BF16)
    y_ref[fc:2 * fc, :] = (ur * gi + ui * gr).astype(BF16)
    contrib = _dot(bt_ref[...], y_ref[...])

    @pl.when(c == 0)
    def _():
        acc_ref[...] = contrib

    @pl.when(c > 0)
    def _():
        acc_ref[...] += contrib

    @pl.when(c == pl.num_programs(1) - 1)
    def _():
        o_ref[...] = (acc_ref[...] + bias_ref[...] * v.astype(F32)).astype(BF16)


def _long_conv(seq, n_seq, row_block0, vh, fwd, inv, spectrum, hy_bias, layer):
    fc = DFT_CHUNK[seq]
    nch = seq // fc
    return pl.pallas_call(
        functools.partial(_longconv_kernel, fc=fc),
        out_shape=jax.ShapeDtypeStruct((n_seq * seq, HY_W), BF16),
        grid=(n_seq, nch),
        in_specs=[
            pl.BlockSpec((seq, HY_W), lambda b, c: (row_block0 + b, 0)),
            pl.BlockSpec((None, 2 * fc, seq), lambda b, c: (c, 0, 0)),
            pl.BlockSpec((None, seq, 2 * fc), lambda b, c: (c, 0, 0)),
            pl.BlockSpec((None, None, 2 * fc, HY_W), lambda b, c: (layer, c, 0, 0)),
            pl.BlockSpec((None, 1, HY_W), lambda b, c: (layer, 0, 0)),
        ],
        out_specs=pl.BlockSpec((seq, HY_W), lambda b, c: (b, 0)),
        scratch_shapes=[pltpu.VMEM((2 * fc, HY_W), BF16), pltpu.VMEM((seq, HY_W), F32)],
        compiler_params=_params("parallel", "arbitrary"),
        name=f"hyena_longconv_{seq}",
    )(vh, fwd, inv, spectrum, hy_bias.reshape(DEPTH, 1, HY_W))


def _softmax_pv(parts, sink):
    m = sink
    for lg, _ in parts:
        m = jnp.maximum(m, jnp.max(lg, axis=-1, keepdims=True))
    den = jnp.exp(sink - m)
    out = None
    for lg, val in parts:
        e = jnp.exp(lg - m)
        den = den + jnp.sum(e, axis=-1, keepdims=True)
        pv = _dot(e.astype(BF16), val)
        out = pv if out is None else out + pv
    return out / den


def _ctx_attn_kernel(sink_ref, q_ref, k_ref, v_ref, o_ref, *, layer):
    q = q_ref[...]
    k = k_ref[...]
    v = v_ref[...]
    outs = []
    for hd in range(N_HEADS):
        j = hd // GROUP
        qh = q[:, HEAD_DIM * hd:HEAD_DIM * (hd + 1)]
        kj = k[:, HEAD_DIM * j:HEAD_DIM * (j + 1)]
        vj = v[:, HEAD_DIM * j:HEAD_DIM * (j + 1)]
        outs.append(_softmax_pv([(_dot_nt(qh, kj), vj)], sink_ref[layer, hd]))
    o_ref[...] = jnp.concatenate(outs, axis=-1).astype(BF16)


def _ctx_attention(sink, q, k, v, layer):
    row = lambda b: (b, 0)
    return pl.pallas_call(
        functools.partial(_ctx_attn_kernel, layer=layer),
        out_shape=jax.ShapeDtypeStruct((N_CTX, ATTN_W), BF16),
        grid=(BATCH,),
        in_specs=[
            pl.BlockSpec(memory_space=pltpu.SMEM),
            pl.BlockSpec((SEQ, ATTN_W), row),
            pl.BlockSpec((SEQ, KV_W), row),
            pl.BlockSpec((SEQ, KV_W), row),
        ],
        out_specs=pl.BlockSpec((SEQ, ATTN_W), row),
        compiler_params=_params("parallel"),
        name="context_attention",
    )(sink, q, k, v)


def _lat_attn_kernel(sink_ref, q_ref, k_ref, v_ref, ck_ref, cv_ref, o_ref, *, layer):
    i = pl.program_id(1)
    start = i * QBLOCK
    ws = pl.multiple_of(jnp.clip(start - WINDOW, 0, DEC_SEQ - BAND), QBLOCK)
    rows = GROUP * QBLOCK
    r = lax.broadcasted_iota(jnp.int32, (rows, BAND), 0)
    qpos = start + (r & (QBLOCK - 1))
    kpos = ws + lax.broadcasted_iota(jnp.int32, (rows, BAND), 1)
    in_window = jnp.abs(qpos - kpos) <= WINDOW
    seg = lax.broadcasted_iota(jnp.int32, (rows, 1), 0) // QBLOCK

    q = q_ref[...]
    kb = k_ref[pl.ds(ws, BAND), :]
    vb = v_ref[pl.ds(ws, BAND), :]
    kc = ck_ref[...].astype(BF16)
    vc = cv_ref[...].astype(BF16)
    outs = []
    for j in range(N_KV_HEADS):
        heads = range(GROUP * j, GROUP * (j + 1))
        q4 = jnp.concatenate([q[:, HEAD_DIM * hd:HEAD_DIM * (hd + 1)] for hd in heads], axis=0)
        sink = jnp.zeros((rows, 1), F32)
        for g, hd in enumerate(heads):
            sink = jnp.where(seg == g, sink_ref[layer, hd], sink)
        cols = slice(HEAD_DIM * j, HEAD_DIM * (j + 1))
        l_ctx = _dot_nt(q4, kc[:, cols])
        l_band = jnp.where(in_window, _dot_nt(q4, kb[:, cols]), MASK_VALUE)
        o4 = _softmax_pv([(l_ctx, vc[:, cols]), (l_band, vb[:, cols])], sink)
        outs.extend(o4[QBLOCK * g:QBLOCK * (g + 1), :] for g in range(GROUP))
    o_ref[...] = jnp.concatenate(outs, axis=-1).astype(BF16)


def _lat_attention(sink, q, k, v, cache_k, cache_v, layer):
    nqb = DEC_SEQ // QBLOCK
    ctx_blocks = N_CTX // DEC_SEQ
    return pl.pallas_call(
        functools.partial(_lat_attn_kernel, layer=layer),
        out_shape=jax.ShapeDtypeStruct((N_LAT, ATTN_W), BF16),
        grid=(DEC_BATCH, nqb),
        in_specs=[
            pl.BlockSpec(memory_space=pltpu.SMEM),
            pl.BlockSpec((QBLOCK, ATTN_W), lambda b, i: (N_CTX // QBLOCK + b * nqb + i, 0)),
            pl.BlockSpec((DEC_SEQ, KV_W), lambda b, i: (ctx_blocks + b, 0)),
            pl.BlockSpec((DEC_SEQ, KV_W), lambda b, i: (ctx_blocks + b, 0)),
            pl.BlockSpec((None, None, PAST_LEN, KV_W), lambda b, i: (b, layer, 0, 0)),
            pl.BlockSpec((None, None, PAST_LEN, KV_W), lambda b, i: (b, layer, 0, 0)),
        ],
        out_specs=pl.BlockSpec((QBLOCK, ATTN_W), lambda b, i: (b * nqb + i, 0)),
        compiler_params=_params("parallel", "parallel"),
        name="latent_attention",
    )(sink, q, k, v, cache_k, cache_v)


def _merge_kernel(h_ref, atc_ref, atl_ref, ybc_ref, ybl_ref, x0_ref, x_ref, mod_ref,
                  wga_ref, wgb_ref, wpa_ref, wpb_ref, wo_ref, g_ref, b_ref,
                  xo_ref, ho_ref, mg_ref):
    is_ctx = pl.program_id(0) < N_CTX // ROW_TILE
    g1 = mod_ref[:, 2 * D_MODEL:3 * D_MODEL]
    sh2 = mod_ref[:, 3 * D_MODEL:4 * D_MODEL]
    sc2 = mod_ref[:, 4 * D_MODEL:5 * D_MODEL]
    for s in range(ROW_TILE // SUB_ROWS):
        rs = slice(SUB_ROWS * s, SUB_ROWS * (s + 1))
        h = h_ref[rs, :]
        attn = jnp.where(is_ctx, atc_ref[rs, :], atl_ref[rs, :])
        conv = jnp.where(is_ctx, ybc_ref[rs, :], ybl_ref[rs, :])
        hy = (x0_ref[rs, :].astype(F32) * conv.astype(F32)).astype(BF16)
        for j in range(D_MODEL // MXU_COLS):
            cs = slice(MXU_COLS * j, MXU_COLS * (j + 1))
            ga = jax.nn.sigmoid(_dot(h, wga_ref[:, cs]))
            gb = jax.nn.sigmoid(_dot(h, wgb_ref[:, cs]))
            mg_ref[rs, cs] = (ga * _dot(attn, wpa_ref[:, cs])
                              + gb * _dot(hy, wpb_ref[:, cs])).astype(BF16)
        sub = _dot(mg_ref[rs, :], wo_ref[...])
        x = _layer_norm(DEEPNORM_ALPHA * x_ref[rs, :] + g1 * sub, g_ref[...], b_ref[...])
        xo_ref[rs, :] = x
        ho_ref[rs, :] = (x * (1.0 + sc2) + sh2).astype(BF16)


def _merge(h, atc, atl, ybc, ybl, x0, x, mods, wga, wgb, wpa, wpb, wo, ln_g, ln_b, layer):
    t = ROW_TILE
    nct = N_CTX // t
    row = lambda m: (m, 0)
    ctx_row = lambda m: (jnp.minimum(m, nct - 1), 0)
    lat_row = lambda m: (jnp.maximum(m - nct, 0), 0)
    return pl.pallas_call(
        _merge_kernel,
        out_shape=(jax.ShapeDtypeStruct((N_TOK, D_MODEL), F32),
                   jax.ShapeDtypeStruct((N_TOK, D_MODEL), BF16)),
        grid=(N_TOK // t,),
        in_specs=[
            pl.BlockSpec((t, D_MODEL), row),
            pl.BlockSpec((t, ATTN_W), ctx_row),
            pl.BlockSpec((t, ATTN_W), lat_row),
            pl.BlockSpec((t, HY_W), ctx_row),
            pl.BlockSpec((t, HY_W), lat_row),
            pl.BlockSpec((t, HY_W), row),
            pl.BlockSpec((t, D_MODEL), row),
            _mod_spec(layer, t),
            _resident_spec((D_MODEL, D_MODEL)),
            _resident_spec((D_MODEL, D_MODEL)),
            _resident_spec((ATTN_W, D_MODEL)),
            _resident_spec((HY_W, D_MODEL)),
            _resident_spec((D_MODEL, D_MODEL)),
            _layer_spec((1, D_MODEL), layer),
            _layer_spec((1, D_MODEL), layer),
        ],
        out_specs=(pl.BlockSpec((t, D_MODEL), row), pl.BlockSpec((t, D_MODEL), row)),
        scratch_shapes=[pltpu.VMEM((t, D_MODEL), BF16)],
        compiler_params=_params("parallel"),
        name="merge_ln1",
    )(h, atc, atl, ybc, ybl, x0, x, mods, wga, wgb, wpa, wpb, wo, ln_g, ln_b)


def _ffn_kernel(h_ref, hp_ref, hn_ref, x_ref, mod_ref, nmod_ref, wg_ref, wv_ref, wd_ref,
                cw_ref, cb_ref, g_ref, b_ref, xo_ref, ho_ref, hx_ref, a_ref):
    n_sub = ROW_TILE // SUB_ROWS
    ext = SUB_ROWS + 2 * HALO
    is_ctx, seq_start, seq_end = _tile_edges(pl.program_id(0), ROW_TILE)
    _fill_extended(hx_ref, hp_ref, h_ref, hn_ref)
    g2 = mod_ref[:, 5 * D_MODEL:6 * D_MODEL]
    sh = nmod_ref[:, 0:D_MODEL]
    sc = nmod_ref[:, D_MODEL:2 * D_MODEL]
    for j in range(n_sub):
        rs = slice(SUB_ROWS * j, SUB_ROWS * (j + 1))
        zero_top, zero_bot = _sub_edges(j, n_sub, is_ctx, seq_start, seq_end)
        hx = hx_ref[SUB_ROWS * j:SUB_ROWS * j + ext, :]
        hm = h_ref[rs, :]
        for c in range(D_FF // MXU_COLS):
            cs = slice(MXU_COLS * c, MXU_COLS * (c + 1))
            gate = _dwconv3(_dot(hx, wg_ref[:, cs]), zero_top, zero_bot, cw_ref, cb_ref, cs)
            gelu = 0.5 * gate * (1.0 + lax.erf(gate * math.sqrt(0.5)))
            a_ref[rs, cs] = (gelu * _dot(hm, wv_ref[:, cs])).astype(BF16)
        sub = _dot(a_ref[rs, :], wd_ref[...])
        x = _layer_norm(DEEPNORM_ALPHA * x_ref[rs, :] + g2 * sub, g_ref[...], b_ref[...])
        xo_ref[rs, :] = x
        ho_ref[rs, :] = (x * (1.0 + sc) + sh).astype(BF16)


def _ffn(h, x, mods, w_up, w_down, conv_w, conv_b, ln_g, ln_b, layer):
    t = ROW_TILE
    row = lambda m: (m, 0)
    next_layer = min(layer + 1, DEPTH - 1)
    return pl.pallas_call(
        _ffn_kernel,
        out_shape=(jax.ShapeDtypeStruct((N_TOK, D_MODEL), F32),
                   jax.ShapeDtypeStruct((N_TOK, D_MODEL), BF16)),
        grid=(N_TOK // t,),
        in_specs=[
            pl.BlockSpec((t, D_MODEL), row),
            *_halo_specs(t, D_MODEL),
            pl.BlockSpec((t, D_MODEL), row),
            _mod_spec(layer, t),
            _mod_spec(next_layer, t),
            _resident_spec((D_MODEL, D_FF), (0, 0)),
            _resident_spec((D_MODEL, D_FF), (0, 1)),
            _resident_spec((D_FF, D_MODEL)),
            _layer_spec((3, D_FF), layer),
            _layer_spec((1, D_FF), layer),
            _layer_spec((1, D_MODEL), layer),
            _layer_spec((1, D_MODEL), layer),
        ],
        out_specs=(pl.BlockSpec((t, D_MODEL), row), pl.BlockSpec((t, D_MODEL), row)),
        scratch_shapes=[pltpu.VMEM((t + 2 * HALO, D_MODEL), BF16), pltpu.VMEM((t, D_FF), BF16)],
        compiler_params=_params("parallel"),
        name="conv_ffn_ln2",
    )(h, h, h, x, mods, mods, w_up, w_up, w_down, conv_w, conv_b, ln_g, ln_b)


def kernel(x_prompt, x_sample, cache_k, cache_v, c, c_ctx, w_ada, b_ada, w_in, attn_sink,
           hy_conv_w, hy_conv_b, filt_w1, filt_b1, filt_freq1, filt_w2, filt_b2, filt_freq2,
           filt_w3, hy_bias, w_pa, w_pb, w_out, ln1_g, ln1_b, w_up, ffn_conv_w, ffn_conv_b,
           w_down, ln2_g, ln2_b):
    cond = jnp.concatenate([c_ctx[None, :], c,
                            jnp.zeros((N_MOD_ROWS - 1 - DEC_BATCH, D_MODEL), F32)], axis=0)
    mods = _modulation(cond, w_ada, b_ada).reshape(DEPTH, N_MOD_ROWS, 1, 6 * D_MODEL)

    x, h = _prologue(x_prompt.reshape(N_CTX, D_MODEL), x_sample.reshape(N_LAT, D_MODEL), mods)

    cos_tab, sin_tab = _rope_tables()
    conv_args = {}
    for seq in (SEQ, DEC_SEQ):
        fwd, inv = _dft_tables(seq)
        hs = _filters(seq, filt_w1, filt_b1, filt_freq1, filt_w2, filt_b2, filt_freq2, filt_w3)
        conv_args[seq] = (fwd, inv, _filter_spectrum(seq, fwd, hs))

    cache_k = cache_k.reshape(DEC_BATCH, DEPTH, PAST_LEN, KV_W)
    cache_v = cache_v.reshape(DEC_BATCH, DEPTH, PAST_LEN, KV_W)
    hy_conv_b3 = hy_conv_b.reshape(DEPTH, 1, 3 * HY_W)
    ffn_conv_b3 = ffn_conv_b.reshape(DEPTH, 1, D_FF)
    ln1_g3, ln1_b3 = ln1_g.reshape(DEPTH, 1, D_MODEL), ln1_b.reshape(DEPTH, 1, D_MODEL)
    ln2_g3, ln2_b3 = ln2_g.reshape(DEPTH, 1, D_MODEL), ln2_b.reshape(DEPTH, 1, D_MODEL)

    hy0 = QKV_W
    ga0 = hy0 + 3 * HY_W
    gb0 = ga0 + D_MODEL
    keys, values = [], []
    for l in range(DEPTH):
        w_in_l = w_in[l]
        w_qkv = w_in_l[:, 0:hy0].astype(BF16)
        w_hy = w_in_l[:, hy0:ga0].astype(BF16)
        w_ga = w_in_l[:, ga0:gb0].astype(BF16)
        w_gb = w_in_l[:, gb0:].astype(BF16)

        q, k, v, kf, vf, x0, vh = _in_proj(h, w_qkv, w_hy, cos_tab, sin_tab,
                                           hy_conv_w, hy_conv_b3, l)
        keys.append(kf[:N_CTX])
        values.append(vf[:N_CTX])

        ybc = _long_conv(SEQ, BATCH, 0, vh, *conv_args[SEQ], hy_bias, l)
        ybl = _long_conv(DEC_SEQ, DEC_BATCH, N_CTX // DEC_SEQ, vh, *conv_args[DEC_SEQ], hy_bias, l)
        atc = _ctx_attention(attn_sink, q, k, v, l)
        atl = _lat_attention(attn_sink, q, k, v, cache_k, cache_v, l)

        x, h2 = _merge(h, atc, atl, ybc, ybl, x0, x, mods, w_ga, w_gb,
                       w_pa[l].astype(BF16), w_pb[l].astype(BF16), w_out[l].astype(BF16),
                       ln1_g3, ln1_b3, l)
        x, h = _ffn(h2, x, mods, w_up[l].astype(BF16), w_down[l].astype(BF16),
                    ffn_conv_w, ffn_conv_b3, ln2_g3, ln2_b3, l)

    y_prompt = x[:N_CTX].reshape(BATCH, SEQ, D_MODEL)
    y_sample = x[N_CTX:].reshape(DEC_BATCH, DEC_SEQ, D_MODEL)
    to_cache = lambda ts: jnp.stack(ts, axis=1).reshape(BATCH, DEPTH, SEQ, N_KV_HEADS, HEAD_DIM)
    new_k = to_cache([t.reshape(BATCH, SEQ, KV_W) for t in keys])
    new_v = to_cache([t.reshape(BATCH, SEQ, KV_W) for t in values])
    return (y_prompt, y_sample, new_k, new_v)
```

```python
import functools
import math

import jax
import jax.numpy as jnp
from jax import lax
from jax.experimental import pallas as pl
from jax.experimental.pallas import tpu as pltpu

D_MODEL = 1024
BATCH = 16
SEQ = 256
DEPTH = 4
DEC_BATCH = 4
DEC_SEQ = 2048
PAST_LEN = 512
GRID_W = 64
HEAD_DIM = 64
N_HEADS = 8
N_KV_HEADS = 2
GROUP = N_HEADS // N_KV_HEADS
ATTN_W = N_HEADS * HEAD_DIM
KV_W = N_KV_HEADS * HEAD_DIM
WINDOW = 128
QBLOCK = 128
HY_W = 512
FILT_EMB = 33
FILT_EMB_PAD = 40
FILT_BANDS = (FILT_EMB - 1) // 2
FILT_HIDDEN = 64
D_FF = 2816
ROPE_BASE = 10000.0
LN_EPS = 1e-5
DEEPNORM_ALPHA = (2 * DEPTH) ** 0.25

N_CTX = BATCH * SEQ
N_LAT = DEC_BATCH * DEC_SEQ
N_TOK = N_CTX + N_LAT
N_MOD_ROWS = 8
QKV_W = ATTN_W + 2 * KV_W
HY_BLOCK = QKV_W
assert 3 * HY_W == 2 * HY_BLOCK
GATE_COL0 = QKV_W + 3 * HY_W
BAND = QBLOCK + 2 * WINDOW
MASK_VALUE = -1e30
LOG2_E = math.log2(math.e)

F32 = jnp.float32
BF16 = jnp.bfloat16

VMEM_LIMIT_BYTES = 56 * 1024 * 1024
MXU_COLS = 256
ROW_TILE = 1024
SUB_ROWS = SEQ
HALO = 16
DFT_CHUNK = {SEQ: 256, DEC_SEQ: 512}
RESIDENT = dict(pipeline_mode=pl.Buffered(1))


def _params(*semantics):
    return pltpu.CompilerParams(dimension_semantics=semantics, vmem_limit_bytes=VMEM_LIMIT_BYTES)


def _dot(a, b):
    return jnp.dot(a, b, preferred_element_type=F32)


def _dot_nt(a, b):
    return lax.dot_general(a, b, (((1,), (1,)), ((), ())), preferred_element_type=F32)


def _mod_row(tile_rows):
    n_ctx_tiles = N_CTX // tile_rows
    tiles_per_seq = DEC_SEQ // tile_rows

    def fn(m):
        return jnp.where(m < n_ctx_tiles, 0, 1 + (m - n_ctx_tiles) // tiles_per_seq)

    return fn


def _mod_spec(layer, tile_rows):
    grp = _mod_row(tile_rows)
    return pl.BlockSpec((None, None, 1, 6 * D_MODEL), lambda m, *_: (layer, grp(m), 0, 0))


def _layer_spec(shape, layer):
    zeros = (0,) * len(shape)
    return pl.BlockSpec((None,) + tuple(shape), lambda *_: (layer,) + zeros)


def _weight_spec(shape, layer, block_index=None):
    block_index = block_index or (0,) * len(shape)
    return pl.BlockSpec((None,) + tuple(shape), lambda *_: (layer,) + tuple(block_index), **RESIDENT)


def _layer_norm(y, g, b):
    mu = jnp.mean(y, axis=-1, keepdims=True)
    yc = y - mu
    var = jnp.mean(yc * yc, axis=-1, keepdims=True)
    return yc * lax.rsqrt(var + LN_EPS) * g + b


def _tile_edges(m, tile_rows):
    n_ctx_tiles = N_CTX // tile_rows
    tiles_per_seq = DEC_SEQ // tile_rows
    is_ctx = m < n_ctx_tiles
    lat_pos = (m - n_ctx_tiles) % tiles_per_seq
    return (is_ctx, jnp.logical_or(is_ctx, lat_pos == 0),
            jnp.logical_or(is_ctx, lat_pos == tiles_per_seq - 1))


def _halo_specs(tile_rows, n_cols):
    tb = tile_rows // HALO
    last = N_TOK // HALO - 1
    return (pl.BlockSpec((HALO, n_cols), lambda m: (jnp.maximum(m * tb - 1, 0), 0)),
            pl.BlockSpec((HALO, n_cols), lambda m: (jnp.minimum((m + 1) * tb, last), 0)))


def _fill_extended(hx_ref, hp_ref, h_ref, hn_ref):
    rows = h_ref.shape[0]
    hx_ref[0:HALO, :] = hp_ref[...]
    hx_ref[HALO:HALO + rows, :] = h_ref[...]
    hx_ref[HALO + rows:2 * HALO + rows, :] = hn_ref[...]


def _sub_edges(j, n_sub, is_ctx, seq_start, seq_end):
    return (seq_start if j == 0 else is_ctx), (seq_end if j == n_sub - 1 else is_ctx)


def _dwconv3(z, zero_top, zero_bot, w_ref, b_ref, cs):
    n = z.shape[0] - 2 * HALO
    top = jnp.where(zero_top, 0.0, z[HALO - 8:HALO])
    bot = jnp.where(zero_bot, 0.0, z[HALO + n:HALO + n + 8])
    z = jnp.concatenate([z[0:HALO - 8], top, z[HALO:HALO + n], bot, z[HALO + n + 8:]], axis=0)
    prev = pltpu.roll(z, 1, 0)[HALO:HALO + n]
    nxt = pltpu.roll(z, z.shape[0] - 1, 0)[HALO:HALO + n]
    return (prev * w_ref[0:1, cs] + z[HALO:HALO + n] * w_ref[1:2, cs] + nxt * w_ref[2:3, cs]
            + b_ref[:, cs])


def _mod_kernel(cond_ref, w_ref, b_ref, o_ref):
    c = cond_ref[...]
    s = (c * jax.nn.sigmoid(c)).astype(BF16)
    o_ref[...] = _dot(s, w_ref[...].astype(BF16)) + b_ref[...]


def _modulation(cond, w_ada, b_ada):
    n_col = 6 * D_MODEL // D_MODEL
    return pl.pallas_call(
        _mod_kernel,
        out_shape=jax.ShapeDtypeStruct((DEPTH, N_MOD_ROWS, 6 * D_MODEL), F32),
        grid=(DEPTH, n_col),
        in_specs=[
            pl.BlockSpec((N_MOD_ROWS, D_MODEL), lambda l, j: (0, 0)),
            pl.BlockSpec((None, D_MODEL, D_MODEL), lambda l, j: (l, 0, j)),
            pl.BlockSpec((None, 1, D_MODEL), lambda l, j: (l, 0, j)),
        ],
        out_specs=pl.BlockSpec((None, N_MOD_ROWS, D_MODEL), lambda l, j: (l, 0, j)),
        compiler_params=_params("parallel", "parallel"),
        name="modulation",
    )(cond, w_ada, b_ada.reshape(DEPTH, 1, 6 * D_MODEL))


PRO_TILE = 1024


def _prologue_kernel(xp_ref, xs_ref, mod_ref, x_ref, h_ref):
    m = pl.program_id(0)

    def emit(x):
        x_ref[...] = x
        sh = mod_ref[:, 0:D_MODEL]
        sc = mod_ref[:, D_MODEL:2 * D_MODEL]
        h_ref[...] = (x * (1.0 + sc) + sh).astype(BF16)

    @pl.when(m < N_CTX // PRO_TILE)
    def _():
        emit(xp_ref[...])

    @pl.when(m >= N_CTX // PRO_TILE)
    def _():
        emit(xs_ref[...])


def _prologue(xp, xs, mods):
    nct = N_CTX // PRO_TILE
    return pl.pallas_call(
        _prologue_kernel,
        out_shape=(jax.ShapeDtypeStruct((N_TOK, D_MODEL), F32),
                   jax.ShapeDtypeStruct((N_TOK, D_MODEL), BF16)),
        grid=(N_TOK // PRO_TILE,),
        in_specs=[
            pl.BlockSpec((PRO_TILE, D_MODEL), lambda m: (jnp.minimum(m, nct - 1), 0)),
            pl.BlockSpec((PRO_TILE, D_MODEL), lambda m: (jnp.maximum(m - nct, 0), 0)),
            _mod_spec(0, PRO_TILE),
        ],
        out_specs=(pl.BlockSpec((PRO_TILE, D_MODEL), lambda m: (m, 0)),
                   pl.BlockSpec((PRO_TILE, D_MODEL), lambda m: (m, 0))),
        compiler_params=_params("parallel"),
        name="prologue",
    )(xp, xs, mods)


ROPE_LANES = 2 * HEAD_DIM


def _in_proj_kernel(h_ref, hp_ref, hn_ref, wqkv_ref, why0_ref, why1_ref, cos_ref, sin_ref,
                    cw_ref, cb_ref, q_ref, k_ref, vt_ref, kf_ref, vf_ref, x0_ref, vh_ref, hx_ref):
    n_sub = ROW_TILE // SUB_ROWS
    ext = SUB_ROWS + 2 * HALO
    is_ctx, seq_start, seq_end = _tile_edges(pl.program_id(0), ROW_TILE)
    _fill_extended(hx_ref, hp_ref, h_ref, hn_ref)
    lane = lax.broadcasted_iota(jnp.int32, (SUB_ROWS, ROPE_LANES), 1)
    first_half = (lane & (HEAD_DIM // 4)) == 0
    scale = HEAD_DIM ** -0.5 * LOG2_E
    for j in range(n_sub):
        rs = slice(SUB_ROWS * j, SUB_ROWS * (j + 1))
        cos = cos_ref[rs, :]
        sin = sin_ref[rs, :]

        def rope(x):
            partner = jnp.where(first_half, pltpu.roll(x, ROPE_LANES - HEAD_DIM // 4, 1),
                                pltpu.roll(x, HEAD_DIM // 4, 1))
            return x * cos + partner * sin

        hm = h_ref[rs, :]
        for jj in range(ATTN_W // MXU_COLS):
            qq = _dot(hm, wqkv_ref[:, MXU_COLS * jj:MXU_COLS * (jj + 1)])
            for i in range(MXU_COLS // ROPE_LANES):
                c0 = MXU_COLS * jj + ROPE_LANES * i
                q_ref[rs, c0:c0 + ROPE_LANES] = (
                    rope(qq[:, ROPE_LANES * i:ROPE_LANES * (i + 1)]) * scale).astype(BF16)
        kv = _dot(hm, wqkv_ref[:, ATTN_W:QKV_W])
        k = kv[:, 0:KV_W]
        v = kv[:, KV_W:2 * KV_W]
        kf_ref[rs, :] = k
        vf_ref[rs, :] = v
        k_ref[rs, :] = rope(k).astype(BF16)
        vt_ref[:, rs] = v.T.astype(BF16)

        zero_top, zero_bot = _sub_edges(j, n_sub, is_ctx, seq_start, seq_end)
        hx = hx_ref[SUB_ROWS * j:SUB_ROWS * j + ext, :]

        def proj_conv(c0):
            w_ref = why0_ref if c0 < HY_BLOCK else why1_ref
            w = w_ref[:, c0 % HY_BLOCK:c0 % HY_BLOCK + MXU_COLS]
            return _dwconv3(_dot(hx, w), zero_top, zero_bot, cw_ref, cb_ref,
                            slice(c0, c0 + MXU_COLS))

        for jj in range(HY_W // MXU_COLS):
            c0 = MXU_COLS * jj
            x0_ref[rs, c0:c0 + MXU_COLS] = proj_conv(c0).astype(BF16)
            x1 = proj_conv(HY_W + c0)
            u = proj_conv(2 * HY_W + c0)
            vh_ref[rs, c0:c0 + MXU_COLS] = (x1 * u).astype(BF16)


def _in_proj(h, w_in, cos_tab, sin_tab, conv_w, conv_b, layer):
    t = ROW_TILE
    nct = N_CTX // t
    tps = DEC_SEQ // t
    tab = lambda m: (jnp.where(m < nct, 0, 1 + (m - nct) % tps), 0, 0)
    row = lambda m: (m, 0)
    bf = lambda w: jax.ShapeDtypeStruct((N_TOK, w), BF16)
    return pl.pallas_call(
        _in_proj_kernel,
        out_shape=(bf(ATTN_W), bf(KV_W),
                   jax.ShapeDtypeStruct((KV_W, N_TOK), BF16),
                   jax.ShapeDtypeStruct((N_TOK, KV_W), F32),
                   jax.ShapeDtypeStruct((N_TOK, KV_W), F32),
                   bf(HY_W), bf(HY_W)),
        grid=(N_TOK // t,),
        in_specs=[
            pl.BlockSpec((t, D_MODEL), row),
            *_halo_specs(t, D_MODEL),
            _weight_spec((D_MODEL, QKV_W), layer, (0, 0)),
            _weight_spec((D_MODEL, HY_BLOCK), layer, (0, 1)),
            _weight_spec((D_MODEL, HY_BLOCK), layer, (0, 2)),
            pl.BlockSpec((None, t, ROPE_LANES), tab),
            pl.BlockSpec((None, t, ROPE_LANES), tab),
            _layer_spec((3, 3 * HY_W), layer),
            _layer_spec((1, 3 * HY_W), layer),
        ],
        out_specs=(pl.BlockSpec((t, ATTN_W), row),
                   pl.BlockSpec((t, KV_W), row),
                   pl.BlockSpec((KV_W, t), lambda m: (0, m)),
                   pl.BlockSpec((t, KV_W), row),
                   pl.BlockSpec((t, KV_W), row),
                   pl.BlockSpec((t, HY_W), row),
                   pl.BlockSpec((t, HY_W), row)),
        scratch_shapes=[pltpu.VMEM((t + 2 * HALO, D_MODEL), BF16)],
        compiler_params=_params("parallel"),
        name="in_proj",
    )(h, h, h, w_in, w_in, w_in, cos_tab, sin_tab, conv_w, conv_b)


def _rope_tables():
    rows = DEC_SEQ // GRID_W
    r, col = jnp.meshgrid(jnp.arange(rows), jnp.arange(GRID_W), indexing='ij')
    pos = jnp.stack([r.reshape(-1), col.reshape(-1)], axis=-1).astype(F32)
    half = HEAD_DIM // 2
    inv_freq = 1.0 / (ROPE_BASE ** (jnp.arange(0, half, 2, dtype=F32) / half))
    ang = pos[:, :, None] * inv_freq
    ang = jnp.stack([ang, ang], axis=-2).reshape(DEC_SEQ, HEAD_DIM)
    cos = jnp.tile(jnp.cos(ang), (1, ROPE_LANES // HEAD_DIM))
    sin = jnp.tile(jnp.sin(ang), (1, ROPE_LANES // HEAD_DIM))
    lane = jnp.arange(ROPE_LANES)
    sin = jnp.where((lane & (HEAD_DIM // 4)) == 0, -sin, sin)
    n_slab = DEC_SEQ // ROW_TILE
    cos = jnp.concatenate([jnp.ones((1, ROW_TILE, ROPE_LANES), F32),
                           cos.reshape(n_slab, ROW_TILE, ROPE_LANES)], axis=0)
    sin = jnp.concatenate([jnp.zeros((1, ROW_TILE, ROPE_LANES), F32),
                           sin.reshape(n_slab, ROW_TILE, ROPE_LANES)], axis=0)
    return cos, sin


def _filter_kernel(feat_ref, t_ref, dec_ref, w1_ref, b1_ref, f1_ref, w2_ref, b2_ref, f2_ref,
                   w3_ref, o_ref):
    hp = lax.Precision.HIGHEST
    a = jnp.dot(feat_ref[...], w1_ref[...], precision=hp, preferred_element_type=F32) + b1_ref[...]
    a = jnp.sin(f1_ref[...] * a)
    a = jnp.dot(a, w2_ref[...], precision=hp, preferred_element_type=F32) + b2_ref[...]
    a = jnp.sin(f2_ref[...] * a)
    hh = jnp.dot(a, w3_ref[...], precision=hp, preferred_element_type=F32)
    window = jnp.exp(-t_ref[...] * dec_ref[...])
    h_fwd = hh[:, 0:HY_W] * window
    h_bwd = hh[:, HY_W:2 * HY_W] * window
    row = lax.broadcasted_iota(jnp.int32, h_bwd.shape, 0)
    h_bwd = jnp.where(row == 0, 0.0, h_bwd)
    o_ref[:, 0:HY_W] = (h_fwd + h_bwd).astype(BF16)
    o_ref[:, HY_W:2 * HY_W] = (h_fwd - h_bwd).astype(BF16)


def _filters(seq, w1, b1, f1, w2, b2, f2, w3):
    t = jnp.linspace(0.0, 1.0, seq, dtype=F32)[:, None]
    w = 2.0 * math.pi * jnp.arange(seq, dtype=F32) / seq
    f = jnp.linspace(1e-4, FILT_BANDS - 1, FILT_BANDS, dtype=F32)
    zr = w[:, None] * f[None, :]
    feats = jnp.concatenate([t, jnp.cos(zr), -jnp.sin(zr),
                             jnp.zeros((seq, FILT_EMB_PAD - FILT_EMB), F32)], axis=-1)
    target = 1e-2
    decay = jnp.abs(jnp.linspace(math.log(target) / 1.5, math.log(target) / 0.3, HY_W,
                                 dtype=F32))[None, :]
    w1p = jnp.pad(w1, ((0, 0), (0, FILT_EMB_PAD - FILT_EMB), (0, 0)))
    vec = lambda a: a.reshape(DEPTH, 1, FILT_HIDDEN)
    return pl.pallas_call(
        _filter_kernel,
        out_shape=jax.ShapeDtypeStruct((DEPTH, seq, 2 * HY_W), BF16),
        grid=(DEPTH,),
        in_specs=[
            pl.BlockSpec((seq, FILT_EMB_PAD), lambda l: (0, 0)),
            pl.BlockSpec((seq, 1), lambda l: (0, 0)),
            pl.BlockSpec((1, HY_W), lambda l: (0, 0)),
            pl.BlockSpec((None, FILT_EMB_PAD, FILT_HIDDEN), lambda l: (l, 0, 0)),
            pl.BlockSpec((None, 1, FILT_HIDDEN), lambda l: (l, 0, 0)),
            pl.BlockSpec((None, 1, FILT_HIDDEN), lambda l: (l, 0, 0)),
            pl.BlockSpec((None, FILT_HIDDEN, FILT_HIDDEN), lambda l: (l, 0, 0)),
            pl.BlockSpec((None, 1, FILT_HIDDEN), lambda l: (l, 0, 0)),
            pl.BlockSpec((None, 1, FILT_HIDDEN), lambda l: (l, 0, 0)),
            pl.BlockSpec((None, FILT_HIDDEN, 2 * HY_W), lambda l: (l, 0, 0)),
        ],
        out_specs=pl.BlockSpec((None, seq, 2 * HY_W), lambda l: (l, 0, 0)),
        compiler_params=_params("parallel"),
        name=f"hyena_filter_{seq}",
    )(feats, t, decay, w1p, vec(b1), vec(f1), w2, vec(b2), vec(f2), w3)


def _dft_tables(seq):
    fc = DFT_CHUNK[seq]
    period = 4 * seq
    f = jnp.arange(seq, dtype=jnp.int32)[:, None]
    t = jnp.arange(seq, dtype=jnp.int32)[None, :]
    k = ((2 * f + 1) * t) & (period - 1)
    ang = k.astype(F32) * (2.0 * math.pi / period)
    cos = jnp.cos(ang).reshape(seq // fc, fc, seq)
    msin = (-jnp.sin(ang)).reshape(seq // fc, fc, seq)
    fwd = jnp.concatenate([cos, msin], axis=1)
    inv = jnp.swapaxes(fwd, 1, 2) * (1.0 / seq)
    return fwd.astype(BF16), inv.astype(BF16)


def _spectrum_kernel(a_ref, hs_ref, g_ref, *, fc):
    g_ref[0:fc, :] = _dot(a_ref[0:fc, :], hs_ref[:, 0:HY_W])
    g_ref[fc:2 * fc, :] = _dot(a_ref[fc:2 * fc, :], hs_ref[:, HY_W:2 * HY_W])


def _filter_spectrum(seq, fwd, hs):
    fc = DFT_CHUNK[seq]
    nch = seq // fc
    return pl.pallas_call(
        functools.partial(_spectrum_kernel, fc=fc),
        out_shape=jax.ShapeDtypeStruct((DEPTH, nch, 2 * fc, HY_W), F32),
        grid=(DEPTH, nch),
        in_specs=[
            pl.BlockSpec((None, 2 * fc, seq), lambda l, c: (c, 0, 0)),
            pl.BlockSpec((None, seq, 2 * HY_W), lambda l, c: (l, 0, 0)),
        ],
        out_specs=pl.BlockSpec((None, None, 2 * fc, HY_W), lambda l, c: (l, c, 0, 0)),
        compiler_params=_params("parallel", "parallel"),
        name=f"hyena_spectrum_{seq}",
    )(fwd, hs)


def _longconv_kernel(v_ref, a_ref, bt_ref, g_ref, bias_ref, o_ref, y_ref, acc_ref, *, fc):
    c = pl.program_id(1)
    v = v_ref[...]
    spec = _dot(a_ref[...], v)
    ur = spec[0:fc, :]
    ui = spec[fc:2 * fc, :]
    gr = g_ref[0:fc, :]
    gi = g_ref[fc:2 * fc, :]
    y_ref[0:fc, :] = (ur * gr - ui * gi).astype(BF16)
    y_ref[fc:2 * fc, :] = (ur * gi + ui * gr).astype(BF16)
    contrib = _dot(bt_ref[...], y_ref[...])

    @pl.when(c == 0)
    def _():
        acc_ref[...] = contrib

    @pl.when(c > 0)
    def _():
        acc_ref[...] += contrib

    @pl.when(c == pl.num_programs(1) - 1)
    def _():
        o_ref[...] = (acc_ref[...] + bias_ref[...] * v.astype(F32)).astype(BF16)


def _long_conv(seq, n_seq, row_block0, vh, fwd, inv, spectrum, hy_bias, layer):
    fc = DFT_CHUNK[seq]
    nch = seq // fc
    return pl.pallas_call(
        functools.partial(_longconv_kernel, fc=fc),
        out_shape=jax.ShapeDtypeStruct((n_seq * seq, HY_W), BF16),
        grid=(n_seq, nch),
        in_specs=[
            pl.BlockSpec((seq, HY_W), lambda b, c: (row_block0 + b, 0)),
            pl.BlockSpec((None, 2 * fc, seq), lambda b, c: (c, 0, 0)),
            pl.BlockSpec((None, seq, 2 * fc), lambda b, c: (c, 0, 0)),
            pl.BlockSpec((None, None, 2 * fc, HY_W), lambda b, c: (layer, c, 0, 0)),
            pl.BlockSpec((None, 1, HY_W), lambda b, c: (layer, 0, 0)),
        ],
        out_specs=pl.BlockSpec((seq, HY_W), lambda b, c: (b, 0)),
        scratch_shapes=[pltpu.VMEM((2 * fc, HY_W), BF16), pltpu.VMEM((seq, HY_W), F32)],
        compiler_params=_params("parallel", "arbitrary"),
        name=f"hyena_longconv_{seq}",
    )(vh, fwd, inv, spectrum, hy_bias.reshape(DEPTH, 1, HY_W))


ONES_ROWS = 16
CTX_SEQS_PER_STEP = 2
QBLOCKS_PER_STEP = 2


def _head_rows(q, j):
    return jnp.concatenate([q[:, HEAD_DIM * hd:HEAD_DIM * (hd + 1)]
                            for hd in range(GROUP * j, GROUP * (j + 1))], axis=0)


def _lane_sink(sink_ref, layer, j, block):
    head = lax.broadcasted_iota(jnp.int32, (1, GROUP * block), 1) // block
    sink = jnp.zeros((1, GROUP * block), F32)
    for g in range(GROUP):
        sink = jnp.where(head == g, sink_ref[layer, GROUP * j + g] * LOG2_E, sink)
    return sink


def _with_ones(vt):
    return jnp.concatenate([vt, jnp.ones((ONES_ROWS, vt.shape[1]), BF16)], axis=0)


def _softmax_pv_t(parts, sink):
    m = sink
    for lg, _ in parts:
        m = jnp.maximum(m, jnp.max(lg, axis=0, keepdims=True))
    out = None
    for lg, vt in parts:
        pv = _dot(vt, jnp.exp2(lg - m).astype(BF16))
        out = pv if out is None else out + pv
    den = out[HEAD_DIM:HEAD_DIM + 1, :] + jnp.exp2(sink - m)
    return out[0:HEAD_DIM, :] / den


def _run_chains(chains, logits_fn, finish_fn):
    pending = logits_fn(*chains[0])
    for c in range(1, len(chains)):
        nxt = logits_fn(*chains[c])
        finish_fn(*chains[c - 1], *pending)
        pending = nxt
    finish_fn(*chains[-1], *pending)


def _store_heads(o_ref, rows, j, block, o_t):
    for g in range(GROUP):
        hd = GROUP * j + g
        o_ref[rows, HEAD_DIM * hd:HEAD_DIM * (hd + 1)] = (
            o_t[:, block * g:block * (g + 1)].T.astype(BF16))


def _ctx_attn_kernel(sink_ref, q_ref, k_ref, vt_ref, o_ref, *, layer):
    def logits(n, j):
        rows = slice(SEQ * n, SEQ * (n + 1))
        hs = slice(HEAD_DIM * j, HEAD_DIM * (j + 1))
        return _dot_nt(k_ref[rows, hs], _head_rows(q_ref[rows, :], j)), _with_ones(vt_ref[hs, rows])

    def finish(n, j, s, vt):
        o_t = _softmax_pv_t([(s, vt)], _lane_sink(sink_ref, layer, j, SEQ))
        _store_heads(o_ref, slice(SEQ * n, SEQ * (n + 1)), j, SEQ, o_t)

    _run_chains([(n, j) for n in range(CTX_SEQS_PER_STEP) for j in range(N_KV_HEADS)],
                logits, finish)


def _ctx_attention(sink, q, k, vt, layer):
    t = CTX_SEQS_PER_STEP * SEQ
    row = lambda b: (b, 0)
    return pl.pallas_call(
        functools.partial(_ctx_attn_kernel, layer=layer),
        out_shape=jax.ShapeDtypeStruct((N_CTX, ATTN_W), BF16),
        grid=(N_CTX // t,),
        in_specs=[
            pl.BlockSpec(memory_space=pltpu.SMEM),
            pl.BlockSpec((t, ATTN_W), row),
            pl.BlockSpec((t, KV_W), row),
            pl.BlockSpec((KV_W, t), lambda b: (0, b)),
        ],
        out_specs=pl.BlockSpec((t, ATTN_W), row),
        compiler_params=_params("parallel"),
        name="context_attention",
    )(sink, q, k, vt)


def _band_bias():
    kpos = jnp.arange(BAND)[None, :, None] - jnp.arange(3)[:, None, None] * WINDOW
    qpos = (jnp.arange(GROUP * QBLOCK) & (QBLOCK - 1))[None, None, :]
    return jnp.where(jnp.abs(qpos - kpos) <= WINDOW, 0.0, MASK_VALUE).astype(F32)


def _lat_attn_kernel(sink_ref, q_ref, k_ref, vt_ref, ck_ref, cv_ref, bias_ref, o_ref,
                     kc_ref, vct_ref, *, layer):
    i = pl.program_id(1)

    @pl.when(i == 0)
    def _():
        kc_ref[...] = ck_ref[...].astype(BF16)
        vct = cv_ref[...].T.astype(BF16)
        for j in range(N_KV_HEADS):
            vct_ref[j] = _with_ones(vct[HEAD_DIM * j:HEAD_DIM * (j + 1), :])

    def logits(n, j):
        start = (i * QBLOCKS_PER_STEP + n) * QBLOCK
        ws = pl.multiple_of(jnp.clip(start - WINDOW, 0, DEC_SEQ - BAND), QBLOCK)
        placement = jnp.where(start < WINDOW, 0, jnp.where(start + QBLOCK + WINDOW > DEC_SEQ, 2, 1))
        q4 = _head_rows(q_ref[QBLOCK * n:QBLOCK * (n + 1), :], j)
        hs = slice(HEAD_DIM * j, HEAD_DIM * (j + 1))
        s_ctx = _dot_nt(kc_ref[:, hs], q4)
        s_band = _dot_nt(k_ref[pl.ds(ws, BAND), hs], q4) + bias_ref[placement]
        return s_ctx, s_band, _with_ones(vt_ref[hs, pl.ds(ws, BAND)])

    def finish(n, j, s_ctx, s_band, vbt):
        o_t = _softmax_pv_t([(s_ctx, vct_ref[j]), (s_band, vbt)],
                            _lane_sink(sink_ref, layer, j, QBLOCK))
        _store_heads(o_ref, slice(QBLOCK * n, QBLOCK * (n + 1)), j, QBLOCK, o_t)

    _run_chains([(n, j) for n in range(QBLOCKS_PER_STEP) for j in range(N_KV_HEADS)],
                logits, finish)


def _lat_attention(sink, q, k, vt, cache_k, cache_v, band_bias, layer):
    t = QBLOCKS_PER_STEP * QBLOCK
    steps = DEC_SEQ // t
    seq_block = N_CTX // DEC_SEQ
    return pl.pallas_call(
        functools.partial(_lat_attn_kernel, layer=layer),
        out_shape=jax.ShapeDtypeStruct((N_LAT, ATTN_W), BF16),
        grid=(DEC_BATCH, steps),
        in_specs=[
            pl.BlockSpec(memory_space=pltpu.SMEM),
            pl.BlockSpec((t, ATTN_W), lambda b, i: (N_CTX // t + b * steps + i, 0)),
            pl.BlockSpec((DEC_SEQ, KV_W), lambda b, i: (seq_block + b, 0)),
            pl.BlockSpec((KV_W, DEC_SEQ), lambda b, i: (0, seq_block + b)),
            pl.BlockSpec((None, None, PAST_LEN, KV_W), lambda b, i: (b, layer, 0, 0)),
            pl.BlockSpec((None, None, PAST_LEN, KV_W), lambda b, i: (b, layer, 0, 0)),
            pl.BlockSpec((3, BAND, GROUP * QBLOCK), lambda b, i: (0, 0, 0)),
        ],
        out_specs=pl.BlockSpec((t, ATTN_W), lambda b, i: (b * steps + i, 0)),
        scratch_shapes=[pltpu.VMEM((PAST_LEN, KV_W), BF16),
                        pltpu.VMEM((N_KV_HEADS, HEAD_DIM + ONES_ROWS, PAST_LEN), BF16)],
        compiler_params=_params("parallel", "arbitrary"),
        name="latent_attention",
    )(sink, q, k, vt, cache_k, cache_v, band_bias)


N_GATE_CHUNKS = D_MODEL // MXU_COLS


def _merge_kernel(h_ref, atc_ref, atl_ref, ybc_ref, ybl_ref, x0_ref, x_ref, mod_ref, *rest):
    wga_refs = rest[0:N_GATE_CHUNKS]
    wgb_refs = rest[N_GATE_CHUNKS:2 * N_GATE_CHUNKS]
    wpa_ref, wpb_ref, wo_ref, g_ref, b_ref, xo_ref, ho_ref, mg_ref = rest[2 * N_GATE_CHUNKS:]
    is_ctx = pl.program_id(0) < N_CTX // ROW_TILE
    g1 = mod_ref[:, 2 * D_MODEL:3 * D_MODEL]
    sh2 = mod_ref[:, 3 * D_MODEL:4 * D_MODEL]
    sc2 = mod_ref[:, 4 * D_MODEL:5 * D_MODEL]
    for s in range(ROW_TILE // SUB_ROWS):
        rs = slice(SUB_ROWS * s, SUB_ROWS * (s + 1))
        h = h_ref[rs, :]
        attn = jnp.where(is_ctx, atc_ref[rs, :], atl_ref[rs, :])
        conv = jnp.where(is_ctx, ybc_ref[rs, :], ybl_ref[rs, :])
        hy = (x0_ref[rs, :].astype(F32) * conv.astype(F32)).astype(BF16)
        for j in range(N_GATE_CHUNKS):
            cs = slice(MXU_COLS * j, MXU_COLS * (j + 1))
            ga = jax.nn.sigmoid(_dot(h, wga_refs[j][...]))
            gb = jax.nn.sigmoid(_dot(h, wgb_refs[j][...]))
            mg_ref[rs, cs] = (ga * _dot(attn, wpa_ref[:, cs])
                              + gb * _dot(hy, wpb_ref[:, cs])).astype(BF16)
        sub = _dot(mg_ref[rs, :], wo_ref[...])
        x = _layer_norm(DEEPNORM_ALPHA * x_ref[rs, :] + g1 * sub, g_ref[...], b_ref[...])
        xo_ref[rs, :] = x
        ho_ref[rs, :] = (x * (1.0 + sc2) + sh2).astype(BF16)


def _merge(h, atc, atl, ybc, ybl, x0, x, mods, w_in, wpa, wpb, wo, ln_g, ln_b, layer):
    t = ROW_TILE
    nct = N_CTX // t
    row = lambda m: (m, 0)
    ctx_row = lambda m: (jnp.minimum(m, nct - 1), 0)
    lat_row = lambda m: (jnp.maximum(m - nct, 0), 0)
    gate_block0 = GATE_COL0 // MXU_COLS
    gate_specs = [_weight_spec((D_MODEL, MXU_COLS), layer, (0, gate_block0 + j))
                  for j in range(2 * N_GATE_CHUNKS)]
    return pl.pallas_call(
        _merge_kernel,
        out_shape=(jax.ShapeDtypeStruct((N_TOK, D_MODEL), F32),
                   jax.ShapeDtypeStruct((N_TOK, D_MODEL), BF16)),
        grid=(N_TOK // t,),
        in_specs=[
            pl.BlockSpec((t, D_MODEL), row),
            pl.BlockSpec((t, ATTN_W), ctx_row),
            pl.BlockSpec((t, ATTN_W), lat_row),
            pl.BlockSpec((t, HY_W), ctx_row),
            pl.BlockSpec((t, HY_W), lat_row),
            pl.BlockSpec((t, HY_W), row),
            pl.BlockSpec((t, D_MODEL), row),
            _mod_spec(layer, t),
            *gate_specs,
            _weight_spec((ATTN_W, D_MODEL), layer),
            _weight_spec((HY_W, D_MODEL), layer),
            _weight_spec((D_MODEL, D_MODEL), layer),
            _layer_spec((1, D_MODEL), layer),
            _layer_spec((1, D_MODEL), layer),
        ],
        out_specs=(pl.BlockSpec((t, D_MODEL), row), pl.BlockSpec((t, D_MODEL), row)),
        scratch_shapes=[pltpu.VMEM((t, D_MODEL), BF16)],
        compiler_params=_params("parallel"),
        name="merge_ln1",
    )(h, atc, atl, ybc, ybl, x0, x, mods, *([w_in] * (2 * N_GATE_CHUNKS)), wpa, wpb, wo,
      ln_g, ln_b)


def _ffn_kernel(h_ref, hp_ref, hn_ref, x_ref, mod_ref, nmod_ref, wg_ref, wv_ref, wd_ref,
                cw_ref, cb_ref, g_ref, b_ref, *rest, last_layer):
    if last_layer:
        yp_ref, ys_ref, hx_ref, a_ref, xo_ref = rest
    else:
        xo_ref, ho_ref, hx_ref, a_ref = rest
    n_sub = ROW_TILE // SUB_ROWS
    ext = SUB_ROWS + 2 * HALO
    is_ctx, seq_start, seq_end = _tile_edges(pl.program_id(0), ROW_TILE)
    _fill_extended(hx_ref, hp_ref, h_ref, hn_ref)
    g2 = mod_ref[:, 5 * D_MODEL:6 * D_MODEL]
    sh = nmod_ref[:, 0:D_MODEL]
    sc = nmod_ref[:, D_MODEL:2 * D_MODEL]
    for j in range(n_sub):
        rs = slice(SUB_ROWS * j, SUB_ROWS * (j + 1))
        zero_top, zero_bot = _sub_edges(j, n_sub, is_ctx, seq_start, seq_end)
        hx = hx_ref[SUB_ROWS * j:SUB_ROWS * j + ext, :]
        hm = h_ref[rs, :]
        for c in range(D_FF // MXU_COLS):
            cs = slice(MXU_COLS * c, MXU_COLS * (c + 1))
            gate = _dwconv3(_dot(hx, wg_ref[:, cs]), zero_top, zero_bot, cw_ref, cb_ref, cs)
            gelu = 0.5 * gate * (1.0 + lax.erf(gate * math.sqrt(0.5)))
            a_ref[rs, cs] = (gelu * _dot(hm, wv_ref[:, cs])).astype(BF16)
        sub = _dot(a_ref[rs, :], wd_ref[...])
        x = _layer_norm(DEEPNORM_ALPHA * x_ref[rs, :] + g2 * sub, g_ref[...], b_ref[...])
        xo_ref[rs, :] = x
        if not last_layer:
            ho_ref[rs, :] = (x * (1.0 + sc) + sh).astype(BF16)

    if last_layer:
        @pl.when(is_ctx)
        def _():
            yp_ref[...] = xo_ref[...]

        @pl.when(jnp.logical_not(is_ctx))
        def _():
            ys_ref[...] = xo_ref[...]


def _ffn(h, x, mods, w_up, w_down, conv_w, conv_b, ln_g, ln_b, layer):
    t = ROW_TILE
    nct = N_CTX // t
    row = lambda m: (m, 0)
    last_layer = layer == DEPTH - 1
    if last_layer:
        out_shape = (jax.ShapeDtypeStruct((N_CTX, D_MODEL), F32),
                     jax.ShapeDtypeStruct((N_LAT, D_MODEL), F32))
        out_specs = (pl.BlockSpec((t, D_MODEL), lambda m: (jnp.minimum(m, nct - 1), 0)),
                     pl.BlockSpec((t, D_MODEL), lambda m: (jnp.maximum(m - nct, 0), 0)))
        scratch = [pltpu.VMEM((t, D_MODEL), F32)]
    else:
        out_shape = (jax.ShapeDtypeStruct((N_TOK, D_MODEL), F32),
                     jax.ShapeDtypeStruct((N_TOK, D_MODEL), BF16))
        out_specs = (pl.BlockSpec((t, D_MODEL), row), pl.BlockSpec((t, D_MODEL), row))
        scratch = []
    return pl.pallas_call(
        functools.partial(_ffn_kernel, last_layer=last_layer),
        out_shape=out_shape,
        grid=(N_TOK // t,),
        in_specs=[
            pl.BlockSpec((t, D_MODEL), row),
            *_halo_specs(t, D_MODEL),
            pl.BlockSpec((t, D_MODEL), row),
            _mod_spec(layer, t),
            _mod_spec(min(layer + 1, DEPTH - 1), t),
            _weight_spec((D_MODEL, D_FF), layer, (0, 0)),
            _weight_spec((D_MODEL, D_FF), layer, (0, 1)),
            _weight_spec((D_FF, D_MODEL), layer),
            _layer_spec((3, D_FF), layer),
            _layer_spec((1, D_FF), layer),
            _layer_spec((1, D_MODEL), layer),
            _layer_spec((1, D_MODEL), layer),
        ],
        out_specs=out_specs,
        scratch_shapes=[pltpu.VMEM((t + 2 * HALO, D_MODEL), BF16), pltpu.VMEM((t, D_FF), BF16),
                        *scratch],
        compiler_params=_params("arbitrary"),
        name="conv_ffn_ln2",
    )(h, h, h, x, mods, mods, w_up, w_up, w_down, conv_w, conv_b, ln_g, ln_b)


def kernel(x_prompt, x_sample, cache_k, cache_v, c, c_ctx, w_ada, b_ada, w_in, attn_sink,
           hy_conv_w, hy_conv_b, filt_w1, filt_b1, filt_freq1, filt_w2, filt_b2, filt_freq2,
           filt_w3, hy_bias, w_pa, w_pb, w_out, ln1_g, ln1_b, w_up, ffn_conv_w, ffn_conv_b,
           w_down, ln2_g, ln2_b):
    cond = jnp.concatenate([c_ctx[None, :], c,
                            jnp.zeros((N_MOD_ROWS - 1 - DEC_BATCH, D_MODEL), F32)], axis=0)
    mods = _modulation(cond, w_ada, b_ada).reshape(DEPTH, N_MOD_ROWS, 1, 6 * D_MODEL)

    x, h = _prologue(x_prompt.reshape(N_CTX, D_MODEL), x_sample.reshape(N_LAT, D_MODEL), mods)

    cos_tab, sin_tab = _rope_tables()
    conv_args = {}
    for seq in (SEQ, DEC_SEQ):
        fwd, inv = _dft_tables(seq)
        hs = _filters(seq, filt_w1, filt_b1, filt_freq1, filt_w2, filt_b2, filt_freq2, filt_w3)
        conv_args[seq] = (fwd, inv, _filter_spectrum(seq, fwd, hs))

    cache_k = cache_k.reshape(DEC_BATCH, DEPTH, PAST_LEN, KV_W)
    cache_v = cache_v.reshape(DEC_BATCH, DEPTH, PAST_LEN, KV_W)
    hy_conv_b3 = hy_conv_b.reshape(DEPTH, 1, 3 * HY_W)
    ffn_conv_b3 = ffn_conv_b.reshape(DEPTH, 1, D_FF)
    ln1_g3, ln1_b3 = ln1_g.reshape(DEPTH, 1, D_MODEL), ln1_b.reshape(DEPTH, 1, D_MODEL)
    ln2_g3, ln2_b3 = ln2_g.reshape(DEPTH, 1, D_MODEL), ln2_b.reshape(DEPTH, 1, D_MODEL)

    band_bias = _band_bias()
    w_in, w_pa, w_pb, w_out, w_up, w_down = (
        w.astype(BF16) for w in (w_in, w_pa, w_pb, w_out, w_up, w_down))
    keys, values = [], []
    for l in range(DEPTH):
        q, k, vt, kf, vf, x0, vh = _in_proj(h, w_in, cos_tab, sin_tab, hy_conv_w, hy_conv_b3, l)
        keys.append(kf[:N_CTX])
        values.append(vf[:N_CTX])

        ybc = _long_conv(SEQ, BATCH, 0, vh, *conv_args[SEQ], hy_bias, l)
        ybl = _long_conv(DEC_SEQ, DEC_BATCH, N_CTX // DEC_SEQ, vh, *conv_args[DEC_SEQ], hy_bias, l)
        atc = _ctx_attention(attn_sink, q, k, vt, l)
        atl = _lat_attention(attn_sink, q, k, vt, cache_k, cache_v, band_bias, l)

        x, h2 = _merge(h, atc, atl, ybc, ybl, x0, x, mods, w_in, w_pa, w_pb, w_out,
                       ln1_g3, ln1_b3, l)
        x, h = _ffn(h2, x, mods, w_up, w_down, ffn_conv_w, ffn_conv_b3, ln2_g3, ln2_b3, l)

    y_prompt_rows, y_sample_rows = x, h
    y_prompt = y_prompt_rows.reshape(BATCH, SEQ, D_MODEL)
    y_sample = y_sample_rows.reshape(DEC_BATCH, DEC_SEQ, D_MODEL)
    to_cache = lambda ts: jnp.stack(ts, axis=1).reshape(BATCH, DEPTH, SEQ, N_KV_HEADS, HEAD_DIM)
    new_k = to_cache([t.reshape(BATCH, SEQ, KV_W) for t in keys])
    new_v = to_cache([t.reshape(BATCH, SEQ, KV_W) for t in values])
    return (y_prompt, y_sample, new_k, new_v)
```

```python
import functools
import math

import jax
import jax.numpy as jnp
from jax import lax
from jax.experimental import pallas as pl
from jax.experimental.pallas import tpu as pltpu

D_MODEL = 1024
BATCH = 16
SEQ = 256
DEPTH = 4
DEC_BATCH = 4
DEC_SEQ = 2048
PAST_LEN = 512
GRID_W = 64
HEAD_DIM = 64
N_HEADS = 8
N_KV_HEADS = 2
GROUP = N_HEADS // N_KV_HEADS
ATTN_W = N_HEADS * HEAD_DIM
KV_W = N_KV_HEADS * HEAD_DIM
WINDOW = 128
QBLOCK = 128
HY_W = 512
FILT_EMB = 33
FILT_EMB_PAD = 40
FILT_BANDS = (FILT_EMB - 1) // 2
FILT_HIDDEN = 64
D_FF = 2816
ROPE_BASE = 10000.0
LN_EPS = 1e-5
DEEPNORM_ALPHA = (2 * DEPTH) ** 0.25

N_CTX = BATCH * SEQ
N_LAT = DEC_BATCH * DEC_SEQ
N_TOK = N_CTX + N_LAT
N_MOD_ROWS = 8
QKV_W = ATTN_W + 2 * KV_W
HY_BLOCK = QKV_W
assert 3 * HY_W == 2 * HY_BLOCK
GATE_COL0 = QKV_W + 3 * HY_W
BAND = QBLOCK + 2 * WINDOW
MASK_VALUE = -1e30
LOG2_E = math.log2(math.e)

F32 = jnp.float32
BF16 = jnp.bfloat16

VMEM_LIMIT_BYTES = 56 * 1024 * 1024
MXU_COLS = 256
ROW_TILE = 1024
SUB_ROWS = SEQ
HALO = 16
RESIDENT = dict(pipeline_mode=pl.Buffered(1))


def _params(*semantics):
    return pltpu.CompilerParams(dimension_semantics=semantics, vmem_limit_bytes=VMEM_LIMIT_BYTES)


def _dot(a, b):
    return jnp.dot(a, b, preferred_element_type=F32)


def _dot_split(a, b):
    a_hi = a.astype(BF16)
    b_hi = b.astype(BF16)
    a_lo = (a - a_hi.astype(F32)).astype(BF16)
    b_lo = (b - b_hi.astype(F32)).astype(BF16)
    return _dot(a_hi, b_hi) + (_dot(a_hi, b_lo) + _dot(a_lo, b_hi))


def _dot_nt(a, b):
    return lax.dot_general(a, b, (((1,), (1,)), ((), ())), preferred_element_type=F32)


def _mod_row(tile_rows):
    n_ctx_tiles = N_CTX // tile_rows
    tiles_per_seq = DEC_SEQ // tile_rows

    def fn(m):
        return jnp.where(m < n_ctx_tiles, 0, 1 + (m - n_ctx_tiles) // tiles_per_seq)

    return fn


def _mod_spec(layer, tile_rows):
    grp = _mod_row(tile_rows)
    return pl.BlockSpec((None, None, 1, 6 * D_MODEL), lambda m, *_: (layer, grp(m), 0, 0))


def _layer_spec(shape, layer):
    zeros = (0,) * len(shape)
    return pl.BlockSpec((None,) + tuple(shape), lambda *_: (layer,) + zeros)


def _weight_spec(shape, layer, block_index=None):
    block_index = block_index or (0,) * len(shape)
    return pl.BlockSpec((None,) + tuple(shape), lambda *_: (layer,) + tuple(block_index), **RESIDENT)


def _layer_norm(y, g, b):
    mu = jnp.mean(y, axis=-1, keepdims=True)
    yc = y - mu
    var = jnp.mean(yc * yc, axis=-1, keepdims=True)
    return yc * lax.rsqrt(var + LN_EPS) * g + b


def _run_chains(chains, first_fn, finish_fn):
    pending = first_fn(*chains[0])
    for c in range(1, len(chains)):
        nxt = first_fn(*chains[c])
        finish_fn(*chains[c - 1], *pending)
        pending = nxt
    finish_fn(*chains[-1], *pending)


def _tile_edges(m, tile_rows):
    n_ctx_tiles = N_CTX // tile_rows
    tiles_per_seq = DEC_SEQ // tile_rows
    is_ctx = m < n_ctx_tiles
    lat_pos = (m - n_ctx_tiles) % tiles_per_seq
    return (is_ctx, jnp.logical_or(is_ctx, lat_pos == 0),
            jnp.logical_or(is_ctx, lat_pos == tiles_per_seq - 1))


def _halo_specs(tile_rows, n_cols):
    tb = tile_rows // HALO
    last = N_TOK // HALO - 1
    return (pl.BlockSpec((HALO, n_cols), lambda m: (jnp.maximum(m * tb - 1, 0), 0)),
            pl.BlockSpec((HALO, n_cols), lambda m: (jnp.minimum((m + 1) * tb, last), 0)))


def _fill_extended(hx_ref, hp_ref, h_ref, hn_ref):
    rows = h_ref.shape[0]
    hx_ref[0:HALO, :] = hp_ref[...]
    hx_ref[HALO:HALO + rows, :] = h_ref[...]
    hx_ref[HALO + rows:2 * HALO + rows, :] = hn_ref[...]


def _sub_edges(j, n_sub, is_ctx, seq_start, seq_end):
    return (seq_start if j == 0 else is_ctx), (seq_end if j == n_sub - 1 else is_ctx)


def _dwconv3(z, zero_top, zero_bot, w_ref, b_ref, cs):
    n = z.shape[0] - 2 * HALO
    top = jnp.where(zero_top, 0.0, z[HALO - 8:HALO])
    bot = jnp.where(zero_bot, 0.0, z[HALO + n:HALO + n + 8])
    z = jnp.concatenate([z[0:HALO - 8], top, z[HALO:HALO + n], bot, z[HALO + n + 8:]], axis=0)
    prev = pltpu.roll(z, 1, 0)[HALO:HALO + n]
    nxt = pltpu.roll(z, z.shape[0] - 1, 0)[HALO:HALO + n]
    return (prev * w_ref[0:1, cs] + z[HALO:HALO + n] * w_ref[1:2, cs] + nxt * w_ref[2:3, cs]
            + b_ref[:, cs])


def _mod_kernel(cond_ref, w_ref, b_ref, o_ref):
    c = cond_ref[...]
    s = (c * jax.nn.sigmoid(c)).astype(BF16)
    o_ref[...] = _dot(s, w_ref[...].astype(BF16)) + b_ref[...]


def _modulation(cond, w_ada, b_ada):
    n_col = 6 * D_MODEL // D_MODEL
    return pl.pallas_call(
        _mod_kernel,
        out_shape=jax.ShapeDtypeStruct((DEPTH, N_MOD_ROWS, 6 * D_MODEL), F32),
        grid=(DEPTH, n_col),
        in_specs=[
            pl.BlockSpec((N_MOD_ROWS, D_MODEL), lambda l, j: (0, 0)),
            pl.BlockSpec((None, D_MODEL, D_MODEL), lambda l, j: (l, 0, j)),
            pl.BlockSpec((None, 1, D_MODEL), lambda l, j: (l, 0, j)),
        ],
        out_specs=pl.BlockSpec((None, N_MOD_ROWS, D_MODEL), lambda l, j: (l, 0, j)),
        compiler_params=_params("parallel", "parallel"),
        name="modulation",
    )(cond, w_ada, b_ada.reshape(DEPTH, 1, 6 * D_MODEL))


PRO_TILE = 1024


def _prologue_kernel(xp_ref, xs_ref, mod_ref, x_ref, h_ref):
    m = pl.program_id(0)

    def emit(x):
        x_ref[...] = x
        sh = mod_ref[:, 0:D_MODEL]
        sc = mod_ref[:, D_MODEL:2 * D_MODEL]
        h_ref[...] = (x * (1.0 + sc) + sh).astype(BF16)

    @pl.when(m < N_CTX // PRO_TILE)
    def _():
        emit(xp_ref[...])

    @pl.when(m >= N_CTX // PRO_TILE)
    def _():
        emit(xs_ref[...])


def _prologue(xp, xs, mods):
    nct = N_CTX // PRO_TILE
    return pl.pallas_call(
        _prologue_kernel,
        out_shape=(jax.ShapeDtypeStruct((N_TOK, D_MODEL), F32),
                   jax.ShapeDtypeStruct((N_TOK, D_MODEL), BF16)),
        grid=(N_TOK // PRO_TILE,),
        in_specs=[
            pl.BlockSpec((PRO_TILE, D_MODEL), lambda m: (jnp.minimum(m, nct - 1), 0)),
            pl.BlockSpec((PRO_TILE, D_MODEL), lambda m: (jnp.maximum(m - nct, 0), 0)),
            _mod_spec(0, PRO_TILE),
        ],
        out_specs=(pl.BlockSpec((PRO_TILE, D_MODEL), lambda m: (m, 0)),
                   pl.BlockSpec((PRO_TILE, D_MODEL), lambda m: (m, 0))),
        compiler_params=_params("parallel"),
        name="prologue",
    )(xp, xs, mods)


ROPE_LANES = 2 * HEAD_DIM


def _in_proj_kernel(h_ref, hp_ref, hn_ref, wqkv_ref, why0_ref, why1_ref, cos_ref, sin_ref,
                    cw_ref, cb_ref, q_ref, k_ref, vt_ref, kf_ref, vf_ref, x0_ref, vh_ref, hx_ref):
    n_sub = ROW_TILE // SUB_ROWS
    ext = SUB_ROWS + 2 * HALO
    is_ctx, seq_start, seq_end = _tile_edges(pl.program_id(0), ROW_TILE)
    _fill_extended(hx_ref, hp_ref, h_ref, hn_ref)
    lane = lax.broadcasted_iota(jnp.int32, (SUB_ROWS, ROPE_LANES), 1)
    first_half = (lane & (HEAD_DIM // 4)) == 0
    scale = HEAD_DIM ** -0.5 * LOG2_E
    for j in range(n_sub):
        rs = slice(SUB_ROWS * j, SUB_ROWS * (j + 1))
        cos = cos_ref[rs, :]
        sin = sin_ref[rs, :]

        def rope(x):
            partner = jnp.where(first_half, pltpu.roll(x, ROPE_LANES - HEAD_DIM // 4, 1),
                                pltpu.roll(x, HEAD_DIM // 4, 1))
            return x * cos + partner * sin

        hm = h_ref[rs, :]
        for jj in range(ATTN_W // MXU_COLS):
            qq = _dot(hm, wqkv_ref[:, MXU_COLS * jj:MXU_COLS * (jj + 1)])
            for i in range(MXU_COLS // ROPE_LANES):
                c0 = MXU_COLS * jj + ROPE_LANES * i
                q_ref[rs, c0:c0 + ROPE_LANES] = (
                    rope(qq[:, ROPE_LANES * i:ROPE_LANES * (i + 1)]) * scale).astype(BF16)
        kv = _dot(hm, wqkv_ref[:, ATTN_W:QKV_W])
        k = kv[:, 0:KV_W]
        v = kv[:, KV_W:2 * KV_W]
        kf_ref[rs, :] = k
        vf_ref[rs, :] = v
        k_ref[rs, :] = rope(k).astype(BF16)
        vt_ref[:, rs] = v.T.astype(BF16)

        zero_top, zero_bot = _sub_edges(j, n_sub, is_ctx, seq_start, seq_end)
        hx = hx_ref[SUB_ROWS * j:SUB_ROWS * j + ext, :]

        def proj_conv(c0):
            w_ref = why0_ref if c0 < HY_BLOCK else why1_ref
            w = w_ref[:, c0 % HY_BLOCK:c0 % HY_BLOCK + MXU_COLS]
            return _dwconv3(_dot(hx, w), zero_top, zero_bot, cw_ref, cb_ref,
                            slice(c0, c0 + MXU_COLS))

        for jj in range(HY_W // MXU_COLS):
            c0 = MXU_COLS * jj
            x0_ref[rs, c0:c0 + MXU_COLS] = proj_conv(c0).astype(BF16)
            x1 = proj_conv(HY_W + c0)
            u = proj_conv(2 * HY_W + c0)
            vh_ref[rs, c0:c0 + MXU_COLS] = (x1 * u).astype(BF16)


def _in_proj(h, w_in, cos_tab, sin_tab, conv_w, conv_b, layer):
    t = ROW_TILE
    nct = N_CTX // t
    tps = DEC_SEQ // t
    tab = lambda m: (jnp.where(m < nct, 0, 1 + (m - nct) % tps), 0, 0)
    row = lambda m: (m, 0)
    bf = lambda w: jax.ShapeDtypeStruct((N_TOK, w), BF16)
    return pl.pallas_call(
        _in_proj_kernel,
        out_shape=(bf(ATTN_W), bf(KV_W),
                   jax.ShapeDtypeStruct((KV_W, N_TOK), BF16),
                   jax.ShapeDtypeStruct((N_TOK, KV_W), F32),
                   jax.ShapeDtypeStruct((N_TOK, KV_W), F32),
                   bf(HY_W), bf(HY_W)),
        grid=(N_TOK // t,),
        in_specs=[
            pl.BlockSpec((t, D_MODEL), row),
            *_halo_specs(t, D_MODEL),
            _weight_spec((D_MODEL, QKV_W), layer, (0, 0)),
            _weight_spec((D_MODEL, HY_BLOCK), layer, (0, 1)),
            _weight_spec((D_MODEL, HY_BLOCK), layer, (0, 2)),
            pl.BlockSpec((None, t, ROPE_LANES), tab),
            pl.BlockSpec((None, t, ROPE_LANES), tab),
            _layer_spec((3, 3 * HY_W), layer),
            _layer_spec((1, 3 * HY_W), layer),
        ],
        out_specs=(pl.BlockSpec((t, ATTN_W), row),
                   pl.BlockSpec((t, KV_W), row),
                   pl.BlockSpec((KV_W, t), lambda m: (0, m)),
                   pl.BlockSpec((t, KV_W), row),
                   pl.BlockSpec((t, KV_W), row),
                   pl.BlockSpec((t, HY_W), row),
                   pl.BlockSpec((t, HY_W), row)),
        scratch_shapes=[pltpu.VMEM((t + 2 * HALO, D_MODEL), BF16)],
        compiler_params=_params("parallel"),
        name="in_proj",
    )(h, h, h, w_in, w_in, w_in, cos_tab, sin_tab, conv_w, conv_b)


def _rope_tables():
    rows = DEC_SEQ // GRID_W
    r, col = jnp.meshgrid(jnp.arange(rows), jnp.arange(GRID_W), indexing='ij')
    pos = jnp.stack([r.reshape(-1), col.reshape(-1)], axis=-1).astype(F32)
    half = HEAD_DIM // 2
    inv_freq = 1.0 / (ROPE_BASE ** (jnp.arange(0, half, 2, dtype=F32) / half))
    ang = pos[:, :, None] * inv_freq
    ang = jnp.stack([ang, ang], axis=-2).reshape(DEC_SEQ, HEAD_DIM)
    cos = jnp.tile(jnp.cos(ang), (1, ROPE_LANES // HEAD_DIM))
    sin = jnp.tile(jnp.sin(ang), (1, ROPE_LANES // HEAD_DIM))
    lane = jnp.arange(ROPE_LANES)
    sin = jnp.where((lane & (HEAD_DIM // 4)) == 0, -sin, sin)
    n_slab = DEC_SEQ // ROW_TILE
    cos = jnp.concatenate([jnp.ones((1, ROW_TILE, ROPE_LANES), F32),
                           cos.reshape(n_slab, ROW_TILE, ROPE_LANES)], axis=0)
    sin = jnp.concatenate([jnp.zeros((1, ROW_TILE, ROPE_LANES), F32),
                           sin.reshape(n_slab, ROW_TILE, ROPE_LANES)], axis=0)
    return cos, sin


def _filter_kernel(feat_ref, t_ref, dec_ref, w1_ref, b1_ref, f1_ref, w2_ref, b2_ref, f2_ref,
                   w3_ref, o_ref):
    a = _dot_split(feat_ref[...], w1_ref[...]) + b1_ref[...]
    a = jnp.sin(f1_ref[...] * a)
    a = _dot_split(a, w2_ref[...]) + b2_ref[...]
    a = jnp.sin(f2_ref[...] * a)
    hh = _dot_split(a, w3_ref[...])
    window = jnp.exp(-t_ref[...] * dec_ref[...])
    h_fwd = hh[:, 0:HY_W] * window
    h_bwd = hh[:, HY_W:2 * HY_W] * window
    row = lax.broadcasted_iota(jnp.int32, h_bwd.shape, 0)
    h_bwd = jnp.where(row == 0, 0.0, h_bwd)
    o_ref[:, 0:HY_W] = (h_fwd + h_bwd).astype(BF16)
    o_ref[:, HY_W:2 * HY_W] = (h_fwd - h_bwd).astype(BF16)


def _filters(seq, w1, b1, f1, w2, b2, f2, w3):
    t = jnp.linspace(0.0, 1.0, seq, dtype=F32)[:, None]
    w = 2.0 * math.pi * jnp.arange(seq, dtype=F32) / seq
    f = jnp.linspace(1e-4, FILT_BANDS - 1, FILT_BANDS, dtype=F32)
    zr = w[:, None] * f[None, :]
    feats = jnp.concatenate([t, jnp.cos(zr), -jnp.sin(zr),
                             jnp.zeros((seq, FILT_EMB_PAD - FILT_EMB), F32)], axis=-1)
    target = 1e-2
    decay = jnp.abs(jnp.linspace(math.log(target) / 1.5, math.log(target) / 0.3, HY_W,
                                 dtype=F32))[None, :]
    w1p = jnp.pad(w1, ((0, 0), (0, FILT_EMB_PAD - FILT_EMB), (0, 0)))
    vec = lambda a: a.reshape(DEPTH, 1, FILT_HIDDEN)
    return pl.pallas_call(
        _filter_kernel,
        out_shape=jax.ShapeDtypeStruct((DEPTH, seq, 2 * HY_W), BF16),
        grid=(DEPTH,),
        in_specs=[
            pl.BlockSpec((seq, FILT_EMB_PAD), lambda l: (0, 0)),
            pl.BlockSpec((seq, 1), lambda l: (0, 0)),
            pl.BlockSpec((1, HY_W), lambda l: (0, 0)),
            pl.BlockSpec((None, FILT_EMB_PAD, FILT_HIDDEN), lambda l: (l, 0, 0)),
            pl.BlockSpec((None, 1, FILT_HIDDEN), lambda l: (l, 0, 0)),
            pl.BlockSpec((None, 1, FILT_HIDDEN), lambda l: (l, 0, 0)),
            pl.BlockSpec((None, FILT_HIDDEN, FILT_HIDDEN), lambda l: (l, 0, 0)),
            pl.BlockSpec((None, 1, FILT_HIDDEN), lambda l: (l, 0, 0)),
            pl.BlockSpec((None, 1, FILT_HIDDEN), lambda l: (l, 0, 0)),
            pl.BlockSpec((None, FILT_HIDDEN, 2 * HY_W), lambda l: (l, 0, 0)),
        ],
        out_specs=pl.BlockSpec((None, seq, 2 * HY_W), lambda l: (l, 0, 0)),
        compiler_params=_params("parallel"),
        name=f"hyena_filter_{seq}",
    )(feats, t, decay, w1p, vec(b1), vec(f1), w2, vec(b2), vec(f2), w3)


TABLE_ROWS = 16


def _phase(a, b, period):
    k = (a * b) & (period - 1)
    ang = k.astype(F32) * (2.0 * math.pi / period)
    return jnp.cos(ang), jnp.sin(ang)


def _table_kernel(o_ref, *, half, row_mult, row_add, col_mult, col_add, stack_rows, scale):
    period = 8 * half
    groups = half // TABLE_ROWS
    b0 = col_mult * lax.broadcasted_iota(jnp.int32, (TABLE_ROWS, half), 1) + col_add
    a0 = row_mult * lax.broadcasted_iota(jnp.int32, (TABLE_ROWS, half), 0) + row_add
    cos_a, sin_a = _phase(a0, b0, period)
    b1 = col_mult * lax.broadcasted_iota(jnp.int32, (groups, half), 1) + col_add
    a1 = row_mult * TABLE_ROWS * lax.broadcasted_iota(jnp.int32, (groups, half), 0)
    cos_b, sin_b = _phase(a1, b1, period)
    cos = cos_a[None] * cos_b[:, None, :] - sin_a[None] * sin_b[:, None, :]
    msin = -(sin_a[None] * cos_b[:, None, :] + cos_a[None] * sin_b[:, None, :])
    cos = (cos.reshape(half, half) * scale).astype(BF16)
    msin = (msin.reshape(half, half) * scale).astype(BF16)
    if stack_rows:
        o_ref[0:half, :] = cos
        o_ref[half:2 * half, :] = msin
    else:
        o_ref[:, 0:half] = cos
        o_ref[:, half:2 * half] = msin


def _dft_tables(seq):
    half = seq // 2

    def table(stack_rows, **kw):
        shape = (2 * half, half) if stack_rows else (half, 2 * half)
        return pl.pallas_call(
            functools.partial(_table_kernel, half=half, stack_rows=stack_rows, **kw),
            out_shape=jax.ShapeDtypeStruct(shape, BF16),
            compiler_params=_params(),
            name=f"dft_table_{seq}",
        )()

    freq = dict(row_mult=2, row_add=1)
    freq_cols = dict(col_mult=2, col_add=1)
    return (table(True, **freq, col_mult=2, col_add=0, scale=1.0),
            table(True, **freq, col_mult=2, col_add=1, scale=1.0),
            table(False, row_mult=2, row_add=0, **freq_cols, scale=1.0 / seq),
            table(False, row_mult=2, row_add=1, **freq_cols, scale=1.0 / seq))


def _spectrum_kernel(fe_ref, fo_ref, h_ref, g_ref, *, half):
    w = HY_W
    cos_even = _dot(fe_ref[0:half, :], h_ref[:, 0:w])
    sin_even = _dot(fe_ref[half:2 * half, :], h_ref[:, w:2 * w])
    cos_odd = _dot(fo_ref[0:half, :], h_ref[:, 2 * w:3 * w])
    sin_odd = _dot(fo_ref[half:2 * half, :], h_ref[:, 3 * w:4 * w])
    g_ref[0] = cos_even + cos_odd
    g_ref[1] = sin_even + sin_odd
    g_ref[2] = cos_even - cos_odd
    g_ref[3] = sin_odd - sin_even


def _filter_spectrum(seq, fwd_even, fwd_odd, hs):
    half = seq // 2
    table = pl.BlockSpec((2 * half, half), lambda l: (0, 0))
    return pl.pallas_call(
        functools.partial(_spectrum_kernel, half=half),
        out_shape=jax.ShapeDtypeStruct((DEPTH, 4, half, HY_W), F32),
        grid=(DEPTH,),
        in_specs=[table, table, pl.BlockSpec((None, half, 4 * HY_W), lambda l: (l, 0, 0))],
        out_specs=pl.BlockSpec((None, 4, half, HY_W), lambda l: (l, 0, 0, 0)),
        compiler_params=_params("parallel"),
        name=f"hyena_spectrum_{seq}",
    )(fwd_even, fwd_odd, hs.reshape(DEPTH, half, 4 * HY_W))


LONGCONV_SEQS_PER_STEP = {SEQ: 4, DEC_SEQ: 1}


def _longconv_kernel(v_ref, fe_ref, fo_ref, ie_ref, io_ref, g_ref, bias_ref, o_ref, *, half, n_seq):
    def forward(s, c):
        rows = slice(half * s, half * (s + 1))
        even = slice(MXU_COLS * c, MXU_COLS * (c + 1))
        odd = slice(HY_W + MXU_COLS * c, HY_W + MXU_COLS * (c + 1))
        return _dot(fe_ref[...], v_ref[rows, even]), _dot(fo_ref[...], v_ref[rows, odd])

    def finish(s, c, p, q):
        rows = slice(half * s, half * (s + 1))
        cs = slice(MXU_COLS * c, MXU_COLS * (c + 1))
        pr, pi = p[0:half], p[half:2 * half]
        qr, qi = q[0:half], q[half:2 * half]
        ur, ui = pr + qr, pi + qi
        wr, wi = pr - qr, qi - pi
        yr = ur * g_ref[0, :, cs] - ui * g_ref[1, :, cs]
        yi = ur * g_ref[1, :, cs] + ui * g_ref[0, :, cs]
        zr = wr * g_ref[2, :, cs] - wi * g_ref[3, :, cs]
        zi = wr * g_ref[3, :, cs] + wi * g_ref[2, :, cs]
        spec_even = jnp.concatenate([yr + zr, yi - zi], axis=0).astype(BF16)
        spec_odd = jnp.concatenate([yr - zr, yi + zi], axis=0).astype(BF16)
        bias = bias_ref[:, cs]
        for tab_ref, spec, c0 in ((ie_ref, spec_even, 0), (io_ref, spec_odd, HY_W)):
            cols = slice(c0 + MXU_COLS * c, c0 + MXU_COLS * (c + 1))
            y = _dot(tab_ref[...], spec) + bias * v_ref[rows, cols].astype(F32)
            o_ref[rows, cols] = y.astype(BF16)

    _run_chains([(s, c) for s in range(n_seq) for c in range(HY_W // MXU_COLS)], forward, finish)


def _long_conv(seq, n_rows, row0, vh, tables, spectrum, hy_bias, layer):
    half = seq // 2
    n_seq = LONGCONV_SEQS_PER_STEP[seq]
    t = n_seq * half
    fwd = pl.BlockSpec((2 * half, half), lambda b: (0, 0), **RESIDENT)
    inv = pl.BlockSpec((half, 2 * half), lambda b: (0, 0), **RESIDENT)
    out = pl.pallas_call(
        functools.partial(_longconv_kernel, half=half, n_seq=n_seq),
        out_shape=jax.ShapeDtypeStruct((n_rows // 2, 2 * HY_W), BF16),
        grid=(n_rows // 2 // t,),
        in_specs=[
            pl.BlockSpec((t, 2 * HY_W), lambda b: (row0 // 2 // t + b, 0)),
            fwd, fwd, inv, inv,
            pl.BlockSpec((None, 4, half, HY_W), lambda b: (layer, 0, 0, 0), **RESIDENT),
            pl.BlockSpec((None, 1, HY_W), lambda b: (layer, 0, 0)),
        ],
        out_specs=pl.BlockSpec((t, 2 * HY_W), lambda b: (b, 0)),
        compiler_params=_params("parallel"),
        name=f"hyena_longconv_{seq}",
    )(vh.reshape(N_TOK // 2, 2 * HY_W), *tables, spectrum, hy_bias.reshape(DEPTH, 1, HY_W))
    return out.reshape(n_rows, HY_W)


ONES_ROWS = 16
CTX_SEQS_PER_STEP = 2
QBLOCKS_PER_STEP = 2


def _head_rows(q, j):
    return jnp.concatenate([q[:, HEAD_DIM * hd:HEAD_DIM * (hd + 1)]
                            for hd in range(GROUP * j, GROUP * (j + 1))], axis=0)


def _lane_sink(sink_ref, layer, j, block):
    head = lax.broadcasted_iota(jnp.int32, (1, GROUP * block), 1) // block
    sink = jnp.zeros((1, GROUP * block), F32)
    for g in range(GROUP):
        sink = jnp.where(head == g, sink_ref[layer, GROUP * j + g] * LOG2_E, sink)
    return sink


def _with_ones(vt):
    return jnp.concatenate([vt, jnp.ones((ONES_ROWS, vt.shape[1]), BF16)], axis=0)


def _softmax_pv_t(parts, sink):
    m = sink
    for lg, _ in parts:
        m = jnp.maximum(m, jnp.max(lg, axis=0, keepdims=True))
    out = None
    for lg, vt in parts:
        pv = _dot(vt, jnp.exp2(lg - m).astype(BF16))
        out = pv if out is None else out + pv
    den = out[HEAD_DIM:HEAD_DIM + 1, :] + jnp.exp2(sink - m)
    return out[0:HEAD_DIM, :] / den


def _store_heads(o_ref, rows, j, block, o_t):
    for g in range(GROUP):
        hd = GROUP * j + g
        o_ref[rows, HEAD_DIM * hd:HEAD_DIM * (hd + 1)] = (
            o_t[:, block * g:block * (g + 1)].T.astype(BF16))


def _ctx_attn_kernel(sink_ref, q_ref, k_ref, vt_ref, o_ref, *, layer):
    def logits(n, j):
        rows = slice(SEQ * n, SEQ * (n + 1))
        hs = slice(HEAD_DIM * j, HEAD_DIM * (j + 1))
        return _dot_nt(k_ref[rows, hs], _head_rows(q_ref[rows, :], j)), _with_ones(vt_ref[hs, rows])

    def finish(n, j, s, vt):
        o_t = _softmax_pv_t([(s, vt)], _lane_sink(sink_ref, layer, j, SEQ))
        _store_heads(o_ref, slice(SEQ * n, SEQ * (n + 1)), j, SEQ, o_t)

    _run_chains([(n, j) for n in range(CTX_SEQS_PER_STEP) for j in range(N_KV_HEADS)],
                logits, finish)


def _ctx_attention(sink, q, k, vt, layer):
    t = CTX_SEQS_PER_STEP * SEQ
    row = lambda b: (b, 0)
    return pl.pallas_call(
        functools.partial(_ctx_attn_kernel, layer=layer),
        out_shape=jax.ShapeDtypeStruct((N_CTX, ATTN_W), BF16),
        grid=(N_CTX // t,),
        in_specs=[
            pl.BlockSpec(memory_space=pltpu.SMEM),
            pl.BlockSpec((t, ATTN_W), row),
            pl.BlockSpec((t, KV_W), row),
            pl.BlockSpec((KV_W, t), lambda b: (0, b)),
        ],
        out_specs=pl.BlockSpec((t, ATTN_W), row),
        compiler_params=_params("parallel"),
        name="context_attention",
    )(sink, q, k, vt)


def _band_bias():
    kpos = jnp.arange(BAND)[None, :, None] - jnp.arange(3)[:, None, None] * WINDOW
    qpos = (jnp.arange(GROUP * QBLOCK) & (QBLOCK - 1))[None, None, :]
    return jnp.where(jnp.abs(qpos - kpos) <= WINDOW, 0.0, MASK_VALUE).astype(F32)


def _lat_attn_kernel(sink_ref, q_ref, k_ref, vt_ref, ck_ref, cv_ref, bias_ref, o_ref,
                     kc_ref, vct_ref, *, layer):
    i = pl.program_id(1)

    @pl.when(i == 0)
    def _():
        kc_ref[...] = ck_ref[...].astype(BF16)
        vct = cv_ref[...].T.astype(BF16)
        for j in range(N_KV_HEADS):
            vct_ref[j] = _with_ones(vct[HEAD_DIM * j:HEAD_DIM * (j + 1), :])

    def logits(n, j):
        start = (i * QBLOCKS_PER_STEP + n) * QBLOCK
        ws = pl.multiple_of(jnp.clip(start - WINDOW, 0, DEC_SEQ - BAND), QBLOCK)
        placement = jnp.where(start < WINDOW, 0, jnp.where(start + QBLOCK + WINDOW > DEC_SEQ, 2, 1))
        q4 = _head_rows(q_ref[QBLOCK * n:QBLOCK * (n + 1), :], j)
        hs = slice(HEAD_DIM * j, HEAD_DIM * (j + 1))
        s_ctx = _dot_nt(kc_ref[:, hs], q4)
        s_band = _dot_nt(k_ref[pl.ds(ws, BAND), hs], q4) + bias_ref[placement]
        return s_ctx, s_band, _with_ones(vt_ref[hs, pl.ds(ws, BAND)])

    def finish(n, j, s_ctx, s_band, vbt):
        o_t = _softmax_pv_t([(s_ctx, vct_ref[j]), (s_band, vbt)],
                            _lane_sink(sink_ref, layer, j, QBLOCK))
        _store_heads(o_ref, slice(QBLOCK * n, QBLOCK * (n + 1)), j, QBLOCK, o_t)

    _run_chains([(n, j) for n in range(QBLOCKS_PER_STEP) for j in range(N_KV_HEADS)],
                logits, finish)


def _lat_attention(sink, q, k, vt, cache_k, cache_v, band_bias, layer):
    t = QBLOCKS_PER_STEP * QBLOCK
    steps = DEC_SEQ // t
    seq_block = N_CTX // DEC_SEQ
    return pl.pallas_call(
        functools.partial(_lat_attn_kernel, layer=layer),
        out_shape=jax.ShapeDtypeStruct((N_LAT, ATTN_W), BF16),
        grid=(DEC_BATCH, steps),
        in_specs=[
            pl.BlockSpec(memory_space=pltpu.SMEM),
            pl.BlockSpec((t, ATTN_W), lambda b, i: (N_CTX // t + b * steps + i, 0)),
            pl.BlockSpec((DEC_SEQ, KV_W), lambda b, i: (seq_block + b, 0)),
            pl.BlockSpec((KV_W, DEC_SEQ), lambda b, i: (0, seq_block + b)),
            pl.BlockSpec((None, None, PAST_LEN, KV_W), lambda b, i: (b, layer, 0, 0)),
            pl.BlockSpec((None, None, PAST_LEN, KV_W), lambda b, i: (b, layer, 0, 0)),
            pl.BlockSpec((3, BAND, GROUP * QBLOCK), lambda b, i: (0, 0, 0)),
        ],
        out_specs=pl.BlockSpec((t, ATTN_W), lambda b, i: (b * steps + i, 0)),
        scratch_shapes=[pltpu.VMEM((PAST_LEN, KV_W), BF16),
                        pltpu.VMEM((N_KV_HEADS, HEAD_DIM + ONES_ROWS, PAST_LEN), BF16)],
        compiler_params=_params("parallel", "arbitrary"),
        name="latent_attention",
    )(sink, q, k, vt, cache_k, cache_v, band_bias)


N_GATE_CHUNKS = D_MODEL // MXU_COLS


def _merge_kernel(h_ref, atc_ref, atl_ref, ybc_ref, ybl_ref, x0_ref, x_ref, mod_ref, *rest):
    wga_refs = rest[0:N_GATE_CHUNKS]
    wgb_refs = rest[N_GATE_CHUNKS:2 * N_GATE_CHUNKS]
    wpa_ref, wpb_ref, wo_ref, g_ref, b_ref, xo_ref, ho_ref, mg_ref = rest[2 * N_GATE_CHUNKS:]
    is_ctx = pl.program_id(0) < N_CTX // ROW_TILE
    g1 = mod_ref[:, 2 * D_MODEL:3 * D_MODEL]
    sh2 = mod_ref[:, 3 * D_MODEL:4 * D_MODEL]
    sc2 = mod_ref[:, 4 * D_MODEL:5 * D_MODEL]
    for s in range(ROW_TILE // SUB_ROWS):
        rs = slice(SUB_ROWS * s, SUB_ROWS * (s + 1))
        h = h_ref[rs, :]
        attn = jnp.where(is_ctx, atc_ref[rs, :], atl_ref[rs, :])
        conv = jnp.where(is_ctx, ybc_ref[rs, :], ybl_ref[rs, :])
        hy = (x0_ref[rs, :].astype(F32) * conv.astype(F32)).astype(BF16)
        for j in range(N_GATE_CHUNKS):
            cs = slice(MXU_COLS * j, MXU_COLS * (j + 1))
            ga = jax.nn.sigmoid(_dot(h, wga_refs[j][...]))
            gb = jax.nn.sigmoid(_dot(h, wgb_refs[j][...]))
            mg_ref[rs, cs] = (ga * _dot(attn, wpa_ref[:, cs])
                              + gb * _dot(hy, wpb_ref[:, cs])).astype(BF16)
        sub = _dot(mg_ref[rs, :], wo_ref[...])
        x = _layer_norm(DEEPNORM_ALPHA * x_ref[rs, :] + g1 * sub, g_ref[...], b_ref[...])
        xo_ref[rs, :] = x
        ho_ref[rs, :] = (x * (1.0 + sc2) + sh2).astype(BF16)


def _merge(h, atc, atl, ybc, ybl, x0, x, mods, w_in, wpa, wpb, wo, ln_g, ln_b, layer):
    t = ROW_TILE
    nct = N_CTX // t
    row = lambda m: (m, 0)
    ctx_row = lambda m: (jnp.minimum(m, nct - 1), 0)
    lat_row = lambda m: (jnp.maximum(m - nct, 0), 0)
    gate_block0 = GATE_COL0 // MXU_COLS
    gate_specs = [_weight_spec((D_MODEL, MXU_COLS), layer, (0, gate_block0 + j))
                  for j in range(2 * N_GATE_CHUNKS)]
    return pl.pallas_call(
        _merge_kernel,
        out_shape=(jax.ShapeDtypeStruct((N_TOK, D_MODEL), F32),
                   jax.ShapeDtypeStruct((N_TOK, D_MODEL), BF16)),
        grid=(N_TOK // t,),
        in_specs=[
            pl.BlockSpec((t, D_MODEL), row),
            pl.BlockSpec((t, ATTN_W), ctx_row),
            pl.BlockSpec((t, ATTN_W), lat_row),
            pl.BlockSpec((t, HY_W), ctx_row),
            pl.BlockSpec((t, HY_W), lat_row),
            pl.BlockSpec((t, HY_W), row),
            pl.BlockSpec((t, D_MODEL), row),
            _mod_spec(layer, t),
            *gate_specs,
            _weight_spec((ATTN_W, D_MODEL), layer),
            _weight_spec((HY_W, D_MODEL), layer),
            _weight_spec((D_MODEL, D_MODEL), layer),
            _layer_spec((1, D_MODEL), layer),
            _layer_spec((1, D_MODEL), layer),
        ],
        out_specs=(pl.BlockSpec((t, D_MODEL), row), pl.BlockSpec((t, D_MODEL), row)),
        scratch_shapes=[pltpu.VMEM((t, D_MODEL), BF16)],
        compiler_params=_params("parallel"),
        name="merge_ln1",
    )(h, atc, atl, ybc, ybl, x0, x, mods, *([w_in] * (2 * N_GATE_CHUNKS)), wpa, wpb, wo,
      ln_g, ln_b)


def _ffn_kernel(h_ref, hp_ref, hn_ref, x_ref, mod_ref, nmod_ref, wg_ref, wv_ref, wd_ref,
                cw_ref, cb_ref, g_ref, b_ref, *rest, last_layer):
    if last_layer:
        yp_ref, ys_ref, hx_ref, a_ref, xo_ref = rest
    else:
        xo_ref, ho_ref, hx_ref, a_ref = rest
    n_sub = ROW_TILE // SUB_ROWS
    ext = SUB_ROWS + 2 * HALO
    is_ctx, seq_start, seq_end = _tile_edges(pl.program_id(0), ROW_TILE)
    _fill_extended(hx_ref, hp_ref, h_ref, hn_ref)
    g2 = mod_ref[:, 5 * D_MODEL:6 * D_MODEL]
    sh = nmod_ref[:, 0:D_MODEL]
    sc = nmod_ref[:, D_MODEL:2 * D_MODEL]
    for j in range(n_sub):
        rs = slice(SUB_ROWS * j, SUB_ROWS * (j + 1))
        zero_top, zero_bot = _sub_edges(j, n_sub, is_ctx, seq_start, seq_end)
        hx = hx_ref[SUB_ROWS * j:SUB_ROWS * j + ext, :]
        hm = h_ref[rs, :]
        for c in range(D_FF // MXU_COLS):
            cs = slice(MXU_COLS * c, MXU_COLS * (c + 1))
            gate = _dwconv3(_dot(hx, wg_ref[:, cs]), zero_top, zero_bot, cw_ref, cb_ref, cs)
            gelu = 0.5 * gate * (1.0 + lax.erf(gate * math.sqrt(0.5)))
            a_ref[rs, cs] = (gelu * _dot(hm, wv_ref[:, cs])).astype(BF16)
        sub = _dot(a_ref[rs, :], wd_ref[...])
        x = _layer_norm(DEEPNORM_ALPHA * x_ref[rs, :] + g2 * sub, g_ref[...], b_ref[...])
        xo_ref[rs, :] = x
        if not last_layer:
            ho_ref[rs, :] = (x * (1.0 + sc) + sh).astype(BF16)

    if last_layer:
        @pl.when(is_ctx)
        def _():
            yp_ref[...] = xo_ref[...]

        @pl.when(jnp.logical_not(is_ctx))
        def _():
            ys_ref[...] = xo_ref[...]


def _ffn(h, x, mods, w_up, w_down, conv_w, conv_b, ln_g, ln_b, layer):
    t = ROW_TILE
    nct = N_CTX // t
    row = lambda m: (m, 0)
    last_layer = layer == DEPTH - 1
    if last_layer:
        out_shape = (jax.ShapeDtypeStruct((N_CTX, D_MODEL), F32),
                     jax.ShapeDtypeStruct((N_LAT, D_MODEL), F32))
        out_specs = (pl.BlockSpec((t, D_MODEL), lambda m: (jnp.minimum(m, nct - 1), 0)),
                     pl.BlockSpec((t, D_MODEL), lambda m: (jnp.maximum(m - nct, 0), 0)))
        scratch = [pltpu.VMEM((t, D_MODEL), F32)]
    else:
        out_shape = (jax.ShapeDtypeStruct((N_TOK, D_MODEL), F32),
                     jax.ShapeDtypeStruct((N_TOK, D_MODEL), BF16))
        out_specs = (pl.BlockSpec((t, D_MODEL), row), pl.BlockSpec((t, D_MODEL), row))
        scratch = []
    return pl.pallas_call(
        functools.partial(_ffn_kernel, last_layer=last_layer),
        out_shape=out_shape,
        grid=(N_TOK // t,),
        in_specs=[
            pl.BlockSpec((t, D_MODEL), row),
            *_halo_specs(t, D_MODEL),
            pl.BlockSpec((t, D_MODEL), row),
            _mod_spec(layer, t),
            _mod_spec(min(layer + 1, DEPTH - 1), t),
            _weight_spec((D_MODEL, D_FF), layer, (0, 0)),
            _weight_spec((D_MODEL, D_FF), layer, (0, 1)),
            _weight_spec((D_FF, D_MODEL), layer),
            _layer_spec((3, D_FF), layer),
            _layer_spec((1, D_FF), layer),
            _layer_spec((1, D_MODEL), layer),
            _layer_spec((1, D_MODEL), layer),
        ],
        out_specs=out_specs,
        scratch_shapes=[pltpu.VMEM((t + 2 * HALO, D_MODEL), BF16), pltpu.VMEM((t, D_FF), BF16),
                        *scratch],
        compiler_params=_params("arbitrary"),
        name="conv_ffn_ln2",
    )(h, h, h, x, mods, mods, w_up, w_up, w_down, conv_w, conv_b, ln_g, ln_b)


def kernel(x_prompt, x_sample, cache_k, cache_v, c, c_ctx, w_ada, b_ada, w_in, attn_sink,
           hy_conv_w, hy_conv_b, filt_w1, filt_b1, filt_freq1, filt_w2, filt_b2, filt_freq2,
           filt_w3, hy_bias, w_pa, w_pb, w_out, ln1_g, ln1_b, w_up, ffn_conv_w, ffn_conv_b,
           w_down, ln2_g, ln2_b):
    cond = jnp.concatenate([c_ctx[None, :], c,
                            jnp.zeros((N_MOD_ROWS - 1 - DEC_BATCH, D_MODEL), F32)], axis=0)
    mods = _modulation(cond, w_ada, b_ada).reshape(DEPTH, N_MOD_ROWS, 1, 6 * D_MODEL)

    x, h = _prologue(x_prompt.reshape(N_CTX, D_MODEL), x_sample.reshape(N_LAT, D_MODEL), mods)

    cos_tab, sin_tab = _rope_tables()
    conv_args = {}
    for seq in (SEQ, DEC_SEQ):
        tables = _dft_tables(seq)
        hs = _filters(seq, filt_w1, filt_b1, filt_freq1, filt_w2, filt_b2, filt_freq2, filt_w3)
        conv_args[seq] = (tables, _filter_spectrum(seq, tables[0], tables[1], hs))

    cache_k = cache_k.reshape(DEC_BATCH, DEPTH, PAST_LEN, KV_W)
    cache_v = cache_v.reshape(DEC_BATCH, DEPTH, PAST_LEN, KV_W)
    hy_conv_b3 = hy_conv_b.reshape(DEPTH, 1, 3 * HY_W)
    ffn_conv_b3 = ffn_conv_b.reshape(DEPTH, 1, D_FF)
    ln1_g3, ln1_b3 = ln1_g.reshape(DEPTH, 1, D_MODEL), ln1_b.reshape(DEPTH, 1, D_MODEL)
    ln2_g3, ln2_b3 = ln2_g.reshape(DEPTH, 1, D_MODEL), ln2_b.reshape(DEPTH, 1, D_MODEL)

    band_bias = _band_bias()
    w_in, w_pa, w_pb, w_out, w_up, w_down = (
        w.astype(BF16) for w in (w_in, w_pa, w_pb, w_out, w_up, w_down))
    keys, values = [], []
    for l in range(DEPTH):
        q, k, vt, kf, vf, x0, vh = _in_proj(h, w_in, cos_tab, sin_tab, hy_conv_w, hy_conv_b3, l)
        keys.append(kf[:N_CTX])
        values.append(vf[:N_CTX])

        ybc = _long_conv(SEQ, N_CTX, 0, vh, *conv_args[SEQ], hy_bias, l)
        ybl = _long_conv(DEC_SEQ, N_LAT, N_CTX, vh, *conv_args[DEC_SEQ], hy_bias, l)
        atc = _ctx_attention(attn_sink, q, k, vt, l)
        atl = _lat_attention(attn_sink, q, k, vt, cache_k, cache_v, band_bias, l)

        x, h2 = _merge(h, atc, atl, ybc, ybl, x0, x, mods, w_in, w_pa, w_pb, w_out,
                       ln1_g3, ln1_b3, l)
        x, h = _ffn(h2, x, mods, w_up, w_down, ffn_conv_w, ffn_conv_b3, ln2_g3, ln2_b3, l)

    y_prompt_rows, y_sample_rows = x, h
    y_prompt = y_prompt_rows.reshape(BATCH, SEQ, D_MODEL)
    y_sample = y_sample_rows.reshape(DEC_BATCH, DEC_SEQ, D_MODEL)
    to_cache = lambda ts: jnp.stack(ts, axis=1).reshape(BATCH, DEPTH, SEQ, N_KV_HEADS, HEAD_DIM)
    new_k = to_cache([t.reshape(BATCH, SEQ, KV_W) for t in keys])
    new_v = to_cache([t.reshape(BATCH, SEQ, KV_W) for t in values])
    return (y_prompt, y_sample, new_k, new_v)
```

```python
import functools
import math

import jax
import jax.numpy as jnp
from jax import lax
from jax.experimental import pallas as pl
from jax.experimental.pallas import tpu as pltpu

D_MODEL = 1024
BATCH = 16
SEQ = 256
DEPTH = 4
DEC_BATCH = 4
DEC_SEQ = 2048
PAST_LEN = 512
GRID_W = 64
HEAD_DIM = 64
N_HEADS = 8
N_KV_HEADS = 2
GROUP = N_HEADS // N_KV_HEADS
ATTN_W = N_HEADS * HEAD_DIM
KV_W = N_KV_HEADS * HEAD_DIM
WINDOW = 128
QBLOCK = 128
HY_W = 512
FILT_EMB = 33
FILT_EMB_PAD = 40
FILT_BANDS = (FILT_EMB - 1) // 2
FILT_HIDDEN = 64
D_FF = 2816
ROPE_BASE = 10000.0
LN_EPS = 1e-5
DEEPNORM_ALPHA = (2 * DEPTH) ** 0.25

N_CTX = BATCH * SEQ
N_LAT = DEC_BATCH * DEC_SEQ
N_TOK = N_CTX + N_LAT
N_MOD_ROWS = 8
QKV_W = ATTN_W + 2 * KV_W
HY_BLOCK = QKV_W
assert 3 * HY_W == 2 * HY_BLOCK
GATE_COL0 = QKV_W + 3 * HY_W
BAND = QBLOCK + 2 * WINDOW
MASK_VALUE = -1e30
LOG2_E = math.log2(math.e)

F32 = jnp.float32
BF16 = jnp.bfloat16

VMEM_LIMIT_BYTES = 56 * 1024 * 1024
MXU_COLS = 256
ROW_TILE = 1024
SUB_ROWS = SEQ
HALO = 16
RESIDENT = dict(pipeline_mode=pl.Buffered(1))


def _params(*semantics):
    return pltpu.CompilerParams(dimension_semantics=semantics, vmem_limit_bytes=VMEM_LIMIT_BYTES)


def _dot(a, b):
    return jnp.dot(a, b, preferred_element_type=F32)


def _dot_split(a, b):
    a_hi = a.astype(BF16)
    b_hi = b.astype(BF16)
    a_lo = (a - a_hi.astype(F32)).astype(BF16)
    b_lo = (b - b_hi.astype(F32)).astype(BF16)
    return _dot(a_hi, b_hi) + (_dot(a_hi, b_lo) + _dot(a_lo, b_hi))


def _dot_nt(a, b):
    return lax.dot_general(a, b, (((1,), (1,)), ((), ())), preferred_element_type=F32)


def _mod_row(tile_rows):
    n_ctx_tiles = N_CTX // tile_rows
    tiles_per_seq = DEC_SEQ // tile_rows

    def fn(m):
        return jnp.where(m < n_ctx_tiles, 0, 1 + (m - n_ctx_tiles) // tiles_per_seq)

    return fn


def _mod_spec(layer, tile_rows):
    grp = _mod_row(tile_rows)
    return pl.BlockSpec((None, None, 1, 6 * D_MODEL), lambda m, *_: (layer, grp(m), 0, 0))


def _layer_spec(shape, layer):
    zeros = (0,) * len(shape)
    return pl.BlockSpec((None,) + tuple(shape), lambda *_: (layer,) + zeros)


def _weight_spec(shape, layer, block_index=None):
    block_index = block_index or (0,) * len(shape)
    return pl.BlockSpec((None,) + tuple(shape), lambda *_: (layer,) + tuple(block_index), **RESIDENT)


def _layer_norm(y, g, b):
    mu = jnp.mean(y, axis=-1, keepdims=True)
    yc = y - mu
    var = jnp.mean(yc * yc, axis=-1, keepdims=True)
    return yc * lax.rsqrt(var + LN_EPS) * g + b


def _run_chains(chains, first_fn, finish_fn):
    pending = first_fn(*chains[0])
    for c in range(1, len(chains)):
        nxt = first_fn(*chains[c])
        finish_fn(*chains[c - 1], *pending)
        pending = nxt
    finish_fn(*chains[-1], *pending)


def _tile_edges(m, tile_rows):
    n_ctx_tiles = N_CTX // tile_rows
    tiles_per_seq = DEC_SEQ // tile_rows
    is_ctx = m < n_ctx_tiles
    lat_pos = (m - n_ctx_tiles) % tiles_per_seq
    return (is_ctx, jnp.logical_or(is_ctx, lat_pos == 0),
            jnp.logical_or(is_ctx, lat_pos == tiles_per_seq - 1))


def _halo_specs(tile_rows, n_cols):
    tb = tile_rows // HALO
    last = N_TOK // HALO - 1
    return (pl.BlockSpec((HALO, n_cols), lambda m: (jnp.maximum(m * tb - 1, 0), 0)),
            pl.BlockSpec((HALO, n_cols), lambda m: (jnp.minimum((m + 1) * tb, last), 0)))


def _fill_extended(hx_ref, hp_ref, h_ref, hn_ref):
    rows = h_ref.shape[0]
    hx_ref[0:HALO, :] = hp_ref[...]
    hx_ref[HALO:HALO + rows, :] = h_ref[...]
    hx_ref[HALO + rows:2 * HALO + rows, :] = hn_ref[...]


def _sub_edges(j, n_sub, is_ctx, seq_start, seq_end):
    return (seq_start if j == 0 else is_ctx), (seq_end if j == n_sub - 1 else is_ctx)


def _dwconv3(z, zero_top, zero_bot, w_ref, b_ref, cs):
    n = z.shape[0] - 2 * HALO
    top = jnp.where(zero_top, 0.0, z[HALO - 8:HALO])
    bot = jnp.where(zero_bot, 0.0, z[HALO + n:HALO + n + 8])
    z = jnp.concatenate([z[0:HALO - 8], top, z[HALO:HALO + n], bot, z[HALO + n + 8:]], axis=0)
    prev = pltpu.roll(z, 1, 0)[HALO:HALO + n]
    nxt = pltpu.roll(z, z.shape[0] - 1, 0)[HALO:HALO + n]
    return (prev * w_ref[0:1, cs] + z[HALO:HALO + n] * w_ref[1:2, cs] + nxt * w_ref[2:3, cs]
            + b_ref[:, cs])


def _mod_kernel(cond_ref, w_ref, b_ref, o_ref):
    c = cond_ref[...]
    s = (c * jax.nn.sigmoid(c)).astype(BF16)
    o_ref[...] = _dot(s, w_ref[...].astype(BF16)) + b_ref[...]


def _modulation(cond, w_ada, b_ada):
    n_col = 6 * D_MODEL // D_MODEL
    return pl.pallas_call(
        _mod_kernel,
        out_shape=jax.ShapeDtypeStruct((DEPTH, N_MOD_ROWS, 6 * D_MODEL), F32),
        grid=(DEPTH, n_col),
        in_specs=[
            pl.BlockSpec((N_MOD_ROWS, D_MODEL), lambda l, j: (0, 0)),
            pl.BlockSpec((None, D_MODEL, D_MODEL), lambda l, j: (l, 0, j)),
            pl.BlockSpec((None, 1, D_MODEL), lambda l, j: (l, 0, j)),
        ],
        out_specs=pl.BlockSpec((None, N_MOD_ROWS, D_MODEL), lambda l, j: (l, 0, j)),
        compiler_params=_params("parallel", "parallel"),
        name="modulation",
    )(cond, w_ada, b_ada.reshape(DEPTH, 1, 6 * D_MODEL))


PRO_TILE = 1024


def _prologue_kernel(xp_ref, xs_ref, mod_ref, x_ref, h_ref):
    m = pl.program_id(0)

    def emit(x):
        x_ref[...] = x
        sh = mod_ref[:, 0:D_MODEL]
        sc = mod_ref[:, D_MODEL:2 * D_MODEL]
        h_ref[...] = (x * (1.0 + sc) + sh).astype(BF16)

    @pl.when(m < N_CTX // PRO_TILE)
    def _():
        emit(xp_ref[...])

    @pl.when(m >= N_CTX // PRO_TILE)
    def _():
        emit(xs_ref[...])


def _prologue(xp, xs, mods):
    nct = N_CTX // PRO_TILE
    return pl.pallas_call(
        _prologue_kernel,
        out_shape=(jax.ShapeDtypeStruct((N_TOK, D_MODEL), F32),
                   jax.ShapeDtypeStruct((N_TOK, D_MODEL), BF16)),
        grid=(N_TOK // PRO_TILE,),
        in_specs=[
            pl.BlockSpec((PRO_TILE, D_MODEL), lambda m: (jnp.minimum(m, nct - 1), 0)),
            pl.BlockSpec((PRO_TILE, D_MODEL), lambda m: (jnp.maximum(m - nct, 0), 0)),
            _mod_spec(0, PRO_TILE),
        ],
        out_specs=(pl.BlockSpec((PRO_TILE, D_MODEL), lambda m: (m, 0)),
                   pl.BlockSpec((PRO_TILE, D_MODEL), lambda m: (m, 0))),
        compiler_params=_params("parallel"),
        name="prologue",
    )(xp, xs, mods)


ROPE_LANES = 2 * HEAD_DIM


def _in_proj_kernel(h_ref, hp_ref, hn_ref, wqkv_ref, why0_ref, why1_ref, cos_ref, sin_ref,
                    cw_ref, cb_ref, q_ref, k_ref, vt_ref, kf_ref, vf_ref, x0_ref, vh_ref, hx_ref):
    n_sub = ROW_TILE // SUB_ROWS
    ext = SUB_ROWS + 2 * HALO
    is_ctx, seq_start, seq_end = _tile_edges(pl.program_id(0), ROW_TILE)
    _fill_extended(hx_ref, hp_ref, h_ref, hn_ref)
    lane = lax.broadcasted_iota(jnp.int32, (SUB_ROWS, ROPE_LANES), 1)
    first_half = (lane & (HEAD_DIM // 4)) == 0
    scale = HEAD_DIM ** -0.5 * LOG2_E
    for j in range(n_sub):
        rs = slice(SUB_ROWS * j, SUB_ROWS * (j + 1))
        cos = cos_ref[rs, :]
        sin = sin_ref[rs, :]

        def rope(x):
            partner = jnp.where(first_half, pltpu.roll(x, ROPE_LANES - HEAD_DIM // 4, 1),
                                pltpu.roll(x, HEAD_DIM // 4, 1))
            return x * cos + partner * sin

        hm = h_ref[rs, :]
        for jj in range(ATTN_W // MXU_COLS):
            qq = _dot(hm, wqkv_ref[:, MXU_COLS * jj:MXU_COLS * (jj + 1)])
            for i in range(MXU_COLS // ROPE_LANES):
                c0 = MXU_COLS * jj + ROPE_LANES * i
                q_ref[rs, c0:c0 + ROPE_LANES] = (
                    rope(qq[:, ROPE_LANES * i:ROPE_LANES * (i + 1)]) * scale).astype(BF16)
        kv = _dot(hm, wqkv_ref[:, ATTN_W:QKV_W])
        k = kv[:, 0:KV_W]
        v = kv[:, KV_W:2 * KV_W]
        kf_ref[rs, :] = k
        vf_ref[rs, :] = v
        k_ref[rs, :] = rope(k).astype(BF16)
        vt_ref[:, rs] = v.T.astype(BF16)

        zero_top, zero_bot = _sub_edges(j, n_sub, is_ctx, seq_start, seq_end)
        hx = hx_ref[SUB_ROWS * j:SUB_ROWS * j + ext, :]

        def proj_conv(c0):
            w_ref = why0_ref if c0 < HY_BLOCK else why1_ref
            w = w_ref[:, c0 % HY_BLOCK:c0 % HY_BLOCK + MXU_COLS]
            return _dwconv3(_dot(hx, w), zero_top, zero_bot, cw_ref, cb_ref,
                            slice(c0, c0 + MXU_COLS))

        for jj in range(HY_W // MXU_COLS):
            c0 = MXU_COLS * jj
            x0_ref[rs, c0:c0 + MXU_COLS] = proj_conv(c0).astype(BF16)
            x1 = proj_conv(HY_W + c0)
            u = proj_conv(2 * HY_W + c0)
            vh_ref[rs, c0:c0 + MXU_COLS] = (x1 * u).astype(BF16)


def _in_proj(h, w_in, cos_tab, sin_tab, conv_w, conv_b, layer):
    t = ROW_TILE
    nct = N_CTX // t
    tps = DEC_SEQ // t
    tab = lambda m: (jnp.where(m < nct, 0, 1 + (m - nct) % tps), 0, 0)
    row = lambda m: (m, 0)
    bf = lambda w: jax.ShapeDtypeStruct((N_TOK, w), BF16)
    return pl.pallas_call(
        _in_proj_kernel,
        out_shape=(bf(ATTN_W), bf(KV_W),
                   jax.ShapeDtypeStruct((KV_W, N_TOK), BF16),
                   jax.ShapeDtypeStruct((N_TOK, KV_W), F32),
                   jax.ShapeDtypeStruct((N_TOK, KV_W), F32),
                   bf(HY_W), bf(HY_W)),
        grid=(N_TOK // t,),
        in_specs=[
            pl.BlockSpec((t, D_MODEL), row),
            *_halo_specs(t, D_MODEL),
            _weight_spec((D_MODEL, QKV_W), layer, (0, 0)),
            _weight_spec((D_MODEL, HY_BLOCK), layer, (0, 1)),
            _weight_spec((D_MODEL, HY_BLOCK), layer, (0, 2)),
            pl.BlockSpec((None, t, ROPE_LANES), tab),
            pl.BlockSpec((None, t, ROPE_LANES), tab),
            _layer_spec((3, 3 * HY_W), layer),
            _layer_spec((1, 3 * HY_W), layer),
        ],
        out_specs=(pl.BlockSpec((t, ATTN_W), row),
                   pl.BlockSpec((t, KV_W), row),
                   pl.BlockSpec((KV_W, t), lambda m: (0, m)),
                   pl.BlockSpec((t, KV_W), row),
                   pl.BlockSpec((t, KV_W), row),
                   pl.BlockSpec((t, HY_W), row),
                   pl.BlockSpec((t, HY_W), row)),
        scratch_shapes=[pltpu.VMEM((t + 2 * HALO, D_MODEL), BF16)],
        compiler_params=_params("parallel"),
        name="in_proj",
    )(h, h, h, w_in, w_in, w_in, cos_tab, sin_tab, conv_w, conv_b)


def _rope_tables():
    rows = DEC_SEQ // GRID_W
    r, col = jnp.meshgrid(jnp.arange(rows), jnp.arange(GRID_W), indexing='ij')
    pos = jnp.stack([r.reshape(-1), col.reshape(-1)], axis=-1).astype(F32)
    half = HEAD_DIM // 2
    inv_freq = 1.0 / (ROPE_BASE ** (jnp.arange(0, half, 2, dtype=F32) / half))
    ang = pos[:, :, None] * inv_freq
    ang = jnp.stack([ang, ang], axis=-2).reshape(DEC_SEQ, HEAD_DIM)
    cos = jnp.tile(jnp.cos(ang), (1, ROPE_LANES // HEAD_DIM))
    sin = jnp.tile(jnp.sin(ang), (1, ROPE_LANES // HEAD_DIM))
    lane = jnp.arange(ROPE_LANES)
    sin = jnp.where((lane & (HEAD_DIM // 4)) == 0, -sin, sin)
    n_slab = DEC_SEQ // ROW_TILE
    cos = jnp.concatenate([jnp.ones((1, ROW_TILE, ROPE_LANES), F32),
                           cos.reshape(n_slab, ROW_TILE, ROPE_LANES)], axis=0)
    sin = jnp.concatenate([jnp.zeros((1, ROW_TILE, ROPE_LANES), F32),
                           sin.reshape(n_slab, ROW_TILE, ROPE_LANES)], axis=0)
    return cos, sin


def _filter_kernel(feat_ref, t_ref, dec_ref, w1_ref, b1_ref, f1_ref, w2_ref, b2_ref, f2_ref,
                   w3_ref, o_ref):
    a = _dot_split(feat_ref[...], w1_ref[...]) + b1_ref[...]
    a = jnp.sin(f1_ref[...] * a)
    a = _dot_split(a, w2_ref[...]) + b2_ref[...]
    a = jnp.sin(f2_ref[...] * a)
    hh = _dot_split(a, w3_ref[...])
    window = jnp.exp(-t_ref[...] * dec_ref[...])
    h_fwd = hh[:, 0:HY_W] * window
    h_bwd = hh[:, HY_W:2 * HY_W] * window
    row = lax.broadcasted_iota(jnp.int32, h_bwd.shape, 0)
    h_bwd = jnp.where(row == 0, 0.0, h_bwd)
    o_ref[:, 0:HY_W] = (h_fwd + h_bwd).astype(BF16)
    o_ref[:, HY_W:2 * HY_W] = (h_fwd - h_bwd).astype(BF16)


def _filters(seq, w1, b1, f1, w2, b2, f2, w3):
    t = jnp.linspace(0.0, 1.0, seq, dtype=F32)[:, None]
    w = 2.0 * math.pi * jnp.arange(seq, dtype=F32) / seq
    f = jnp.linspace(1e-4, FILT_BANDS - 1, FILT_BANDS, dtype=F32)
    zr = w[:, None] * f[None, :]
    feats = jnp.concatenate([t, jnp.cos(zr), -jnp.sin(zr),
                             jnp.zeros((seq, FILT_EMB_PAD - FILT_EMB), F32)], axis=-1)
    target = 1e-2
    decay = jnp.abs(jnp.linspace(math.log(target) / 1.5, math.log(target) / 0.3, HY_W,
                                 dtype=F32))[None, :]
    w1p = jnp.pad(w1, ((0, 0), (0, FILT_EMB_PAD - FILT_EMB), (0, 0)))
    vec = lambda a: a.reshape(DEPTH, 1, FILT_HIDDEN)
    return pl.pallas_call(
        _filter_kernel,
        out_shape=jax.ShapeDtypeStruct((DEPTH, seq, 2 * HY_W), BF16),
        grid=(DEPTH,),
        in_specs=[
            pl.BlockSpec((seq, FILT_EMB_PAD), lambda l: (0, 0)),
            pl.BlockSpec((seq, 1), lambda l: (0, 0)),
            pl.BlockSpec((1, HY_W), lambda l: (0, 0)),
            pl.BlockSpec((None, FILT_EMB_PAD, FILT_HIDDEN), lambda l: (l, 0, 0)),
            pl.BlockSpec((None, 1, FILT_HIDDEN), lambda l: (l, 0, 0)),
            pl.BlockSpec((None, 1, FILT_HIDDEN), lambda l: (l, 0, 0)),
            pl.BlockSpec((None, FILT_HIDDEN, FILT_HIDDEN), lambda l: (l, 0, 0)),
            pl.BlockSpec((None, 1, FILT_HIDDEN), lambda l: (l, 0, 0)),
            pl.BlockSpec((None, 1, FILT_HIDDEN), lambda l: (l, 0, 0)),
            pl.BlockSpec((None, FILT_HIDDEN, 2 * HY_W), lambda l: (l, 0, 0)),
        ],
        out_specs=pl.BlockSpec((None, seq, 2 * HY_W), lambda l: (l, 0, 0)),
        compiler_params=_params("parallel"),
        name=f"hyena_filter_{seq}",
    )(feats, t, decay, w1p, vec(b1), vec(f1), w2, vec(b2), vec(f2), w3)


TABLE_ROWS = 16


def _phase(a, b, period):
    k = (a * b) & (period - 1)
    ang = k.astype(F32) * (2.0 * math.pi / period)
    return jnp.cos(ang), jnp.sin(ang)


def _table_kernel(o_ref, *, half, row_mult, row_add, col_mult, col_add, stack_rows, scale):
    period = 8 * half
    groups = half // TABLE_ROWS
    b0 = col_mult * lax.broadcasted_iota(jnp.int32, (TABLE_ROWS, half), 1) + col_add
    a0 = row_mult * lax.broadcasted_iota(jnp.int32, (TABLE_ROWS, half), 0) + row_add
    cos_a, sin_a = _phase(a0, b0, period)
    b1 = col_mult * lax.broadcasted_iota(jnp.int32, (groups, half), 1) + col_add
    a1 = row_mult * TABLE_ROWS * lax.broadcasted_iota(jnp.int32, (groups, half), 0)
    cos_b, sin_b = _phase(a1, b1, period)
    cos = cos_a[None] * cos_b[:, None, :] - sin_a[None] * sin_b[:, None, :]
    msin = -(sin_a[None] * cos_b[:, None, :] + cos_a[None] * sin_b[:, None, :])
    cos = (cos.reshape(half, half) * scale).astype(BF16)
    msin = (msin.reshape(half, half) * scale).astype(BF16)
    if stack_rows:
        o_ref[0:half, :] = cos
        o_ref[half:2 * half, :] = msin
    else:
        o_ref[:, 0:half] = cos
        o_ref[:, half:2 * half] = msin


def _dft_tables(seq):
    half = seq // 2

    def table(stack_rows, **kw):
        shape = (2 * half, half) if stack_rows else (half, 2 * half)
        return pl.pallas_call(
            functools.partial(_table_kernel, half=half, stack_rows=stack_rows, **kw),
            out_shape=jax.ShapeDtypeStruct(shape, BF16),
            compiler_params=_params(),
            name=f"dft_table_{seq}",
        )()

    freq = dict(row_mult=2, row_add=1)
    freq_cols = dict(col_mult=2, col_add=1)
    return (table(True, **freq, col_mult=2, col_add=0, scale=1.0),
            table(True, **freq, col_mult=2, col_add=1, scale=1.0),
            table(False, row_mult=2, row_add=0, **freq_cols, scale=1.0 / seq),
            table(False, row_mult=2, row_add=1, **freq_cols, scale=1.0 / seq))


LANES = 128


def _split_rows(x, nat_ref):
    n = x.shape[0] // 2
    k = x.shape[1] // LANES
    for i in range(k):
        nat_ref[i] = x[:, LANES * i:LANES * (i + 1)]
    pick = lambda start: jnp.concatenate(
        [nat_ref[i, pl.ds(start, n, stride=2), :] for i in range(k)], axis=1)
    return pick(0), pick(1)


def _interleave_rows(even, odd, nat_ref):
    n = even.shape[0]
    k = even.shape[1] // LANES
    for i in range(k):
        nat_ref[i, pl.ds(0, n, stride=2), :] = even[:, LANES * i:LANES * (i + 1)]
        nat_ref[i, pl.ds(1, n, stride=2), :] = odd[:, LANES * i:LANES * (i + 1)]
    return jnp.concatenate([nat_ref[i] for i in range(k)], axis=1)


def _spectrum_kernel(fe_ref, fo_ref, h_ref, g_ref, nat_ref, *, half):
    even, odd = _split_rows(h_ref[...].astype(F32), nat_ref)
    even, odd = even.astype(BF16), odd.astype(BF16)
    w = HY_W
    cos_even = _dot(fe_ref[0:half, :], even[:, 0:w])
    sin_even = _dot(fe_ref[half:2 * half, :], even[:, w:2 * w])
    cos_odd = _dot(fo_ref[0:half, :], odd[:, 0:w])
    sin_odd = _dot(fo_ref[half:2 * half, :], odd[:, w:2 * w])
    g_ref[0] = cos_even + cos_odd
    g_ref[1] = sin_even + sin_odd
    g_ref[2] = cos_even - cos_odd
    g_ref[3] = sin_odd - sin_even


def _filter_spectrum(seq, fwd_even, fwd_odd, hs):
    half = seq // 2
    table = pl.BlockSpec((2 * half, half), lambda l: (0, 0))
    return pl.pallas_call(
        functools.partial(_spectrum_kernel, half=half),
        out_shape=jax.ShapeDtypeStruct((DEPTH, 4, half, HY_W), F32),
        grid=(DEPTH,),
        in_specs=[table, table, pl.BlockSpec((None, seq, 2 * HY_W), lambda l: (l, 0, 0))],
        out_specs=pl.BlockSpec((None, 4, half, HY_W), lambda l: (l, 0, 0, 0)),
        scratch_shapes=[pltpu.VMEM((2 * HY_W // LANES, seq, LANES), F32)],
        compiler_params=_params("parallel"),
        name=f"hyena_spectrum_{seq}",
    )(fwd_even, fwd_odd, hs)


LONGCONV_SEQS_PER_STEP = {SEQ: 4, DEC_SEQ: 1}
CHAINS_IN_FLIGHT = 2


def _longconv_kernel(v_ref, fe_ref, fo_ref, ie_ref, io_ref, g_ref, bias_ref, o_ref,
                     vin_ref, out_ref, *, half, n_seq):
    seq = 2 * half

    def forward(s, c, slot):
        v = v_ref[seq * s:seq * (s + 1), MXU_COLS * c:MXU_COLS * (c + 1)].astype(F32)
        even, odd = _split_rows(v, vin_ref.at[slot])
        return _dot(fe_ref[...], even.astype(BF16)), _dot(fo_ref[...], odd.astype(BF16))

    def finish(s, c, slot, p, q):
        rows = slice(seq * s, seq * (s + 1))
        cs = slice(MXU_COLS * c, MXU_COLS * (c + 1))
        pr, pi = p[0:half], p[half:2 * half]
        qr, qi = q[0:half], q[half:2 * half]
        ur, ui = pr + qr, pi + qi
        wr, wi = pr - qr, qi - pi
        yr = ur * g_ref[0, :, cs] - ui * g_ref[1, :, cs]
        yi = ur * g_ref[1, :, cs] + ui * g_ref[0, :, cs]
        zr = wr * g_ref[2, :, cs] - wi * g_ref[3, :, cs]
        zi = wr * g_ref[3, :, cs] + wi * g_ref[2, :, cs]
        spec_even = jnp.concatenate([yr + zr, yi - zi], axis=0).astype(BF16)
        spec_odd = jnp.concatenate([yr - zr, yi + zi], axis=0).astype(BF16)
        y = _interleave_rows(_dot(ie_ref[...], spec_even), _dot(io_ref[...], spec_odd),
                             out_ref.at[slot])
        v = jnp.concatenate([vin_ref[slot, i] for i in range(MXU_COLS // LANES)], axis=1)
        o_ref[rows, cs] = (y + bias_ref[:, cs] * v).astype(BF16)

    chains = [(s, c) for s in range(n_seq) for c in range(HY_W // MXU_COLS)]
    _run_chains([(s, c, i % CHAINS_IN_FLIGHT) for i, (s, c) in enumerate(chains)], forward, finish)


def _long_conv(seq, n_rows, row0, vh, tables, spectrum, hy_bias, layer):
    half = seq // 2
    n_seq = LONGCONV_SEQS_PER_STEP[seq]
    t = n_seq * seq
    fwd = pl.BlockSpec((2 * half, half), lambda b: (0, 0), **RESIDENT)
    inv = pl.BlockSpec((half, 2 * half), lambda b: (0, 0), **RESIDENT)
    scratch = pltpu.VMEM((CHAINS_IN_FLIGHT, MXU_COLS // LANES, seq, LANES), F32)
    return pl.pallas_call(
        functools.partial(_longconv_kernel, half=half, n_seq=n_seq),
        out_shape=jax.ShapeDtypeStruct((n_rows, HY_W), BF16),
        grid=(n_rows // t,),
        in_specs=[
            pl.BlockSpec((t, HY_W), lambda b: (row0 // t + b, 0)),
            fwd, fwd, inv, inv,
            pl.BlockSpec((None, 4, half, HY_W), lambda b: (layer, 0, 0, 0), **RESIDENT),
            pl.BlockSpec((None, 1, HY_W), lambda b: (layer, 0, 0)),
        ],
        out_specs=pl.BlockSpec((t, HY_W), lambda b: (b, 0)),
        scratch_shapes=[scratch, scratch],
        compiler_params=_params("parallel"),
        name=f"hyena_longconv_{seq}",
    )(vh, *tables, spectrum, hy_bias.reshape(DEPTH, 1, HY_W))


ONES_ROWS = 16
CTX_SEQS_PER_STEP = 4
QBLOCKS_PER_STEP = 8


def _head_rows(q, j):
    return jnp.concatenate([q[:, HEAD_DIM * hd:HEAD_DIM * (hd + 1)]
                            for hd in range(GROUP * j, GROUP * (j + 1))], axis=0)


def _lane_sink(sink_ref, layer, j, block):
    head = lax.broadcasted_iota(jnp.int32, (1, GROUP * block), 1) // block
    sink = jnp.zeros((1, GROUP * block), F32)
    for g in range(GROUP):
        sink = jnp.where(head == g, sink_ref[layer, GROUP * j + g] * LOG2_E, sink)
    return sink


def _with_ones(vt):
    return jnp.concatenate([vt, jnp.ones((ONES_ROWS, vt.shape[1]), BF16)], axis=0)


def _softmax_pv_t(parts, sink):
    m = sink
    for lg, _ in parts:
        m = jnp.maximum(m, jnp.max(lg, axis=0, keepdims=True))
    out = None
    for lg, vt in parts:
        pv = _dot(vt, jnp.exp2(lg - m).astype(BF16))
        out = pv if out is None else out + pv
    den = out[HEAD_DIM:HEAD_DIM + 1, :] + jnp.exp2(sink - m)
    return out[0:HEAD_DIM, :] / den


def _store_heads(o_ref, rows, j, block, o_t):
    for g in range(GROUP):
        hd = GROUP * j + g
        o_ref[rows, HEAD_DIM * hd:HEAD_DIM * (hd + 1)] = (
            o_t[:, block * g:block * (g + 1)].T.astype(BF16))


def _ctx_attn_kernel(sink_ref, q_ref, k_ref, vt_ref, o_ref, *, layer):
    def logits(n, j):
        rows = slice(SEQ * n, SEQ * (n + 1))
        hs = slice(HEAD_DIM * j, HEAD_DIM * (j + 1))
        return _dot_nt(k_ref[rows, hs], _head_rows(q_ref[rows, :], j)), _with_ones(vt_ref[hs, rows])

    def finish(n, j, s, vt):
        o_t = _softmax_pv_t([(s, vt)], _lane_sink(sink_ref, layer, j, SEQ))
        _store_heads(o_ref, slice(SEQ * n, SEQ * (n + 1)), j, SEQ, o_t)

    _run_chains([(n, j) for n in range(CTX_SEQS_PER_STEP) for j in range(N_KV_HEADS)],
                logits, finish)


def _ctx_attention(sink, q, k, vt, layer):
    t = CTX_SEQS_PER_STEP * SEQ
    row = lambda b: (b, 0)
    return pl.pallas_call(
        functools.partial(_ctx_attn_kernel, layer=layer),
        out_shape=jax.ShapeDtypeStruct((N_CTX, ATTN_W), BF16),
        grid=(N_CTX // t,),
        in_specs=[
            pl.BlockSpec(memory_space=pltpu.SMEM),
            pl.BlockSpec((t, ATTN_W), row),
            pl.BlockSpec((t, KV_W), row),
            pl.BlockSpec((KV_W, t), lambda b: (0, b)),
        ],
        out_specs=pl.BlockSpec((t, ATTN_W), row),
        compiler_params=_params("parallel"),
        name="context_attention",
    )(sink, q, k, vt)


def _band_bias():
    kpos = jnp.arange(BAND)[None, :, None] - jnp.arange(3)[:, None, None] * WINDOW
    qpos = (jnp.arange(GROUP * QBLOCK) & (QBLOCK - 1))[None, None, :]
    return jnp.where(jnp.abs(qpos - kpos) <= WINDOW, 0.0, MASK_VALUE).astype(F32)


def _lat_attn_kernel(sink_ref, q_ref, k_ref, vt_ref, ck_ref, cv_ref, bias_ref, o_ref,
                     kc_ref, vct_ref, *, layer):
    i = pl.program_id(1)

    @pl.when(i == 0)
    def _():
        kc_ref[...] = ck_ref[...].astype(BF16)
        vct = cv_ref[...].T.astype(BF16)
        for j in range(N_KV_HEADS):
            vct_ref[j] = _with_ones(vct[HEAD_DIM * j:HEAD_DIM * (j + 1), :])

    def logits(n, j):
        start = (i * QBLOCKS_PER_STEP + n) * QBLOCK
        ws = pl.multiple_of(jnp.clip(start - WINDOW, 0, DEC_SEQ - BAND), QBLOCK)
        placement = jnp.where(start < WINDOW, 0, jnp.where(start + QBLOCK + WINDOW > DEC_SEQ, 2, 1))
        q4 = _head_rows(q_ref[QBLOCK * n:QBLOCK * (n + 1), :], j)
        hs = slice(HEAD_DIM * j, HEAD_DIM * (j + 1))
        s_ctx = _dot_nt(kc_ref[:, hs], q4)
        s_band = _dot_nt(k_ref[pl.ds(ws, BAND), hs], q4) + bias_ref[placement]
        return s_ctx, s_band, _with_ones(vt_ref[hs, pl.ds(ws, BAND)])

    def finish(n, j, s_ctx, s_band, vbt):
        o_t = _softmax_pv_t([(s_ctx, vct_ref[j]), (s_band, vbt)],
                            _lane_sink(sink_ref, layer, j, QBLOCK))
        _store_heads(o_ref, slice(QBLOCK * n, QBLOCK * (n + 1)), j, QBLOCK, o_t)

    _run_chains([(n, j) for n in range(QBLOCKS_PER_STEP) for j in range(N_KV_HEADS)],
                logits, finish)


def _lat_attention(sink, q, k, vt, cache_k, cache_v, band_bias, layer):
    t = QBLOCKS_PER_STEP * QBLOCK
    steps = DEC_SEQ // t
    seq_block = N_CTX // DEC_SEQ
    return pl.pallas_call(
        functools.partial(_lat_attn_kernel, layer=layer),
        out_shape=jax.ShapeDtypeStruct((N_LAT, ATTN_W), BF16),
        grid=(DEC_BATCH, steps),
        in_specs=[
            pl.BlockSpec(memory_space=pltpu.SMEM),
            pl.BlockSpec((t, ATTN_W), lambda b, i: (N_CTX // t + b * steps + i, 0)),
            pl.BlockSpec((DEC_SEQ, KV_W), lambda b, i: (seq_block + b, 0)),
            pl.BlockSpec((KV_W, DEC_SEQ), lambda b, i: (0, seq_block + b)),
            pl.BlockSpec((None, None, PAST_LEN, KV_W), lambda b, i: (b, layer, 0, 0)),
            pl.BlockSpec((None, None, PAST_LEN, KV_W), lambda b, i: (b, layer, 0, 0)),
            pl.BlockSpec((3, BAND, GROUP * QBLOCK), lambda b, i: (0, 0, 0)),
        ],
        out_specs=pl.BlockSpec((t, ATTN_W), lambda b, i: (b * steps + i, 0)),
        scratch_shapes=[pltpu.VMEM((PAST_LEN, KV_W), BF16),
                        pltpu.VMEM((N_KV_HEADS, HEAD_DIM + ONES_ROWS, PAST_LEN), BF16)],
        compiler_params=_params("parallel", "arbitrary"),
        name="latent_attention",
    )(sink, q, k, vt, cache_k, cache_v, band_bias)


N_GATE_CHUNKS = D_MODEL // MXU_COLS


def _merge_kernel(h_ref, atc_ref, atl_ref, ybc_ref, ybl_ref, x0_ref, x_ref, mod_ref, *rest):
    wga_refs = rest[0:N_GATE_CHUNKS]
    wgb_refs = rest[N_GATE_CHUNKS:2 * N_GATE_CHUNKS]
    wpa_ref, wpb_ref, wo_ref, g_ref, b_ref, xo_ref, ho_ref, mg_ref = rest[2 * N_GATE_CHUNKS:]
    is_ctx = pl.program_id(0) < N_CTX // ROW_TILE
    g1 = mod_ref[:, 2 * D_MODEL:3 * D_MODEL]
    sh2 = mod_ref[:, 3 * D_MODEL:4 * D_MODEL]
    sc2 = mod_ref[:, 4 * D_MODEL:5 * D_MODEL]
    for s in range(ROW_TILE // SUB_ROWS):
        rs = slice(SUB_ROWS * s, SUB_ROWS * (s + 1))
        h = h_ref[rs, :]
        attn = jnp.where(is_ctx, atc_ref[rs, :], atl_ref[rs, :])
        conv = jnp.where(is_ctx, ybc_ref[rs, :], ybl_ref[rs, :])
        hy = (x0_ref[rs, :].astype(F32) * conv.astype(F32)).astype(BF16)
        for j in range(N_GATE_CHUNKS):
            cs = slice(MXU_COLS * j, MXU_COLS * (j + 1))
            ga = jax.nn.sigmoid(_dot(h, wga_refs[j][...]))
            gb = jax.nn.sigmoid(_dot(h, wgb_refs[j][...]))
            mg_ref[rs, cs] = (ga * _dot(attn, wpa_ref[:, cs])
                              + gb * _dot(hy, wpb_ref[:, cs])).astype(BF16)
        sub = _dot(mg_ref[rs, :], wo_ref[...])
        x = _layer_norm(DEEPNORM_ALPHA * x_ref[rs, :] + g1 * sub, g_ref[...], b_ref[...])
        xo_ref[rs, :] = x
        ho_ref[rs, :] = (x * (1.0 + sc2) + sh2).astype(BF16)


def _merge(h, atc, atl, ybc, ybl, x0, x, mods, w_in, wpa, wpb, wo, ln_g, ln_b, layer):
    t = ROW_TILE
    nct = N_CTX // t
    row = lambda m: (m, 0)
    ctx_row = lambda m: (jnp.minimum(m, nct - 1), 0)
    lat_row = lambda m: (jnp.maximum(m - nct, 0), 0)
    gate_block0 = GATE_COL0 // MXU_COLS
    gate_specs = [_weight_spec((D_MODEL, MXU_COLS), layer, (0, gate_block0 + j))
                  for j in range(2 * N_GATE_CHUNKS)]
    return pl.pallas_call(
        _merge_kernel,
        out_shape=(jax.ShapeDtypeStruct((N_TOK, D_MODEL), F32),
                   jax.ShapeDtypeStruct((N_TOK, D_MODEL), BF16)),
        grid=(N_TOK // t,),
        in_specs=[
            pl.BlockSpec((t, D_MODEL), row),
            pl.BlockSpec((t, ATTN_W), ctx_row),
            pl.BlockSpec((t, ATTN_W), lat_row),
            pl.BlockSpec((t, HY_W), ctx_row),
            pl.BlockSpec((t, HY_W), lat_row),
            pl.BlockSpec((t, HY_W), row),
            pl.BlockSpec((t, D_MODEL), row),
            _mod_spec(layer, t),
            *gate_specs,
            _weight_spec((ATTN_W, D_MODEL), layer),
            _weight_spec((HY_W, D_MODEL), layer),
            _weight_spec((D_MODEL, D_MODEL), layer),
            _layer_spec((1, D_MODEL), layer),
            _layer_spec((1, D_MODEL), layer),
        ],
        out_specs=(pl.BlockSpec((t, D_MODEL), row), pl.BlockSpec((t, D_MODEL), row)),
        scratch_shapes=[pltpu.VMEM((t, D_MODEL), BF16)],
        compiler_params=_params("parallel"),
        name="merge_ln1",
    )(h, atc, atl, ybc, ybl, x0, x, mods, *([w_in] * (2 * N_GATE_CHUNKS)), wpa, wpb, wo,
      ln_g, ln_b)


def _ffn_kernel(h_ref, hp_ref, hn_ref, x_ref, mod_ref, nmod_ref, wg_ref, wv_ref, wd_ref,
                cw_ref, cb_ref, g_ref, b_ref, *rest, last_layer):
    if last_layer:
        yp_ref, ys_ref, hx_ref, a_ref, xo_ref = rest
    else:
        xo_ref, ho_ref, hx_ref, a_ref = rest
    n_sub = ROW_TILE // SUB_ROWS
    ext = SUB_ROWS + 2 * HALO
    is_ctx, seq_start, seq_end = _tile_edges(pl.program_id(0), ROW_TILE)
    _fill_extended(hx_ref, hp_ref, h_ref, hn_ref)
    g2 = mod_ref[:, 5 * D_MODEL:6 * D_MODEL]
    sh = nmod_ref[:, 0:D_MODEL]
    sc = nmod_ref[:, D_MODEL:2 * D_MODEL]
    for j in range(n_sub):
        rs = slice(SUB_ROWS * j, SUB_ROWS * (j + 1))
        zero_top, zero_bot = _sub_edges(j, n_sub, is_ctx, seq_start, seq_end)
        hx = hx_ref[SUB_ROWS * j:SUB_ROWS * j + ext, :]
        hm = h_ref[rs, :]
        for c in range(D_FF // MXU_COLS):
            cs = slice(MXU_COLS * c, MXU_COLS * (c + 1))
            gate = _dwconv3(_dot(hx, wg_ref[:, cs]), zero_top, zero_bot, cw_ref, cb_ref, cs)
            gelu = 0.5 * gate * (1.0 + lax.erf(gate * math.sqrt(0.5)))
            a_ref[rs, cs] = (gelu * _dot(hm, wv_ref[:, cs])).astype(BF16)
        sub = _dot(a_ref[rs, :], wd_ref[...])
        x = _layer_norm(DEEPNORM_ALPHA * x_ref[rs, :] + g2 * sub, g_ref[...], b_ref[...])
        xo_ref[rs, :] = x
        if not last_layer:
            ho_ref[rs, :] = (x * (1.0 + sc) + sh).astype(BF16)

    if last_layer:
        @pl.when(is_ctx)
        def _():
            yp_ref[...] = xo_ref[...]

        @pl.when(jnp.logical_not(is_ctx))
        def _():
            ys_ref[...] = xo_ref[...]


def _ffn(h, x, mods, w_up, w_down, conv_w, conv_b, ln_g, ln_b, layer):
    t = ROW_TILE
    nct = N_CTX // t
    row = lambda m: (m, 0)
    last_layer = layer == DEPTH - 1
    if last_layer:
        out_shape = (jax.ShapeDtypeStruct((N_CTX, D_MODEL), F32),
                     jax.ShapeDtypeStruct((N_LAT, D_MODEL), F32))
        out_specs = (pl.BlockSpec((t, D_MODEL), lambda m: (jnp.minimum(m, nct - 1), 0)),
                     pl.BlockSpec((t, D_MODEL), lambda m: (jnp.maximum(m - nct, 0), 0)))
        scratch = [pltpu.VMEM((t, D_MODEL), F32)]
    else:
        out_shape = (jax.ShapeDtypeStruct((N_TOK, D_MODEL), F32),
                     jax.ShapeDtypeStruct((N_TOK, D_MODEL), BF16))
        out_specs = (pl.BlockSpec((t, D_MODEL), row), pl.BlockSpec((t, D_MODEL), row))
        scratch = []
    return pl.pallas_call(
        functools.partial(_ffn_kernel, last_layer=last_layer),
        out_shape=out_shape,
        grid=(N_TOK // t,),
        in_specs=[
            pl.BlockSpec((t, D_MODEL), row),
            *_halo_specs(t, D_MODEL),
            pl.BlockSpec((t, D_MODEL), row),
            _mod_spec(layer, t),
            _mod_spec(min(layer + 1, DEPTH - 1), t),
            _weight_spec((D_MODEL, D_FF), layer, (0, 0)),
            _weight_spec((D_MODEL, D_FF), layer, (0, 1)),
            _weight_spec((D_FF, D_MODEL), layer),
            _layer_spec((3, D_FF), layer),
            _layer_spec((1, D_FF), layer),
            _layer_spec((1, D_MODEL), layer),
            _layer_spec((1, D_MODEL), layer),
        ],
        out_specs=out_specs,
        scratch_shapes=[pltpu.VMEM((t + 2 * HALO, D_MODEL), BF16), pltpu.VMEM((t, D_FF), BF16),
                        *scratch],
        compiler_params=_params("arbitrary"),
        name="conv_ffn_ln2",
    )(h, h, h, x, mods, mods, w_up, w_up, w_down, conv_w, conv_b, ln_g, ln_b)


def kernel(x_prompt, x_sample, cache_k, cache_v, c, c_ctx, w_ada, b_ada, w_in, attn_sink,
           hy_conv_w, hy_conv_b, filt_w1, filt_b1, filt_freq1, filt_w2, filt_b2, filt_freq2,
           filt_w3, hy_bias, w_pa, w_pb, w_out, ln1_g, ln1_b, w_up, ffn_conv_w, ffn_conv_b,
           w_down, ln2_g, ln2_b):
    cond = jnp.concatenate([c_ctx[None, :], c,
                            jnp.zeros((N_MOD_ROWS - 1 - DEC_BATCH, D_MODEL), F32)], axis=0)
    mods = _modulation(cond, w_ada, b_ada).reshape(DEPTH, N_MOD_ROWS, 1, 6 * D_MODEL)

    x, h = _prologue(x_prompt.reshape(N_CTX, D_MODEL), x_sample.reshape(N_LAT, D_MODEL), mods)

    cos_tab, sin_tab = _rope_tables()
    conv_args = {}
    for seq in (SEQ, DEC_SEQ):
        tables = _dft_tables(seq)
        hs = _filters(seq, filt_w1, filt_b1, filt_freq1, filt_w2, filt_b2, filt_freq2, filt_w3)
        conv_args[seq] = (tables, _filter_spectrum(seq, tables[0], tables[1], hs))

    cache_k = cache_k.reshape(DEC_BATCH, DEPTH, PAST_LEN, KV_W)
    cache_v = cache_v.reshape(DEC_BATCH, DEPTH, PAST_LEN, KV_W)
    hy_conv_b3 = hy_conv_b.reshape(DEPTH, 1, 3 * HY_W)
    ffn_conv_b3 = ffn_conv_b.reshape(DEPTH, 1, D_FF)
    ln1_g3, ln1_b3 = ln1_g.reshape(DEPTH, 1, D_MODEL), ln1_b.reshape(DEPTH, 1, D_MODEL)
    ln2_g3, ln2_b3 = ln2_g.reshape(DEPTH, 1, D_MODEL), ln2_b.reshape(DEPTH, 1, D_MODEL)

    band_bias = _band_bias()
    w_in, w_pa, w_pb, w_out, w_up, w_down = (
        w.astype(BF16) for w in (w_in, w_pa, w_pb, w_out, w_up, w_down))
    keys, values = [], []
    for l in range(DEPTH):
        q, k, vt, kf, vf, x0, vh = _in_proj(h, w_in, cos_tab, sin_tab, hy_conv_w, hy_conv_b3, l)
        keys.append(kf[:N_CTX])
        values.append(vf[:N_CTX])

        ybc = _long_conv(SEQ, N_CTX, 0, vh, *conv_args[SEQ], hy_bias, l)
        ybl = _long_conv(DEC_SEQ, N_LAT, N_CTX, vh, *conv_args[DEC_SEQ], hy_bias, l)
        atc = _ctx_attention(attn_sink, q, k, vt, l)
        atl = _lat_attention(attn_sink, q, k, vt, cache_k, cache_v, band_bias, l)

        x, h2 = _merge(h, atc, atl, ybc, ybl, x0, x, mods, w_in, w_pa, w_pb, w_out,
                       ln1_g3, ln1_b3, l)
        x, h = _ffn(h2, x, mods, w_up, w_down, ffn_conv_w, ffn_conv_b3, ln2_g3, ln2_b3, l)

    y_prompt_rows, y_sample_rows = x, h
    y_prompt = y_prompt_rows.reshape(BATCH, SEQ, D_MODEL)
    y_sample = y_sample_rows.reshape(DEC_BATCH, DEC_SEQ, D_MODEL)
    to_cache = lambda ts: jnp.stack(ts, axis=1).reshape(BATCH, DEPTH, SEQ, N_KV_HEADS, HEAD_DIM)
    new_k = to_cache([t.reshape(BATCH, SEQ, KV_W) for t in keys])
    new_v = to_cache([t.reshape(BATCH, SEQ, KV_W) for t in values])
    return (y_prompt, y_sample, new_k, new_v)
```

```python
import functools
import math

import jax
import jax.numpy as jnp
from jax import lax
from jax.experimental import pallas as pl
from jax.experimental.pallas import tpu as pltpu

D_MODEL = 1024
BATCH = 16
SEQ = 256
DEPTH = 4
DEC_BATCH = 4
DEC_SEQ = 2048
PAST_LEN = 512
GRID_W = 64
HEAD_DIM = 64
N_HEADS = 8
N_KV_HEADS = 2
GROUP = N_HEADS // N_KV_HEADS
ATTN_W = N_HEADS * HEAD_DIM
KV_W = N_KV_HEADS * HEAD_DIM
WINDOW = 128
QBLOCK = 128
HY_W = 512
FILT_EMB = 33
FILT_EMB_PAD = 40
FILT_BANDS = (FILT_EMB - 1) // 2
FILT_HIDDEN = 64
D_FF = 2816
ROPE_BASE = 10000.0
LN_EPS = 1e-5
DEEPNORM_ALPHA = (2 * DEPTH) ** 0.25

N_CTX = BATCH * SEQ
N_LAT = DEC_BATCH * DEC_SEQ
N_TOK = N_CTX + N_LAT
N_MOD_ROWS = 8
QKV_W = ATTN_W + 2 * KV_W
HY_BLOCK = QKV_W
assert 3 * HY_W == 2 * HY_BLOCK
GATE_COL0 = QKV_W + 3 * HY_W
BAND = QBLOCK + 2 * WINDOW
MASK_VALUE = -1e30
LOG2_E = math.log2(math.e)

F32 = jnp.float32
BF16 = jnp.bfloat16

VMEM_LIMIT_BYTES = 56 * 1024 * 1024
MXU_COLS = 256
ROW_TILE = 1024
SUB_ROWS = SEQ
HALO = 16
RESIDENT = dict(pipeline_mode=pl.Buffered(1))
CAST_BLOCKS = 8


def _params(*semantics):
    return pltpu.CompilerParams(dimension_semantics=semantics, vmem_limit_bytes=VMEM_LIMIT_BYTES)


def _dot(a, b):
    return jnp.dot(a, b, preferred_element_type=F32)


def _dot_split(a, b):
    a_hi = a.astype(BF16)
    b_hi = b.astype(BF16)
    a_lo = (a - a_hi.astype(F32)).astype(BF16)
    b_lo = (b - b_hi.astype(F32)).astype(BF16)
    return _dot(a_hi, b_hi) + (_dot(a_hi, b_lo) + _dot(a_lo, b_hi))


def _dot_nt(a, b):
    return lax.dot_general(a, b, (((1,), (1,)), ((), ())), preferred_element_type=F32)


def _mod_row(tile_rows):
    n_ctx_tiles = N_CTX // tile_rows
    tiles_per_seq = DEC_SEQ // tile_rows

    def fn(m):
        return jnp.where(m < n_ctx_tiles, 0, 1 + (m - n_ctx_tiles) // tiles_per_seq)

    return fn


def _mod_spec(layer, tile_rows):
    grp = _mod_row(tile_rows)
    return pl.BlockSpec((None, None, 1, 6 * D_MODEL), lambda m, *_: (layer, grp(m), 0, 0))


def _layer_spec(shape, layer):
    zeros = (0,) * len(shape)
    return pl.BlockSpec((None,) + tuple(shape), lambda *_: (layer,) + zeros)


def _weight_spec(shape, block_index=None):
    block_index = block_index or (0,) * len(shape)
    return pl.BlockSpec((None,) + tuple(shape), lambda *_: (0,) + tuple(block_index), **RESIDENT)


def _layer_norm(y, g, b):
    mu = jnp.mean(y, axis=-1, keepdims=True)
    yc = y - mu
    var = jnp.mean(yc * yc, axis=-1, keepdims=True)
    return yc * lax.rsqrt(var + LN_EPS) * g + b


def _run_chains(chains, first_fn, finish_fn):
    pending = first_fn(*chains[0])
    for c in range(1, len(chains)):
        nxt = first_fn(*chains[c])
        finish_fn(*chains[c - 1], *pending)
        pending = nxt
    finish_fn(*chains[-1], *pending)


def _tile_edges(m, tile_rows):
    n_ctx_tiles = N_CTX // tile_rows
    tiles_per_seq = DEC_SEQ // tile_rows
    is_ctx = m < n_ctx_tiles
    lat_pos = (m - n_ctx_tiles) % tiles_per_seq
    return (is_ctx, jnp.logical_or(is_ctx, lat_pos == 0),
            jnp.logical_or(is_ctx, lat_pos == tiles_per_seq - 1))


def _halo_specs(tile_rows, n_cols):
    tb = tile_rows // HALO
    last = N_TOK // HALO - 1
    return (pl.BlockSpec((HALO, n_cols), lambda m: (jnp.maximum(m * tb - 1, 0), 0)),
            pl.BlockSpec((HALO, n_cols), lambda m: (jnp.minimum((m + 1) * tb, last), 0)))


def _fill_extended(hx_ref, hp_ref, h_ref, hn_ref):
    rows = h_ref.shape[0]
    hx_ref[0:HALO, :] = hp_ref[...]
    hx_ref[HALO:HALO + rows, :] = h_ref[...]
    hx_ref[HALO + rows:2 * HALO + rows, :] = hn_ref[...]


def _sub_edges(j, n_sub, is_ctx, seq_start, seq_end):
    return (seq_start if j == 0 else is_ctx), (seq_end if j == n_sub - 1 else is_ctx)


def _dwconv3(z, zero_top, zero_bot, w_ref, b_ref, cs):
    n = z.shape[0] - 2 * HALO
    top = jnp.where(zero_top, 0.0, z[HALO - 8:HALO])
    bot = jnp.where(zero_bot, 0.0, z[HALO + n:HALO + n + 8])
    z = jnp.concatenate([z[0:HALO - 8], top, z[HALO:HALO + n], bot, z[HALO + n + 8:]], axis=0)
    prev = pltpu.roll(z, 1, 0)[HALO:HALO + n]
    nxt = pltpu.roll(z, z.shape[0] - 1, 0)[HALO:HALO + n]
    return (prev * w_ref[0:1, cs] + z[HALO:HALO + n] * w_ref[1:2, cs] + nxt * w_ref[2:3, cs]
            + b_ref[:, cs])


def _mod_kernel(cond_ref, w_ref, b_ref, o_ref):
    c = cond_ref[...]
    s = (c * jax.nn.sigmoid(c)).astype(BF16)
    o_ref[...] = _dot(s, w_ref[...].astype(BF16)) + b_ref[...]


def _modulation(cond, w_ada, b_ada):
    n_col = 6 * D_MODEL // D_MODEL
    return pl.pallas_call(
        _mod_kernel,
        out_shape=jax.ShapeDtypeStruct((DEPTH, N_MOD_ROWS, 6 * D_MODEL), F32),
        grid=(DEPTH, n_col),
        in_specs=[
            pl.BlockSpec((N_MOD_ROWS, D_MODEL), lambda l, j: (0, 0)),
            pl.BlockSpec((None, D_MODEL, D_MODEL), lambda l, j: (l, 0, j)),
            pl.BlockSpec((None, 1, D_MODEL), lambda l, j: (l, 0, j)),
        ],
        out_specs=pl.BlockSpec((None, N_MOD_ROWS, D_MODEL), lambda l, j: (l, 0, j)),
        compiler_params=_params("parallel", "parallel"),
        name="modulation",
    )(cond, w_ada, b_ada.reshape(DEPTH, 1, 6 * D_MODEL))


PRO_TILE = 1024


def _prologue_kernel(xp_ref, xs_ref, mod_ref, x_ref, h_ref):
    m = pl.program_id(0)

    def emit(x):
        x_ref[...] = x
        sh = mod_ref[:, 0:D_MODEL]
        sc = mod_ref[:, D_MODEL:2 * D_MODEL]
        h_ref[...] = (x * (1.0 + sc) + sh).astype(BF16)

    @pl.when(m < N_CTX // PRO_TILE)
    def _():
        emit(xp_ref[...])

    @pl.when(m >= N_CTX // PRO_TILE)
    def _():
        emit(xs_ref[...])


def _prologue(xp, xs, mods):
    nct = N_CTX // PRO_TILE
    return pl.pallas_call(
        _prologue_kernel,
        out_shape=(jax.ShapeDtypeStruct((N_TOK, D_MODEL), F32),
                   jax.ShapeDtypeStruct((N_TOK, D_MODEL), BF16)),
        grid=(N_TOK // PRO_TILE,),
        in_specs=[
            pl.BlockSpec((PRO_TILE, D_MODEL), lambda m: (jnp.minimum(m, nct - 1), 0)),
            pl.BlockSpec((PRO_TILE, D_MODEL), lambda m: (jnp.maximum(m - nct, 0), 0)),
            _mod_spec(0, PRO_TILE),
        ],
        out_specs=(pl.BlockSpec((PRO_TILE, D_MODEL), lambda m: (m, 0)),
                   pl.BlockSpec((PRO_TILE, D_MODEL), lambda m: (m, 0))),
        compiler_params=_params("parallel"),
        name="prologue",
    )(xp, xs, mods)


ROPE_LANES = 2 * HEAD_DIM


N_IN_PROJ_OUTS = 7


def _in_proj_kernel(h_ref, hp_ref, hn_ref, wqkv_ref, why0_ref, why1_ref, cos_ref, sin_ref,
                    cw_ref, cb_ref, *rest):
    n_cast = (len(rest) - N_IN_PROJ_OUTS - 1) // 2
    cast_src = rest[0:n_cast]
    q_ref, k_ref, vt_ref, kf_ref, vf_ref, x0_ref, vh_ref = rest[n_cast:n_cast + N_IN_PROJ_OUTS]
    cast_dst = rest[n_cast + N_IN_PROJ_OUTS:-1]
    hx_ref = rest[-1]

    @pl.when(pl.program_id(0) < CAST_BLOCKS)
    def _():
        for src, dst in zip(cast_src, cast_dst):
            dst[...] = src[...].astype(BF16)

    n_sub = ROW_TILE // SUB_ROWS
    ext = SUB_ROWS + 2 * HALO
    is_ctx, seq_start, seq_end = _tile_edges(pl.program_id(0), ROW_TILE)
    _fill_extended(hx_ref, hp_ref, h_ref, hn_ref)
    lane = lax.broadcasted_iota(jnp.int32, (SUB_ROWS, ROPE_LANES), 1)
    first_half = (lane & (HEAD_DIM // 4)) == 0
    scale = HEAD_DIM ** -0.5 * LOG2_E
    for j in range(n_sub):
        rs = slice(SUB_ROWS * j, SUB_ROWS * (j + 1))
        cos = cos_ref[rs, :]
        sin = sin_ref[rs, :]

        def rope(x):
            partner = jnp.where(first_half, pltpu.roll(x, ROPE_LANES - HEAD_DIM // 4, 1),
                                pltpu.roll(x, HEAD_DIM // 4, 1))
            return x * cos + partner * sin

        hm = h_ref[rs, :]
        for jj in range(ATTN_W // MXU_COLS):
            qq = _dot(hm, wqkv_ref[:, MXU_COLS * jj:MXU_COLS * (jj + 1)])
            for i in range(MXU_COLS // ROPE_LANES):
                c0 = MXU_COLS * jj + ROPE_LANES * i
                q_ref[rs, c0:c0 + ROPE_LANES] = (
                    rope(qq[:, ROPE_LANES * i:ROPE_LANES * (i + 1)]) * scale).astype(BF16)
        kv = _dot(hm, wqkv_ref[:, ATTN_W:QKV_W])
        k = kv[:, 0:KV_W]
        v = kv[:, KV_W:2 * KV_W]
        kf_ref[rs, :] = k
        vf_ref[rs, :] = v
        k_ref[rs, :] = rope(k).astype(BF16)
        vt_ref[:, rs] = v.T.astype(BF16)

        zero_top, zero_bot = _sub_edges(j, n_sub, is_ctx, seq_start, seq_end)
        hx = hx_ref[SUB_ROWS * j:SUB_ROWS * j + ext, :]

        def proj_conv(c0):
            w_ref = why0_ref if c0 < HY_BLOCK else why1_ref
            w = w_ref[:, c0 % HY_BLOCK:c0 % HY_BLOCK + MXU_COLS]
            return _dwconv3(_dot(hx, w), zero_top, zero_bot, cw_ref, cb_ref,
                            slice(c0, c0 + MXU_COLS))

        for jj in range(HY_W // MXU_COLS):
            c0 = MXU_COLS * jj
            x0_ref[rs, c0:c0 + MXU_COLS] = proj_conv(c0).astype(BF16)
            x1 = proj_conv(HY_W + c0)
            u = proj_conv(2 * HY_W + c0)
            vh_ref[rs, c0:c0 + MXU_COLS] = (x1 * u).astype(BF16)


def _in_proj(h, w_in, cos_tab, sin_tab, conv_w, conv_b, to_cast, layer):
    t = ROW_TILE
    nct = N_CTX // t
    tps = DEC_SEQ // t
    tab = lambda m: (jnp.where(m < nct, 0, 1 + (m - nct) % tps), 0, 0)
    row = lambda m: (m, 0)
    bf = lambda w: jax.ShapeDtypeStruct((N_TOK, w), BF16)
    cast_block = lambda m: jnp.minimum(m, CAST_BLOCKS - 1)
    cast_in, cast_out, cast_shapes = [], [], []
    for w, w_layer in to_cast:
        rows, cols = w.shape[1] // CAST_BLOCKS, w.shape[2]
        cast_in.append(pl.BlockSpec((None, rows, cols),
                                    lambda m, w_layer=w_layer: (w_layer, cast_block(m), 0)))
        cast_out.append(pl.BlockSpec((None, rows, cols), lambda m: (0, cast_block(m), 0)))
        cast_shapes.append(jax.ShapeDtypeStruct((1,) + w.shape[1:], BF16))
    return pl.pallas_call(
        _in_proj_kernel,
        out_shape=(bf(ATTN_W), bf(KV_W),
                   jax.ShapeDtypeStruct((KV_W, N_TOK), BF16),
                   jax.ShapeDtypeStruct((N_TOK, KV_W), F32),
                   jax.ShapeDtypeStruct((N_TOK, KV_W), F32),
                   bf(HY_W), bf(HY_W), *cast_shapes),
        grid=(N_TOK // t,),
        in_specs=[
            pl.BlockSpec((t, D_MODEL), row),
            *_halo_specs(t, D_MODEL),
            _weight_spec((D_MODEL, QKV_W), (0, 0)),
            _weight_spec((D_MODEL, HY_BLOCK), (0, 1)),
            _weight_spec((D_MODEL, HY_BLOCK), (0, 2)),
            pl.BlockSpec((None, t, ROPE_LANES), tab),
            pl.BlockSpec((None, t, ROPE_LANES), tab),
            _layer_spec((3, 3 * HY_W), layer),
            _layer_spec((1, 3 * HY_W), layer),
            *cast_in,
        ],
        out_specs=(pl.BlockSpec((t, ATTN_W), row),
                   pl.BlockSpec((t, KV_W), row),
                   pl.BlockSpec((KV_W, t), lambda m: (0, m)),
                   pl.BlockSpec((t, KV_W), row),
                   pl.BlockSpec((t, KV_W), row),
                   pl.BlockSpec((t, HY_W), row),
                   pl.BlockSpec((t, HY_W), row),
                   *cast_out),
        scratch_shapes=[pltpu.VMEM((t + 2 * HALO, D_MODEL), BF16)],
        compiler_params=_params("arbitrary"),
        name="in_proj",
    )(h, h, h, w_in, w_in, w_in, cos_tab, sin_tab, conv_w, conv_b, *(w for w, _ in to_cast))


def _rope_tables():
    rows = DEC_SEQ // GRID_W
    r, col = jnp.meshgrid(jnp.arange(rows), jnp.arange(GRID_W), indexing='ij')
    pos = jnp.stack([r.reshape(-1), col.reshape(-1)], axis=-1).astype(F32)
    half = HEAD_DIM // 2
    inv_freq = 1.0 / (ROPE_BASE ** (jnp.arange(0, half, 2, dtype=F32) / half))
    ang = pos[:, :, None] * inv_freq
    ang = jnp.stack([ang, ang], axis=-2).reshape(DEC_SEQ, HEAD_DIM)
    cos = jnp.tile(jnp.cos(ang), (1, ROPE_LANES // HEAD_DIM))
    sin = jnp.tile(jnp.sin(ang), (1, ROPE_LANES // HEAD_DIM))
    lane = jnp.arange(ROPE_LANES)
    sin = jnp.where((lane & (HEAD_DIM // 4)) == 0, -sin, sin)
    n_slab = DEC_SEQ // ROW_TILE
    cos = jnp.concatenate([jnp.ones((1, ROW_TILE, ROPE_LANES), F32),
                           cos.reshape(n_slab, ROW_TILE, ROPE_LANES)], axis=0)
    sin = jnp.concatenate([jnp.zeros((1, ROW_TILE, ROPE_LANES), F32),
                           sin.reshape(n_slab, ROW_TILE, ROPE_LANES)], axis=0)
    return cos, sin


def _filter_kernel(feat_ref, t_ref, dec_ref, w1_ref, b1_ref, f1_ref, w2_ref, b2_ref, f2_ref,
                   w3_ref, o_ref):
    a = _dot_split(feat_ref[...], w1_ref[...]) + b1_ref[...]
    a = jnp.sin(f1_ref[...] * a)
    a = _dot_split(a, w2_ref[...]) + b2_ref[...]
    a = jnp.sin(f2_ref[...] * a)
    hh = _dot_split(a, w3_ref[...])
    window = jnp.exp(-t_ref[...] * dec_ref[...])
    h_fwd = hh[:, 0:HY_W] * window
    h_bwd = hh[:, HY_W:2 * HY_W] * window
    row = lax.broadcasted_iota(jnp.int32, h_bwd.shape, 0)
    h_bwd = jnp.where(row == 0, 0.0, h_bwd)
    o_ref[:, 0:HY_W] = (h_fwd + h_bwd).astype(BF16)
    o_ref[:, HY_W:2 * HY_W] = (h_fwd - h_bwd).astype(BF16)


def _filters(seq, w1, b1, f1, w2, b2, f2, w3):
    t = jnp.linspace(0.0, 1.0, seq, dtype=F32)[:, None]
    w = 2.0 * math.pi * jnp.arange(seq, dtype=F32) / seq
    f = jnp.linspace(1e-4, FILT_BANDS - 1, FILT_BANDS, dtype=F32)
    zr = w[:, None] * f[None, :]
    feats = jnp.concatenate([t, jnp.cos(zr), -jnp.sin(zr),
                             jnp.zeros((seq, FILT_EMB_PAD - FILT_EMB), F32)], axis=-1)
    target = 1e-2
    decay = jnp.abs(jnp.linspace(math.log(target) / 1.5, math.log(target) / 0.3, HY_W,
                                 dtype=F32))[None, :]
    w1p = jnp.pad(w1, ((0, 0), (0, FILT_EMB_PAD - FILT_EMB), (0, 0)))
    vec = lambda a: a.reshape(DEPTH, 1, FILT_HIDDEN)
    return pl.pallas_call(
        _filter_kernel,
        out_shape=jax.ShapeDtypeStruct((DEPTH, seq, 2 * HY_W), BF16),
        grid=(DEPTH,),
        in_specs=[
            pl.BlockSpec((seq, FILT_EMB_PAD), lambda l: (0, 0)),
            pl.BlockSpec((seq, 1), lambda l: (0, 0)),
            pl.BlockSpec((1, HY_W), lambda l: (0, 0)),
            pl.BlockSpec((None, FILT_EMB_PAD, FILT_HIDDEN), lambda l: (l, 0, 0)),
            pl.BlockSpec((None, 1, FILT_HIDDEN), lambda l: (l, 0, 0)),
            pl.BlockSpec((None, 1, FILT_HIDDEN), lambda l: (l, 0, 0)),
            pl.BlockSpec((None, FILT_HIDDEN, FILT_HIDDEN), lambda l: (l, 0, 0)),
            pl.BlockSpec((None, 1, FILT_HIDDEN), lambda l: (l, 0, 0)),
            pl.BlockSpec((None, 1, FILT_HIDDEN), lambda l: (l, 0, 0)),
            pl.BlockSpec((None, FILT_HIDDEN, 2 * HY_W), lambda l: (l, 0, 0)),
        ],
        out_specs=pl.BlockSpec((None, seq, 2 * HY_W), lambda l: (l, 0, 0)),
        compiler_params=_params("parallel"),
        name=f"hyena_filter_{seq}",
    )(feats, t, decay, w1p, vec(b1), vec(f1), w2, vec(b2), vec(f2), w3)


TABLE_ROWS = 16


def _phase(a, b, period):
    k = (a * b) & (period - 1)
    ang = k.astype(F32) * (2.0 * math.pi / period)
    return jnp.cos(ang), jnp.sin(ang)


def _table_kernel(o_ref, *, half, row_mult, row_add, col_mult, col_add, stack_rows, scale):
    period = 8 * half
    groups = half // TABLE_ROWS
    b0 = col_mult * lax.broadcasted_iota(jnp.int32, (TABLE_ROWS, half), 1) + col_add
    a0 = row_mult * lax.broadcasted_iota(jnp.int32, (TABLE_ROWS, half), 0) + row_add
    cos_a, sin_a = _phase(a0, b0, period)
    b1 = col_mult * lax.broadcasted_iota(jnp.int32, (groups, half), 1) + col_add
    a1 = row_mult * TABLE_ROWS * lax.broadcasted_iota(jnp.int32, (groups, half), 0)
    cos_b, sin_b = _phase(a1, b1, period)
    cos = cos_a[None] * cos_b[:, None, :] - sin_a[None] * sin_b[:, None, :]
    msin = -(sin_a[None] * cos_b[:, None, :] + cos_a[None] * sin_b[:, None, :])
    cos = (cos.reshape(half, half) * scale).astype(BF16)
    msin = (msin.reshape(half, half) * scale).astype(BF16)
    if stack_rows:
        o_ref[0:half, :] = cos
        o_ref[half:2 * half, :] = msin
    else:
        o_ref[:, 0:half] = cos
        o_ref[:, half:2 * half] = msin


def _dft_tables(seq):
    half = seq // 2

    def table(stack_rows, **kw):
        shape = (2 * half, half) if stack_rows else (half, 2 * half)
        return pl.pallas_call(
            functools.partial(_table_kernel, half=half, stack_rows=stack_rows, **kw),
            out_shape=jax.ShapeDtypeStruct(shape, BF16),
            compiler_params=_params(),
            name=f"dft_table_{seq}",
        )()

    freq = dict(row_mult=2, row_add=1)
    freq_cols = dict(col_mult=2, col_add=1)
    return (table(True, **freq, col_mult=2, col_add=0, scale=1.0),
            table(True, **freq, col_mult=2, col_add=1, scale=1.0),
            table(False, row_mult=2, row_add=0, **freq_cols, scale=1.0 / seq),
            table(False, row_mult=2, row_add=1, **freq_cols, scale=1.0 / seq))


LANES = 128


def _split_rows(x, nat_ref):
    n = x.shape[0] // 2
    k = x.shape[1] // LANES
    for i in range(k):
        nat_ref[i] = x[:, LANES * i:LANES * (i + 1)]
    pick = lambda start: jnp.concatenate(
        [nat_ref[i, pl.ds(start, n, stride=2), :] for i in range(k)], axis=1)
    return pick(0), pick(1)


def _interleave_rows(even, odd, nat_ref):
    n = even.shape[0]
    k = even.shape[1] // LANES
    for i in range(k):
        nat_ref[i, pl.ds(0, n, stride=2), :] = even[:, LANES * i:LANES * (i + 1)]
        nat_ref[i, pl.ds(1, n, stride=2), :] = odd[:, LANES * i:LANES * (i + 1)]
    return jnp.concatenate([nat_ref[i] for i in range(k)], axis=1)


def _spectrum_kernel(fe_ref, fo_ref, h_ref, g_ref, nat_ref, *, half):
    even, odd = _split_rows(h_ref[...].astype(F32), nat_ref)
    even, odd = even.astype(BF16), odd.astype(BF16)
    w = HY_W
    cos_even = _dot(fe_ref[0:half, :], even[:, 0:w])
    sin_even = _dot(fe_ref[half:2 * half, :], even[:, w:2 * w])
    cos_odd = _dot(fo_ref[0:half, :], odd[:, 0:w])
    sin_odd = _dot(fo_ref[half:2 * half, :], odd[:, w:2 * w])
    g_ref[0] = cos_even + cos_odd
    g_ref[1] = sin_even + sin_odd
    g_ref[2] = cos_even - cos_odd
    g_ref[3] = sin_odd - sin_even


def _filter_spectrum(seq, fwd_even, fwd_odd, hs):
    half = seq // 2
    table = pl.BlockSpec((2 * half, half), lambda l: (0, 0))
    return pl.pallas_call(
        functools.partial(_spectrum_kernel, half=half),
        out_shape=jax.ShapeDtypeStruct((DEPTH, 4, half, HY_W), F32),
        grid=(DEPTH,),
        in_specs=[table, table, pl.BlockSpec((None, seq, 2 * HY_W), lambda l: (l, 0, 0))],
        out_specs=pl.BlockSpec((None, 4, half, HY_W), lambda l: (l, 0, 0, 0)),
        scratch_shapes=[pltpu.VMEM((2 * HY_W // LANES, seq, LANES), F32)],
        compiler_params=_params("parallel"),
        name=f"hyena_spectrum_{seq}",
    )(fwd_even, fwd_odd, hs)


LONGCONV_SEQS_PER_STEP = {SEQ: 4, DEC_SEQ: 1}
CHAINS_IN_FLIGHT = 2


def _longconv_kernel(v_ref, fe_ref, fo_ref, ie_ref, io_ref, g_ref, bias_ref, o_ref,
                     vin_ref, out_ref, *, half, n_seq):
    seq = 2 * half

    def forward(s, c, slot):
        v = v_ref[seq * s:seq * (s + 1), MXU_COLS * c:MXU_COLS * (c + 1)].astype(F32)
        even, odd = _split_rows(v, vin_ref.at[slot])
        return _dot(fe_ref[...], even.astype(BF16)), _dot(fo_ref[...], odd.astype(BF16))

    def finish(s, c, slot, p, q):
        rows = slice(seq * s, seq * (s + 1))
        cs = slice(MXU_COLS * c, MXU_COLS * (c + 1))
        pr, pi = p[0:half], p[half:2 * half]
        qr, qi = q[0:half], q[half:2 * half]
        ur, ui = pr + qr, pi + qi
        wr, wi = pr - qr, qi - pi
        yr = ur * g_ref[0, :, cs] - ui * g_ref[1, :, cs]
        yi = ur * g_ref[1, :, cs] + ui * g_ref[0, :, cs]
        zr = wr * g_ref[2, :, cs] - wi * g_ref[3, :, cs]
        zi = wr * g_ref[3, :, cs] + wi * g_ref[2, :, cs]
        spec_even = jnp.concatenate([yr + zr, yi - zi], axis=0).astype(BF16)
        spec_odd = jnp.concatenate([yr - zr, yi + zi], axis=0).astype(BF16)
        y = _interleave_rows(_dot(ie_ref[...], spec_even), _dot(io_ref[...], spec_odd),
                             out_ref.at[slot])
        v = jnp.concatenate([vin_ref[slot, i] for i in range(MXU_COLS // LANES)], axis=1)
        o_ref[rows, cs] = (y + bias_ref[:, cs] * v).astype(BF16)

    chains = [(s, c) for s in range(n_seq) for c in range(HY_W // MXU_COLS)]
    _run_chains([(s, c, i % CHAINS_IN_FLIGHT) for i, (s, c) in enumerate(chains)], forward, finish)


def _long_conv(seq, n_rows, row0, vh, tables, spectrum, hy_bias, layer):
    half = seq // 2
    n_seq = LONGCONV_SEQS_PER_STEP[seq]
    t = n_seq * seq
    fwd = pl.BlockSpec((2 * half, half), lambda b: (0, 0), **RESIDENT)
    inv = pl.BlockSpec((half, 2 * half), lambda b: (0, 0), **RESIDENT)
    scratch = pltpu.VMEM((CHAINS_IN_FLIGHT, MXU_COLS // LANES, seq, LANES), F32)
    return pl.pallas_call(
        functools.partial(_longconv_kernel, half=half, n_seq=n_seq),
        out_shape=jax.ShapeDtypeStruct((n_rows, HY_W), BF16),
        grid=(n_rows // t,),
        in_specs=[
            pl.BlockSpec((t, HY_W), lambda b: (row0 // t + b, 0)),
            fwd, fwd, inv, inv,
            pl.BlockSpec((None, 4, half, HY_W), lambda b: (layer, 0, 0, 0), **RESIDENT),
            pl.BlockSpec((None, 1, HY_W), lambda b: (layer, 0, 0)),
        ],
        out_specs=pl.BlockSpec((t, HY_W), lambda b: (b, 0)),
        scratch_shapes=[scratch, scratch],
        compiler_params=_params("parallel"),
        name=f"hyena_longconv_{seq}",
    )(vh, *tables, spectrum, hy_bias.reshape(DEPTH, 1, HY_W))


ONES_ROWS = 16
CTX_SEQS_PER_STEP = 4
QBLOCKS_PER_STEP = 8


def _head_rows(q, j):
    return jnp.concatenate([q[:, HEAD_DIM * hd:HEAD_DIM * (hd + 1)]
                            for hd in range(GROUP * j, GROUP * (j + 1))], axis=0)


def _lane_sink(sink_ref, layer, j, block):
    head = lax.broadcasted_iota(jnp.int32, (1, GROUP * block), 1) // block
    sink = jnp.zeros((1, GROUP * block), F32)
    for g in range(GROUP):
        sink = jnp.where(head == g, sink_ref[layer, GROUP * j + g] * LOG2_E, sink)
    return sink


def _with_ones(vt):
    return jnp.concatenate([vt, jnp.ones((ONES_ROWS, vt.shape[1]), BF16)], axis=0)


def _softmax_pv_t(parts, sink):
    m = sink
    for lg, _ in parts:
        m = jnp.maximum(m, jnp.max(lg, axis=0, keepdims=True))
    out = None
    for lg, vt in parts:
        pv = _dot(vt, jnp.exp2(lg - m).astype(BF16))
        out = pv if out is None else out + pv
    den = out[HEAD_DIM:HEAD_DIM + 1, :] + jnp.exp2(sink - m)
    return out[0:HEAD_DIM, :] / den


def _store_heads(o_ref, rows, j, block, o_t):
    for g in range(GROUP):
        hd = GROUP * j + g
        o_ref[rows, HEAD_DIM * hd:HEAD_DIM * (hd + 1)] = (
            o_t[:, block * g:block * (g + 1)].T.astype(BF16))


def _ctx_attn_kernel(sink_ref, q_ref, k_ref, vt_ref, o_ref, *, layer):
    def logits(n, j):
        rows = slice(SEQ * n, SEQ * (n + 1))
        hs = slice(HEAD_DIM * j, HEAD_DIM * (j + 1))
        return _dot_nt(k_ref[rows, hs], _head_rows(q_ref[rows, :], j)), _with_ones(vt_ref[hs, rows])

    def finish(n, j, s, vt):
        o_t = _softmax_pv_t([(s, vt)], _lane_sink(sink_ref, layer, j, SEQ))
        _store_heads(o_ref, slice(SEQ * n, SEQ * (n + 1)), j, SEQ, o_t)

    _run_chains([(n, j) for n in range(CTX_SEQS_PER_STEP) for j in range(N_KV_HEADS)],
                logits, finish)


def _ctx_attention(sink, q, k, vt, layer):
    t = CTX_SEQS_PER_STEP * SEQ
    row = lambda b: (b, 0)
    return pl.pallas_call(
        functools.partial(_ctx_attn_kernel, layer=layer),
        out_shape=jax.ShapeDtypeStruct((N_CTX, ATTN_W), BF16),
        grid=(N_CTX // t,),
        in_specs=[
            pl.BlockSpec(memory_space=pltpu.SMEM),
            pl.BlockSpec((t, ATTN_W), row),
            pl.BlockSpec((t, KV_W), row),
            pl.BlockSpec((KV_W, t), lambda b: (0, b)),
        ],
        out_specs=pl.BlockSpec((t, ATTN_W), row),
        compiler_params=_params("parallel"),
        name="context_attention",
    )(sink, q, k, vt)


def _band_bias():
    kpos = jnp.arange(BAND)[None, :, None] - jnp.arange(3)[:, None, None] * WINDOW
    qpos = (jnp.arange(GROUP * QBLOCK) & (QBLOCK - 1))[None, None, :]
    return jnp.where(jnp.abs(qpos - kpos) <= WINDOW, 0.0, MASK_VALUE).astype(F32)


def _lat_attn_kernel(sink_ref, q_ref, k_ref, vt_ref, ck_ref, cv_ref, bias_ref, o_ref,
                     kc_ref, vct_ref, *, layer):
    i = pl.program_id(1)

    @pl.when(i == 0)
    def _():
        kc_ref[...] = ck_ref[...].astype(BF16)
        vct = cv_ref[...].T.astype(BF16)
        for j in range(N_KV_HEADS):
            vct_ref[j] = _with_ones(vct[HEAD_DIM * j:HEAD_DIM * (j + 1), :])

    def logits(n, j):
        start = (i * QBLOCKS_PER_STEP + n) * QBLOCK
        ws = pl.multiple_of(jnp.clip(start - WINDOW, 0, DEC_SEQ - BAND), QBLOCK)
        placement = jnp.where(start < WINDOW, 0, jnp.where(start + QBLOCK + WINDOW > DEC_SEQ, 2, 1))
        q4 = _head_rows(q_ref[QBLOCK * n:QBLOCK * (n + 1), :], j)
        hs = slice(HEAD_DIM * j, HEAD_DIM * (j + 1))
        s_ctx = _dot_nt(kc_ref[:, hs], q4)
        s_band = _dot_nt(k_ref[pl.ds(ws, BAND), hs], q4) + bias_ref[placement]
        return s_ctx, s_band, _with_ones(vt_ref[hs, pl.ds(ws, BAND)])

    def finish(n, j, s_ctx, s_band, vbt):
        o_t = _softmax_pv_t([(s_ctx, vct_ref[j]), (s_band, vbt)],
                            _lane_sink(sink_ref, layer, j, QBLOCK))
        _store_heads(o_ref, slice(QBLOCK * n, QBLOCK * (n + 1)), j, QBLOCK, o_t)

    _run_chains([(n, j) for n in range(QBLOCKS_PER_STEP) for j in range(N_KV_HEADS)],
                logits, finish)


def _lat_attention(sink, q, k, vt, cache_k, cache_v, band_bias, layer):
    t = QBLOCKS_PER_STEP * QBLOCK
    steps = DEC_SEQ // t
    seq_block = N_CTX // DEC_SEQ
    return pl.pallas_call(
        functools.partial(_lat_attn_kernel, layer=layer),
        out_shape=jax.ShapeDtypeStruct((N_LAT, ATTN_W), BF16),
        grid=(DEC_BATCH, steps),
        in_specs=[
            pl.BlockSpec(memory_space=pltpu.SMEM),
            pl.BlockSpec((t, ATTN_W), lambda b, i: (N_CTX // t + b * steps + i, 0)),
            pl.BlockSpec((DEC_SEQ, KV_W), lambda b, i: (seq_block + b, 0)),
            pl.BlockSpec((KV_W, DEC_SEQ), lambda b, i: (0, seq_block + b)),
            pl.BlockSpec((None, None, PAST_LEN, KV_W), lambda b, i: (b, layer, 0, 0)),
            pl.BlockSpec((None, None, PAST_LEN, KV_W), lambda b, i: (b, layer, 0, 0)),
            pl.BlockSpec((3, BAND, GROUP * QBLOCK), lambda b, i: (0, 0, 0)),
        ],
        out_specs=pl.BlockSpec((t, ATTN_W), lambda b, i: (b * steps + i, 0)),
        scratch_shapes=[pltpu.VMEM((PAST_LEN, KV_W), BF16),
                        pltpu.VMEM((N_KV_HEADS, HEAD_DIM + ONES_ROWS, PAST_LEN), BF16)],
        compiler_params=_params("parallel", "arbitrary"),
        name="latent_attention",
    )(sink, q, k, vt, cache_k, cache_v, band_bias)


N_GATE_CHUNKS = D_MODEL // MXU_COLS


def _merge_kernel(h_ref, atc_ref, atl_ref, ybc_ref, ybl_ref, x0_ref, x_ref, mod_ref, *rest):
    wga_refs = rest[0:N_GATE_CHUNKS]
    wgb_refs = rest[N_GATE_CHUNKS:2 * N_GATE_CHUNKS]
    wpa_ref, wpb_ref, wo_ref, g_ref, b_ref, xo_ref, ho_ref, mg_ref = rest[2 * N_GATE_CHUNKS:]
    is_ctx = pl.program_id(0) < N_CTX // ROW_TILE
    g1 = mod_ref[:, 2 * D_MODEL:3 * D_MODEL]
    sh2 = mod_ref[:, 3 * D_MODEL:4 * D_MODEL]
    sc2 = mod_ref[:, 4 * D_MODEL:5 * D_MODEL]
    for s in range(ROW_TILE // SUB_ROWS):
        rs = slice(SUB_ROWS * s, SUB_ROWS * (s + 1))
        h = h_ref[rs, :]
        attn = jnp.where(is_ctx, atc_ref[rs, :], atl_ref[rs, :])
        conv = jnp.where(is_ctx, ybc_ref[rs, :], ybl_ref[rs, :])
        hy = (x0_ref[rs, :].astype(F32) * conv.astype(F32)).astype(BF16)
        for j in range(N_GATE_CHUNKS):
            cs = slice(MXU_COLS * j, MXU_COLS * (j + 1))
            ga = jax.nn.sigmoid(_dot(h, wga_refs[j][...]))
            gb = jax.nn.sigmoid(_dot(h, wgb_refs[j][...]))
            mg_ref[rs, cs] = (ga * _dot(attn, wpa_ref[:, cs])
                              + gb * _dot(hy, wpb_ref[:, cs])).astype(BF16)
        sub = _dot(mg_ref[rs, :], wo_ref[...])
        x = _layer_norm(DEEPNORM_ALPHA * x_ref[rs, :] + g1 * sub, g_ref[...], b_ref[...])
        xo_ref[rs, :] = x
        ho_ref[rs, :] = (x * (1.0 + sc2) + sh2).astype(BF16)


def _merge(h, atc, atl, ybc, ybl, x0, x, mods, w_in, wpa, wpb, wo, ln_g, ln_b, layer):
    t = ROW_TILE
    nct = N_CTX // t
    row = lambda m: (m, 0)
    ctx_row = lambda m: (jnp.minimum(m, nct - 1), 0)
    lat_row = lambda m: (jnp.maximum(m - nct, 0), 0)
    gate_block0 = GATE_COL0 // MXU_COLS
    gate_specs = [_weight_spec((D_MODEL, MXU_COLS), (0, gate_block0 + j))
                  for j in range(2 * N_GATE_CHUNKS)]
    return pl.pallas_call(
        _merge_kernel,
        out_shape=(jax.ShapeDtypeStruct((N_TOK, D_MODEL), F32),
                   jax.ShapeDtypeStruct((N_TOK, D_MODEL), BF16)),
        grid=(N_TOK // t,),
        in_specs=[
            pl.BlockSpec((t, D_MODEL), row),
            pl.BlockSpec((t, ATTN_W), ctx_row),
            pl.BlockSpec((t, ATTN_W), lat_row),
            pl.BlockSpec((t, HY_W), ctx_row),
            pl.BlockSpec((t, HY_W), lat_row),
            pl.BlockSpec((t, HY_W), row),
            pl.BlockSpec((t, D_MODEL), row),
            _mod_spec(layer, t),
            *gate_specs,
            _weight_spec((ATTN_W, D_MODEL)),
            _weight_spec((HY_W, D_MODEL)),
            _weight_spec((D_MODEL, D_MODEL)),
            _layer_spec((1, D_MODEL), layer),
            _layer_spec((1, D_MODEL), layer),
        ],
        out_specs=(pl.BlockSpec((t, D_MODEL), row), pl.BlockSpec((t, D_MODEL), row)),
        scratch_shapes=[pltpu.VMEM((t, D_MODEL), BF16)],
        compiler_params=_params("parallel"),
        name="merge_ln1",
    )(h, atc, atl, ybc, ybl, x0, x, mods, *([w_in] * (2 * N_GATE_CHUNKS)), wpa, wpb, wo,
      ln_g, ln_b)


def _ffn_kernel(h_ref, hp_ref, hn_ref, x_ref, mod_ref, nmod_ref, wg_ref, wv_ref, wd_ref,
                cw_ref, cb_ref, g_ref, b_ref, *rest, last_layer):
    if last_layer:
        yp_ref, ys_ref, hx_ref, a_ref, xo_ref = rest
    else:
        xo_ref, ho_ref, hx_ref, a_ref = rest
    n_sub = ROW_TILE // SUB_ROWS
    ext = SUB_ROWS + 2 * HALO
    is_ctx, seq_start, seq_end = _tile_edges(pl.program_id(0), ROW_TILE)
    _fill_extended(hx_ref, hp_ref, h_ref, hn_ref)
    g2 = mod_ref[:, 5 * D_MODEL:6 * D_MODEL]
    sh = nmod_ref[:, 0:D_MODEL]
    sc = nmod_ref[:, D_MODEL:2 * D_MODEL]
    for j in range(n_sub):
        rs = slice(SUB_ROWS * j, SUB_ROWS * (j + 1))
        zero_top, zero_bot = _sub_edges(j, n_sub, is_ctx, seq_start, seq_end)
        hx = hx_ref[SUB_ROWS * j:SUB_ROWS * j + ext, :]
        hm = h_ref[rs, :]
        for c in range(D_FF // MXU_COLS):
            cs = slice(MXU_COLS * c, MXU_COLS * (c + 1))
            gate = _dwconv3(_dot(hx, wg_ref[:, cs]), zero_top, zero_bot, cw_ref, cb_ref, cs)
            gelu = 0.5 * gate * (1.0 + lax.erf(gate * math.sqrt(0.5)))
            a_ref[rs, cs] = (gelu * _dot(hm, wv_ref[:, cs])).astype(BF16)
        sub = _dot(a_ref[rs, :], wd_ref[...])
        x = _layer_norm(DEEPNORM_ALPHA * x_ref[rs, :] + g2 * sub, g_ref[...], b_ref[...])
        xo_ref[rs, :] = x
        if not last_layer:
            ho_ref[rs, :] = (x * (1.0 + sc) + sh).astype(BF16)

    if last_layer:
        @pl.when(is_ctx)
        def _():
            yp_ref[...] = xo_ref[...]

        @pl.when(jnp.logical_not(is_ctx))
        def _():
            ys_ref[...] = xo_ref[...]


def _ffn(h, x, mods, w_up, w_down, conv_w, conv_b, ln_g, ln_b, layer):
    t = ROW_TILE
    nct = N_CTX // t
    row = lambda m: (m, 0)
    last_layer = layer == DEPTH - 1
    if last_layer:
        out_shape = (jax.ShapeDtypeStruct((N_CTX, D_MODEL), F32),
                     jax.ShapeDtypeStruct((N_LAT, D_MODEL), F32))
        out_specs = (pl.BlockSpec((t, D_MODEL), lambda m: (jnp.minimum(m, nct - 1), 0)),
                     pl.BlockSpec((t, D_MODEL), lambda m: (jnp.maximum(m - nct, 0), 0)))
        scratch = [pltpu.VMEM((t, D_MODEL), F32)]
    else:
        out_shape = (jax.ShapeDtypeStruct((N_TOK, D_MODEL), F32),
                     jax.ShapeDtypeStruct((N_TOK, D_MODEL), BF16))
        out_specs = (pl.BlockSpec((t, D_MODEL), row), pl.BlockSpec((t, D_MODEL), row))
        scratch = []
    return pl.pallas_call(
        functools.partial(_ffn_kernel, last_layer=last_layer),
        out_shape=out_shape,
        grid=(N_TOK // t,),
        in_specs=[
            pl.BlockSpec((t, D_MODEL), row),
            *_halo_specs(t, D_MODEL),
            pl.BlockSpec((t, D_MODEL), row),
            _mod_spec(layer, t),
            _mod_spec(min(layer + 1, DEPTH - 1), t),
            _weight_spec((D_MODEL, D_FF), (0, 0)),
            _weight_spec((D_MODEL, D_FF), (0, 1)),
            _weight_spec((D_FF, D_MODEL)),
            _layer_spec((3, D_FF), layer),
            _layer_spec((1, D_FF), layer),
            _layer_spec((1, D_MODEL), layer),
            _layer_spec((1, D_MODEL), layer),
        ],
        out_specs=out_specs,
        scratch_shapes=[pltpu.VMEM((t + 2 * HALO, D_MODEL), BF16), pltpu.VMEM((t, D_FF), BF16),
                        *scratch],
        compiler_params=_params("arbitrary"),
        name="conv_ffn_ln2",
    )(h, h, h, x, mods, mods, w_up, w_up, w_down, conv_w, conv_b, ln_g, ln_b)


def kernel(x_prompt, x_sample, cache_k, cache_v, c, c_ctx, w_ada, b_ada, w_in, attn_sink,
           hy_conv_w, hy_conv_b, filt_w1, filt_b1, filt_freq1, filt_w2, filt_b2, filt_freq2,
           filt_w3, hy_bias, w_pa, w_pb, w_out, ln1_g, ln1_b, w_up, ffn_conv_w, ffn_conv_b,
           w_down, ln2_g, ln2_b):
    cond = jnp.concatenate([c_ctx[None, :], c,
                            jnp.zeros((N_MOD_ROWS - 1 - DEC_BATCH, D_MODEL), F32)], axis=0)
    mods = _modulation(cond, w_ada, b_ada).reshape(DEPTH, N_MOD_ROWS, 1, 6 * D_MODEL)

    x, h = _prologue(x_prompt.reshape(N_CTX, D_MODEL), x_sample.reshape(N_LAT, D_MODEL), mods)

    cos_tab, sin_tab = _rope_tables()
    conv_args = {}
    for seq in (SEQ, DEC_SEQ):
        tables = _dft_tables(seq)
        hs = _filters(seq, filt_w1, filt_b1, filt_freq1, filt_w2, filt_b2, filt_freq2, filt_w3)
        conv_args[seq] = (tables, _filter_spectrum(seq, tables[0], tables[1], hs))

    cache_k = cache_k.reshape(DEC_BATCH, DEPTH, PAST_LEN, KV_W)
    cache_v = cache_v.reshape(DEC_BATCH, DEPTH, PAST_LEN, KV_W)
    hy_conv_b3 = hy_conv_b.reshape(DEPTH, 1, 3 * HY_W)
    ffn_conv_b3 = ffn_conv_b.reshape(DEPTH, 1, D_FF)
    ln1_g3, ln1_b3 = ln1_g.reshape(DEPTH, 1, D_MODEL), ln1_b.reshape(DEPTH, 1, D_MODEL)
    ln2_g3, ln2_b3 = ln2_g.reshape(DEPTH, 1, D_MODEL), ln2_b.reshape(DEPTH, 1, D_MODEL)

    band_bias = _band_bias()
    w_in_l = w_in[0:1].astype(BF16)
    keys, values = [], []
    for l in range(DEPTH):
        to_cast = [(w, l) for w in (w_pa, w_pb, w_out, w_up, w_down)]
        if l + 1 < DEPTH:
            to_cast.append((w_in, l + 1))
        q, k, vt, kf, vf, x0, vh, w_pa_l, w_pb_l, w_out_l, w_up_l, w_down_l, *w_in_next = _in_proj(
            h, w_in_l, cos_tab, sin_tab, hy_conv_w, hy_conv_b3, to_cast, l)
        keys.append(kf[:N_CTX])
        values.append(vf[:N_CTX])

        ybc = _long_conv(SEQ, N_CTX, 0, vh, *conv_args[SEQ], hy_bias, l)
        ybl = _long_conv(DEC_SEQ, N_LAT, N_CTX, vh, *conv_args[DEC_SEQ], hy_bias, l)
        atc = _ctx_attention(attn_sink, q, k, vt, l)
        atl = _lat_attention(attn_sink, q, k, vt, cache_k, cache_v, band_bias, l)

        x, h2 = _merge(h, atc, atl, ybc, ybl, x0, x, mods, w_in_l, w_pa_l, w_pb_l, w_out_l,
                       ln1_g3, ln1_b3, l)
        x, h = _ffn(h2, x, mods, w_up_l, w_down_l, ffn_conv_w, ffn_conv_b3, ln2_g3, ln2_b3, l)
        if w_in_next:
            w_in_l = w_in_next[0]

    y_prompt_rows, y_sample_rows = x, h
    y_prompt = y_prompt_rows.reshape(BATCH, SEQ, D_MODEL)
    y_sample = y_sample_rows.reshape(DEC_BATCH, DEC_SEQ, D_MODEL)
    to_cache = lambda ts: jnp.stack(ts, axis=1).reshape(BATCH, DEPTH, SEQ, N_KV_HEADS, HEAD_DIM)
    new_k = to_cache([t.reshape(BATCH, SEQ, KV_W) for t in keys])
    new_v = to_cache([t.reshape(BATCH, SEQ, KV_W) for t in values])
    return (y_prompt, y_sample, new_k, new_v)
```

```python
import functools
import math

import jax
import jax.numpy as jnp
from jax import lax
from jax.experimental import pallas as pl
from jax.experimental.pallas import tpu as pltpu

D_MODEL = 1024
BATCH = 16
SEQ = 256
DEPTH = 4
DEC_BATCH = 4
DEC_SEQ = 2048
PAST_LEN = 512
GRID_W = 64
HEAD_DIM = 64
N_HEADS = 8
N_KV_HEADS = 2
GROUP = N_HEADS // N_KV_HEADS
ATTN_W = N_HEADS * HEAD_DIM
KV_W = N_KV_HEADS * HEAD_DIM
WINDOW = 128
QBLOCK = 128
HY_W = 512
FILT_EMB = 33
FILT_EMB_PAD = 40
FILT_BANDS = (FILT_EMB - 1) // 2
FILT_HIDDEN = 64
D_FF = 2816
ROPE_BASE = 10000.0
LN_EPS = 1e-5
DEEPNORM_ALPHA = (2 * DEPTH) ** 0.25

N_CTX = BATCH * SEQ
N_LAT = DEC_BATCH * DEC_SEQ
N_TOK = N_CTX + N_LAT
N_MOD_ROWS = 8
QKV_W = ATTN_W + 2 * KV_W
HY_BLOCK = QKV_W
assert 3 * HY_W == 2 * HY_BLOCK
GATE_COL0 = QKV_W + 3 * HY_W
BAND = QBLOCK + 2 * WINDOW
MASK_VALUE = -1e30
LOG2_E = math.log2(math.e)

F32 = jnp.float32
BF16 = jnp.bfloat16

VMEM_LIMIT_BYTES = 56 * 1024 * 1024
MXU_COLS = 256
ROW_TILE = 1024
SUB_ROWS = SEQ
HALO = 16
RESIDENT = dict(pipeline_mode=pl.Buffered(1))
CAST_BLOCKS = 8


def _params(*semantics):
    return pltpu.CompilerParams(dimension_semantics=semantics, vmem_limit_bytes=VMEM_LIMIT_BYTES)


def _dot(a, b):
    return jnp.dot(a, b, preferred_element_type=F32)


def _dot_split(a, b):
    a_hi = a.astype(BF16)
    b_hi = b.astype(BF16)
    a_lo = (a - a_hi.astype(F32)).astype(BF16)
    b_lo = (b - b_hi.astype(F32)).astype(BF16)
    return _dot(a_hi, b_hi) + (_dot(a_hi, b_lo) + _dot(a_lo, b_hi))


def _dot_nt(a, b):
    return lax.dot_general(a, b, (((1,), (1,)), ((), ())), preferred_element_type=F32)


def _mod_row(tile_rows):
    n_ctx_tiles = N_CTX // tile_rows
    tiles_per_seq = DEC_SEQ // tile_rows

    def fn(m):
        return jnp.where(m < n_ctx_tiles, 0, 1 + (m - n_ctx_tiles) // tiles_per_seq)

    return fn


def _mod_spec(layer, tile_rows):
    grp = _mod_row(tile_rows)
    return pl.BlockSpec((None, None, 1, 6 * D_MODEL), lambda m, *_: (layer, grp(m), 0, 0))


def _layer_spec(shape, layer):
    zeros = (0,) * len(shape)
    return pl.BlockSpec((None,) + tuple(shape), lambda *_: (layer,) + zeros)


def _weight_spec(shape, block_index=None):
    block_index = block_index or (0,) * len(shape)
    return pl.BlockSpec((None,) + tuple(shape), lambda *_: (0,) + tuple(block_index), **RESIDENT)


def _layer_norm(y, g, b):
    mu = jnp.mean(y, axis=-1, keepdims=True)
    yc = y - mu
    var = jnp.mean(yc * yc, axis=-1, keepdims=True)
    return yc * lax.rsqrt(var + LN_EPS) * g + b


def _run_chains(chains, first_fn, finish_fn, ahead=1):
    pending = [first_fn(*chain) for chain in chains[:ahead]]
    for c, chain in enumerate(chains):
        if c + ahead < len(chains):
            pending.append(first_fn(*chains[c + ahead]))
        finish_fn(*chain, *pending.pop(0))


def _tile_edges(m, tile_rows):
    n_ctx_tiles = N_CTX // tile_rows
    tiles_per_seq = DEC_SEQ // tile_rows
    is_ctx = m < n_ctx_tiles
    lat_pos = (m - n_ctx_tiles) % tiles_per_seq
    return (is_ctx, jnp.logical_or(is_ctx, lat_pos == 0),
            jnp.logical_or(is_ctx, lat_pos == tiles_per_seq - 1))


def _halo_specs(tile_rows, n_cols):
    tb = tile_rows // HALO
    last = N_TOK // HALO - 1
    return (pl.BlockSpec((HALO, n_cols), lambda m: (jnp.maximum(m * tb - 1, 0), 0)),
            pl.BlockSpec((HALO, n_cols), lambda m: (jnp.minimum((m + 1) * tb, last), 0)))


def _fill_extended(hx_ref, hp_ref, h_ref, hn_ref):
    rows = h_ref.shape[0]
    hx_ref[0:HALO, :] = hp_ref[...]
    hx_ref[HALO:HALO + rows, :] = h_ref[...]
    hx_ref[HALO + rows:2 * HALO + rows, :] = hn_ref[...]


def _sub_edges(j, n_sub, is_ctx, seq_start, seq_end):
    return (seq_start if j == 0 else is_ctx), (seq_end if j == n_sub - 1 else is_ctx)


def _dwconv3(z, zero_top, zero_bot, w_ref, b_ref, cs):
    n = z.shape[0] - 2 * HALO
    top = jnp.where(zero_top, 0.0, z[HALO - 8:HALO])
    bot = jnp.where(zero_bot, 0.0, z[HALO + n:HALO + n + 8])
    z = jnp.concatenate([z[0:HALO - 8], top, z[HALO:HALO + n], bot, z[HALO + n + 8:]], axis=0)
    prev = pltpu.roll(z, 1, 0)[HALO:HALO + n]
    nxt = pltpu.roll(z, z.shape[0] - 1, 0)[HALO:HALO + n]
    return (prev * w_ref[0:1, cs] + z[HALO:HALO + n] * w_ref[1:2, cs] + nxt * w_ref[2:3, cs]
            + b_ref[:, cs])


def _mod_kernel(cond_ref, w_ref, b_ref, o_ref):
    c = cond_ref[...]
    s = (c * jax.nn.sigmoid(c)).astype(BF16)
    o_ref[...] = _dot(s, w_ref[...].astype(BF16)) + b_ref[...]


def _modulation(cond, w_ada, b_ada):
    n_col = 6 * D_MODEL // D_MODEL
    return pl.pallas_call(
        _mod_kernel,
        out_shape=jax.ShapeDtypeStruct((DEPTH, N_MOD_ROWS, 6 * D_MODEL), F32),
        grid=(DEPTH, n_col),
        in_specs=[
            pl.BlockSpec((N_MOD_ROWS, D_MODEL), lambda l, j: (0, 0)),
            pl.BlockSpec((None, D_MODEL, D_MODEL), lambda l, j: (l, 0, j)),
            pl.BlockSpec((None, 1, D_MODEL), lambda l, j: (l, 0, j)),
        ],
        out_specs=pl.BlockSpec((None, N_MOD_ROWS, D_MODEL), lambda l, j: (l, 0, j)),
        compiler_params=_params("parallel", "parallel"),
        name="modulation",
    )(cond, w_ada, b_ada.reshape(DEPTH, 1, 6 * D_MODEL))


PRO_TILE = 1024


def _prologue_kernel(xp_ref, xs_ref, mod_ref, x_ref, h_ref):
    m = pl.program_id(0)

    def emit(x):
        x_ref[...] = x
        sh = mod_ref[:, 0:D_MODEL]
        sc = mod_ref[:, D_MODEL:2 * D_MODEL]
        h_ref[...] = (x * (1.0 + sc) + sh).astype(BF16)

    @pl.when(m < N_CTX // PRO_TILE)
    def _():
        emit(xp_ref[...])

    @pl.when(m >= N_CTX // PRO_TILE)
    def _():
        emit(xs_ref[...])


def _prologue(xp, xs, mods):
    nct = N_CTX // PRO_TILE
    return pl.pallas_call(
        _prologue_kernel,
        out_shape=(jax.ShapeDtypeStruct((N_TOK, D_MODEL), F32),
                   jax.ShapeDtypeStruct((N_TOK, D_MODEL), BF16)),
        grid=(N_TOK // PRO_TILE,),
        in_specs=[
            pl.BlockSpec((PRO_TILE, D_MODEL), lambda m: (jnp.minimum(m, nct - 1), 0)),
            pl.BlockSpec((PRO_TILE, D_MODEL), lambda m: (jnp.maximum(m - nct, 0), 0)),
            _mod_spec(0, PRO_TILE),
        ],
        out_specs=(pl.BlockSpec((PRO_TILE, D_MODEL), lambda m: (m, 0)),
                   pl.BlockSpec((PRO_TILE, D_MODEL), lambda m: (m, 0))),
        compiler_params=_params("parallel"),
        name="prologue",
    )(xp, xs, mods)


ROPE_LANES = 2 * HEAD_DIM


N_IN_PROJ_OUTS = 7


def _in_proj_kernel(h_ref, hp_ref, hn_ref, wqkv_ref, why0_ref, why1_ref, cos_ref, sin_ref,
                    cw_ref, cb_ref, *rest):
    n_cast = (len(rest) - N_IN_PROJ_OUTS - 1) // 2
    cast_src = rest[0:n_cast]
    q_ref, k_ref, vt_ref, kf_ref, vf_ref, x0_ref, vh_ref = rest[n_cast:n_cast + N_IN_PROJ_OUTS]
    cast_dst = rest[n_cast + N_IN_PROJ_OUTS:-1]
    hx_ref = rest[-1]

    @pl.when(pl.program_id(0) < CAST_BLOCKS)
    def _():
        for src, dst in zip(cast_src, cast_dst):
            dst[...] = src[...].astype(BF16)

    n_sub = ROW_TILE // SUB_ROWS
    ext = SUB_ROWS + 2 * HALO
    is_ctx, seq_start, seq_end = _tile_edges(pl.program_id(0), ROW_TILE)
    _fill_extended(hx_ref, hp_ref, h_ref, hn_ref)
    lane = lax.broadcasted_iota(jnp.int32, (SUB_ROWS, ROPE_LANES), 1)
    first_half = (lane & (HEAD_DIM // 4)) == 0
    scale = HEAD_DIM ** -0.5 * LOG2_E
    for j in range(n_sub):
        rs = slice(SUB_ROWS * j, SUB_ROWS * (j + 1))
        cos = cos_ref[rs, :]
        sin = sin_ref[rs, :]

        def rope(x):
            partner = jnp.where(first_half, pltpu.roll(x, ROPE_LANES - HEAD_DIM // 4, 1),
                                pltpu.roll(x, HEAD_DIM // 4, 1))
            return x * cos + partner * sin

        hm = h_ref[rs, :]
        for jj in range(ATTN_W // MXU_COLS):
            qq = _dot(hm, wqkv_ref[:, MXU_COLS * jj:MXU_COLS * (jj + 1)])
            for i in range(MXU_COLS // ROPE_LANES):
                c0 = MXU_COLS * jj + ROPE_LANES * i
                q_ref[rs, c0:c0 + ROPE_LANES] = (
                    rope(qq[:, ROPE_LANES * i:ROPE_LANES * (i + 1)]) * scale).astype(BF16)
        kv = _dot(hm, wqkv_ref[:, ATTN_W:QKV_W])
        k = kv[:, 0:KV_W]
        v = kv[:, KV_W:2 * KV_W]
        kf_ref[rs, :] = k
        vf_ref[rs, :] = v
        k_ref[rs, :] = rope(k).astype(BF16)
        vt_ref[:, rs] = v.T.astype(BF16)

        zero_top, zero_bot = _sub_edges(j, n_sub, is_ctx, seq_start, seq_end)
        hx = hx_ref[SUB_ROWS * j:SUB_ROWS * j + ext, :]

        def proj_conv(c0):
            w_ref = why0_ref if c0 < HY_BLOCK else why1_ref
            w = w_ref[:, c0 % HY_BLOCK:c0 % HY_BLOCK + MXU_COLS]
            return _dwconv3(_dot(hx, w), zero_top, zero_bot, cw_ref, cb_ref,
                            slice(c0, c0 + MXU_COLS))

        for jj in range(HY_W // MXU_COLS):
            c0 = MXU_COLS * jj
            x0_ref[rs, c0:c0 + MXU_COLS] = proj_conv(c0).astype(BF16)
            x1 = proj_conv(HY_W + c0)
            u = proj_conv(2 * HY_W + c0)
            vh_ref[rs, c0:c0 + MXU_COLS] = (x1 * u).astype(BF16)


def _in_proj(h, w_in, cos_tab, sin_tab, conv_w, conv_b, to_cast, layer):
    t = ROW_TILE
    nct = N_CTX // t
    tps = DEC_SEQ // t
    tab = lambda m: (jnp.where(m < nct, 0, 1 + (m - nct) % tps), 0, 0)
    row = lambda m: (m, 0)
    bf = lambda w: jax.ShapeDtypeStruct((N_TOK, w), BF16)
    cast_block = lambda m: jnp.minimum(m, CAST_BLOCKS - 1)
    cast_in, cast_out, cast_shapes = [], [], []
    for w, w_layer in to_cast:
        rows, cols = w.shape[1] // CAST_BLOCKS, w.shape[2]
        cast_in.append(pl.BlockSpec((None, rows, cols),
                                    lambda m, w_layer=w_layer: (w_layer, cast_block(m), 0)))
        cast_out.append(pl.BlockSpec((None, rows, cols), lambda m: (0, cast_block(m), 0)))
        cast_shapes.append(jax.ShapeDtypeStruct((1,) + w.shape[1:], BF16))
    return pl.pallas_call(
        _in_proj_kernel,
        out_shape=(bf(ATTN_W), bf(KV_W),
                   jax.ShapeDtypeStruct((KV_W, N_TOK), BF16),
                   jax.ShapeDtypeStruct((N_TOK, KV_W), F32),
                   jax.ShapeDtypeStruct((N_TOK, KV_W), F32),
                   bf(HY_W), bf(HY_W), *cast_shapes),
        grid=(N_TOK // t,),
        in_specs=[
            pl.BlockSpec((t, D_MODEL), row),
            *_halo_specs(t, D_MODEL),
            _weight_spec((D_MODEL, QKV_W), (0, 0)),
            _weight_spec((D_MODEL, HY_BLOCK), (0, 1)),
            _weight_spec((D_MODEL, HY_BLOCK), (0, 2)),
            pl.BlockSpec((None, t, ROPE_LANES), tab),
            pl.BlockSpec((None, t, ROPE_LANES), tab),
            _layer_spec((3, 3 * HY_W), layer),
            _layer_spec((1, 3 * HY_W), layer),
            *cast_in,
        ],
        out_specs=(pl.BlockSpec((t, ATTN_W), row),
                   pl.BlockSpec((t, KV_W), row),
                   pl.BlockSpec((KV_W, t), lambda m: (0, m)),
                   pl.BlockSpec((t, KV_W), row),
                   pl.BlockSpec((t, KV_W), row),
                   pl.BlockSpec((t, HY_W), row),
                   pl.BlockSpec((t, HY_W), row),
                   *cast_out),
        scratch_shapes=[pltpu.VMEM((t + 2 * HALO, D_MODEL), BF16)],
        compiler_params=_params("arbitrary"),
        name="in_proj",
    )(h, h, h, w_in, w_in, w_in, cos_tab, sin_tab, conv_w, conv_b, *(w for w, _ in to_cast))


def _rope_tables():
    rows = DEC_SEQ // GRID_W
    r, col = jnp.meshgrid(jnp.arange(rows), jnp.arange(GRID_W), indexing='ij')
    pos = jnp.stack([r.reshape(-1), col.reshape(-1)], axis=-1).astype(F32)
    half = HEAD_DIM // 2
    inv_freq = 1.0 / (ROPE_BASE ** (jnp.arange(0, half, 2, dtype=F32) / half))
    ang = pos[:, :, None] * inv_freq
    ang = jnp.stack([ang, ang], axis=-2).reshape(DEC_SEQ, HEAD_DIM)
    cos = jnp.tile(jnp.cos(ang), (1, ROPE_LANES // HEAD_DIM))
    sin = jnp.tile(jnp.sin(ang), (1, ROPE_LANES // HEAD_DIM))
    lane = jnp.arange(ROPE_LANES)
    sin = jnp.where((lane & (HEAD_DIM // 4)) == 0, -sin, sin)
    n_slab = DEC_SEQ // ROW_TILE
    cos = jnp.concatenate([jnp.ones((1, ROW_TILE, ROPE_LANES), F32),
                           cos.reshape(n_slab, ROW_TILE, ROPE_LANES)], axis=0)
    sin = jnp.concatenate([jnp.zeros((1, ROW_TILE, ROPE_LANES), F32),
                           sin.reshape(n_slab, ROW_TILE, ROPE_LANES)], axis=0)
    return cos, sin


def _filter_kernel(feat_ref, t_ref, dec_ref, w1_ref, b1_ref, f1_ref, w2_ref, b2_ref, f2_ref,
                   w3_ref, o_ref):
    a = _dot_split(feat_ref[...], w1_ref[...]) + b1_ref[...]
    a = jnp.sin(f1_ref[...] * a)
    a = _dot_split(a, w2_ref[...]) + b2_ref[...]
    a = jnp.sin(f2_ref[...] * a)
    hh = _dot_split(a, w3_ref[...])
    window = jnp.exp(-t_ref[...] * dec_ref[...])
    h_fwd = hh[:, 0:HY_W] * window
    h_bwd = hh[:, HY_W:2 * HY_W] * window
    row = lax.broadcasted_iota(jnp.int32, h_bwd.shape, 0)
    h_bwd = jnp.where(row == 0, 0.0, h_bwd)
    o_ref[:, 0:HY_W] = (h_fwd + h_bwd).astype(BF16)
    o_ref[:, HY_W:2 * HY_W] = (h_fwd - h_bwd).astype(BF16)


def _filters(seq, w1, b1, f1, w2, b2, f2, w3):
    t = jnp.linspace(0.0, 1.0, seq, dtype=F32)[:, None]
    w = 2.0 * math.pi * jnp.arange(seq, dtype=F32) / seq
    f = jnp.linspace(1e-4, FILT_BANDS - 1, FILT_BANDS, dtype=F32)
    zr = w[:, None] * f[None, :]
    feats = jnp.concatenate([t, jnp.cos(zr), -jnp.sin(zr),
                             jnp.zeros((seq, FILT_EMB_PAD - FILT_EMB), F32)], axis=-1)
    target = 1e-2
    decay = jnp.abs(jnp.linspace(math.log(target) / 1.5, math.log(target) / 0.3, HY_W,
                                 dtype=F32))[None, :]
    w1p = jnp.pad(w1, ((0, 0), (0, FILT_EMB_PAD - FILT_EMB), (0, 0)))
    vec = lambda a: a.reshape(DEPTH, 1, FILT_HIDDEN)
    return pl.pallas_call(
        _filter_kernel,
        out_shape=jax.ShapeDtypeStruct((DEPTH, seq, 2 * HY_W), BF16),
        grid=(DEPTH,),
        in_specs=[
            pl.BlockSpec((seq, FILT_EMB_PAD), lambda l: (0, 0)),
            pl.BlockSpec((seq, 1), lambda l: (0, 0)),
            pl.BlockSpec((1, HY_W), lambda l: (0, 0)),
            pl.BlockSpec((None, FILT_EMB_PAD, FILT_HIDDEN), lambda l: (l, 0, 0)),
            pl.BlockSpec((None, 1, FILT_HIDDEN), lambda l: (l, 0, 0)),
            pl.BlockSpec((None, 1, FILT_HIDDEN), lambda l: (l, 0, 0)),
            pl.BlockSpec((None, FILT_HIDDEN, FILT_HIDDEN), lambda l: (l, 0, 0)),
            pl.BlockSpec((None, 1, FILT_HIDDEN), lambda l: (l, 0, 0)),
            pl.BlockSpec((None, 1, FILT_HIDDEN), lambda l: (l, 0, 0)),
            pl.BlockSpec((None, FILT_HIDDEN, 2 * HY_W), lambda l: (l, 0, 0)),
        ],
        out_specs=pl.BlockSpec((None, seq, 2 * HY_W), lambda l: (l, 0, 0)),
        compiler_params=_params("parallel"),
        name=f"hyena_filter_{seq}",
    )(feats, t, decay, w1p, vec(b1), vec(f1), w2, vec(b2), vec(f2), w3)


LONGCONV_RADIX = {SEQ: 2, DEC_SEQ: 4}
LONGCONV_SEQS_PER_STEP = {SEQ: 4, DEC_SEQ: 1}
CHAINS_IN_FLIGHT = 2
TABLE_ROWS = 16
LANES = 128


def _freq_classes(radix):
    return [(k, m) for k in range(radix // 2) for m in range(2)]


def _rot_power(k, m, r):
    return (3 * k * r + 2 * m * r) % 4


def _add_rotated(acc, z, p):
    zr, zi = z
    re, im, s_re, s_im = ((zr, zi, 1, 1), (zi, zr, -1, 1), (zr, zi, -1, -1), (zi, zr, 1, -1))[p]
    if acc is None:
        return (re if s_re > 0 else -re, im if s_im > 0 else -im)
    return (acc[0] + re if s_re > 0 else acc[0] - re, acc[1] + im if s_im > 0 else acc[1] - im)


def _class_spectra(parts, radix):
    out = []
    for k, m in _freq_classes(radix):
        acc = None
        for r in range(radix):
            acc = _add_rotated(acc, parts[r], _rot_power(k, m, r))
        out.append((acc[0], -acc[1]) if m else acc)
    return out


def _phase(a, b, period):
    k = (a * b) & (period - 1)
    ang = k.astype(F32) * (2.0 * math.pi / period)
    return jnp.cos(ang), jnp.sin(ang)


def _table_kernel(o_ref, *, n, period, row_mult, row_add, col_mult, col_add, stack_rows, scale):
    groups = n // TABLE_ROWS
    b0 = col_mult * lax.broadcasted_iota(jnp.int32, (TABLE_ROWS, n), 1) + col_add
    a0 = row_mult * lax.broadcasted_iota(jnp.int32, (TABLE_ROWS, n), 0) + row_add
    cos_a, sin_a = _phase(a0, b0, period)
    b1 = col_mult * lax.broadcasted_iota(jnp.int32, (groups, n), 1) + col_add
    a1 = row_mult * TABLE_ROWS * lax.broadcasted_iota(jnp.int32, (groups, n), 0)
    cos_b, sin_b = _phase(a1, b1, period)
    cos = cos_a[None] * cos_b[:, None, :] - sin_a[None] * sin_b[:, None, :]
    msin = -(sin_a[None] * cos_b[:, None, :] + cos_a[None] * sin_b[:, None, :])
    cos = (cos.reshape(n, n) * scale).astype(BF16)
    msin = (msin.reshape(n, n) * scale).astype(BF16)
    if stack_rows:
        o_ref[0:n, :] = cos
        o_ref[n:2 * n, :] = msin
    else:
        o_ref[:, 0:n] = cos
        o_ref[:, n:2 * n] = msin


def _dft_tables(seq):
    radix = LONGCONV_RADIX[seq]
    n = seq // radix

    def table(stack_rows, **kw):
        shape = (2 * n, n) if stack_rows else (n, 2 * n)
        return pl.pallas_call(
            functools.partial(_table_kernel, n=n, period=4 * seq, stack_rows=stack_rows, **kw),
            out_shape=jax.ShapeDtypeStruct(shape, BF16),
            compiler_params=_params(),
            name=f"dft_table_{seq}",
        )()

    fwd = [table(True, row_mult=2, row_add=1, col_mult=radix, col_add=r, scale=1.0)
           for r in range(radix)]
    inv = [table(False, row_mult=radix, row_add=r, col_mult=2, col_add=1, scale=1.0 / seq)
           for r in range(radix)]
    return fwd, inv


def _split_rows(x, nat_ref, radix):
    n = x.shape[0] // radix
    k = x.shape[1] // LANES
    for i in range(k):
        nat_ref[i] = x[:, LANES * i:LANES * (i + 1)]
    return [jnp.concatenate([nat_ref[i, pl.ds(r, n, stride=radix), :] for i in range(k)], axis=1)
            for r in range(radix)]


def _interleave_rows(parts, nat_ref):
    radix = len(parts)
    n = parts[0].shape[0]
    k = parts[0].shape[1] // LANES
    for i in range(k):
        for r in range(radix):
            nat_ref[i, pl.ds(r, n, stride=radix), :] = parts[r][:, LANES * i:LANES * (i + 1)]
    return jnp.concatenate([nat_ref[i] for i in range(k)], axis=1)


def _transform_parts(fwd_refs, parts, n):
    out = []
    for tab_ref, x in zip(fwd_refs, parts):
        p = _dot(tab_ref[...], x.astype(BF16))
        out.append((p[0:n], p[n:2 * n]))
    return out


def _spectrum_kernel(*refs, radix, n):
    fwd_refs, (h_ref, g_ref, nat_ref) = refs[0:radix], refs[radix:]
    parts = _split_rows(h_ref[...].astype(F32), nat_ref, radix)
    w = HY_W
    sum_spec = _class_spectra(_transform_parts(fwd_refs, [x[:, 0:w] for x in parts], n), radix)
    diff_spec = _class_spectra(_transform_parts(fwd_refs, [x[:, w:2 * w] for x in parts], n), radix)
    for c in range(radix):
        g_ref[2 * c] = sum_spec[c][0]
        g_ref[2 * c + 1] = diff_spec[c][1]


def _filter_spectrum(seq, fwd, hs):
    radix = LONGCONV_RADIX[seq]
    n = seq // radix
    table = pl.BlockSpec((2 * n, n), lambda l: (0, 0))
    return pl.pallas_call(
        functools.partial(_spectrum_kernel, radix=radix, n=n),
        out_shape=jax.ShapeDtypeStruct((DEPTH, 2 * radix, n, HY_W), F32),
        grid=(DEPTH,),
        in_specs=[*([table] * radix), pl.BlockSpec((None, seq, 2 * HY_W), lambda l: (l, 0, 0))],
        out_specs=pl.BlockSpec((None, 2 * radix, n, HY_W), lambda l: (l, 0, 0, 0)),
        scratch_shapes=[pltpu.VMEM((2 * HY_W // LANES, seq, LANES), F32)],
        compiler_params=_params("parallel"),
        name=f"hyena_spectrum_{seq}",
    )(*fwd, hs)


def _longconv_kernel(*refs, radix, n, n_seq):
    v_ref = refs[0]
    fwd_refs = refs[1:1 + radix]
    inv_refs = refs[1 + radix:1 + 2 * radix]
    g_ref, bias_ref, o_ref, vin_ref, out_ref = refs[1 + 2 * radix:]
    seq = radix * n
    classes = _freq_classes(radix)

    def forward(s, c, slot):
        v = v_ref[seq * s:seq * (s + 1), MXU_COLS * c:MXU_COLS * (c + 1)].astype(F32)
        return (_transform_parts(fwd_refs, _split_rows(v, vin_ref.at[slot], radix), n),)

    def finish(s, c, slot, parts):
        rows = slice(seq * s, seq * (s + 1))
        cs = slice(MXU_COLS * c, MXU_COLS * (c + 1))
        prods = []
        for i, (ur, ui) in enumerate(_class_spectra(parts, radix)):
            gr, gi = g_ref[2 * i, :, cs], g_ref[2 * i + 1, :, cs]
            yr, yi = ur * gr - ui * gi, ur * gi + ui * gr
            prods.append((yr, -yi) if classes[i][1] else (yr, yi))
        outs = []
        for r in range(radix):
            acc = None
            for (k, m), y in zip(classes, prods):
                acc = _add_rotated(acc, y, (-_rot_power(k, m, r)) % 4)
            spec = jnp.concatenate(acc, axis=0).astype(BF16)
            outs.append(_dot(inv_refs[r][...], spec))
        y = _interleave_rows(outs, out_ref.at[slot])
        v = jnp.concatenate([vin_ref[slot, i] for i in range(MXU_COLS // LANES)], axis=1)
        o_ref[rows, cs] = (y + bias_ref[:, cs] * v).astype(BF16)

    chains = [(s, c) for s in range(n_seq) for c in range(HY_W // MXU_COLS)]
    _run_chains([(s, c, i % CHAINS_IN_FLIGHT) for i, (s, c) in enumerate(chains)], forward, finish)


def _long_conv(seq, n_rows, row0, vh, tables, spectrum, hy_bias, layer):
    radix = LONGCONV_RADIX[seq]
    n = seq // radix
    n_seq = LONGCONV_SEQS_PER_STEP[seq]
    t = n_seq * seq
    fwd, inv = tables
    scratch = pltpu.VMEM((CHAINS_IN_FLIGHT, MXU_COLS // LANES, seq, LANES), F32)
    return pl.pallas_call(
        functools.partial(_longconv_kernel, radix=radix, n=n, n_seq=n_seq),
        out_shape=jax.ShapeDtypeStruct((n_rows, HY_W), BF16),
        grid=(n_rows // t,),
        in_specs=[
            pl.BlockSpec((t, HY_W), lambda b: (row0 // t + b, 0)),
            *([pl.BlockSpec((2 * n, n), lambda b: (0, 0), **RESIDENT)] * radix),
            *([pl.BlockSpec((n, 2 * n), lambda b: (0, 0), **RESIDENT)] * radix),
            pl.BlockSpec((None, 2 * radix, n, HY_W), lambda b: (layer, 0, 0, 0), **RESIDENT),
            pl.BlockSpec((None, 1, HY_W), lambda b: (layer, 0, 0)),
        ],
        out_specs=pl.BlockSpec((t, HY_W), lambda b: (b, 0)),
        scratch_shapes=[scratch, scratch],
        compiler_params=_params("parallel"),
        name=f"hyena_longconv_{seq}",
    )(vh, *fwd, *inv, spectrum, hy_bias.reshape(DEPTH, 1, HY_W))


ONES_ROWS = 16
CTX_SEQS_PER_STEP = 4
QBLOCKS_PER_STEP = 8


def _head_rows(q, j):
    return jnp.concatenate([q[:, HEAD_DIM * hd:HEAD_DIM * (hd + 1)]
                            for hd in range(GROUP * j, GROUP * (j + 1))], axis=0)


def _lane_sink(sink_ref, layer, j, block):
    head = lax.broadcasted_iota(jnp.int32, (1, GROUP * block), 1) // block
    sink = jnp.zeros((1, GROUP * block), F32)
    for g in range(GROUP):
        sink = jnp.where(head == g, sink_ref[layer, GROUP * j + g] * LOG2_E, sink)
    return sink


def _with_ones(vt):
    return jnp.concatenate([vt, jnp.ones((ONES_ROWS, vt.shape[1]), BF16)], axis=0)


def _softmax_pv_t(parts, sink):
    m = sink
    for lg, _ in parts:
        m = jnp.maximum(m, jnp.max(lg, axis=0, keepdims=True))
    out = None
    for lg, vt in parts:
        pv = _dot(vt, jnp.exp2(lg - m).astype(BF16))
        out = pv if out is None else out + pv
    den = out[HEAD_DIM:HEAD_DIM + 1, :] + jnp.exp2(sink - m)
    return out[0:HEAD_DIM, :] / den


def _store_heads(o_ref, rows, j, block, o_t):
    for g in range(GROUP):
        hd = GROUP * j + g
        o_ref[rows, HEAD_DIM * hd:HEAD_DIM * (hd + 1)] = (
            o_t[:, block * g:block * (g + 1)].T.astype(BF16))


def _ctx_attn_kernel(sink_ref, q_ref, k_ref, vt_ref, o_ref, *, layer):
    def logits(n, j):
        rows = slice(SEQ * n, SEQ * (n + 1))
        hs = slice(HEAD_DIM * j, HEAD_DIM * (j + 1))
        return _dot_nt(k_ref[rows, hs], _head_rows(q_ref[rows, :], j)), _with_ones(vt_ref[hs, rows])

    def finish(n, j, s, vt):
        o_t = _softmax_pv_t([(s, vt)], _lane_sink(sink_ref, layer, j, SEQ))
        _store_heads(o_ref, slice(SEQ * n, SEQ * (n + 1)), j, SEQ, o_t)

    _run_chains([(n, j) for n in range(CTX_SEQS_PER_STEP) for j in range(N_KV_HEADS)],
                logits, finish)


def _ctx_attention(sink, q, k, vt, layer):
    t = CTX_SEQS_PER_STEP * SEQ
    row = lambda b: (b, 0)
    return pl.pallas_call(
        functools.partial(_ctx_attn_kernel, layer=layer),
        out_shape=jax.ShapeDtypeStruct((N_CTX, ATTN_W), BF16),
        grid=(N_CTX // t,),
        in_specs=[
            pl.BlockSpec(memory_space=pltpu.SMEM),
            pl.BlockSpec((t, ATTN_W), row),
            pl.BlockSpec((t, KV_W), row),
            pl.BlockSpec((KV_W, t), lambda b: (0, b)),
        ],
        out_specs=pl.BlockSpec((t, ATTN_W), row),
        compiler_params=_params("parallel"),
        name="context_attention",
    )(sink, q, k, vt)


def _band_bias():
    kpos = jnp.arange(BAND)[None, :, None] - jnp.arange(3)[:, None, None] * WINDOW
    qpos = (jnp.arange(GROUP * QBLOCK) & (QBLOCK - 1))[None, None, :]
    return jnp.where(jnp.abs(qpos - kpos) <= WINDOW, 0.0, MASK_VALUE).astype(F32)


def _lat_attn_kernel(sink_ref, q_ref, k_ref, vt_ref, ck_ref, cv_ref, bias_ref, o_ref,
                     kc_ref, vct_ref, *, layer):
    i = pl.program_id(1)

    @pl.when(i == 0)
    def _():
        kc_ref[...] = ck_ref[...].astype(BF16)
        vct = cv_ref[...].T.astype(BF16)
        for j in range(N_KV_HEADS):
            vct_ref[j] = _with_ones(vct[HEAD_DIM * j:HEAD_DIM * (j + 1), :])

    def logits(n, j):
        start = (i * QBLOCKS_PER_STEP + n) * QBLOCK
        ws = pl.multiple_of(jnp.clip(start - WINDOW, 0, DEC_SEQ - BAND), QBLOCK)
        placement = jnp.where(start < WINDOW, 0, jnp.where(start + QBLOCK + WINDOW > DEC_SEQ, 2, 1))
        q4 = _head_rows(q_ref[QBLOCK * n:QBLOCK * (n + 1), :], j)
        hs = slice(HEAD_DIM * j, HEAD_DIM * (j + 1))
        s_ctx = _dot_nt(kc_ref[:, hs], q4)
        s_band = _dot_nt(k_ref[pl.ds(ws, BAND), hs], q4) + bias_ref[placement]
        return s_ctx, s_band, _with_ones(vt_ref[hs, pl.ds(ws, BAND)])

    def finish(n, j, s_ctx, s_band, vbt):
        o_t = _softmax_pv_t([(s_ctx, vct_ref[j]), (s_band, vbt)],
                            _lane_sink(sink_ref, layer, j, QBLOCK))
        _store_heads(o_ref, slice(QBLOCK * n, QBLOCK * (n + 1)), j, QBLOCK, o_t)

    _run_chains([(n, j) for n in range(QBLOCKS_PER_STEP) for j in range(N_KV_HEADS)],
                logits, finish, ahead=2)


def _lat_attention(sink, q, k, vt, cache_k, cache_v, band_bias, layer):
    t = QBLOCKS_PER_STEP * QBLOCK
    steps = DEC_SEQ // t
    seq_block = N_CTX // DEC_SEQ
    return pl.pallas_call(
        functools.partial(_lat_attn_kernel, layer=layer),
        out_shape=jax.ShapeDtypeStruct((N_LAT, ATTN_W), BF16),
        grid=(DEC_BATCH, steps),
        in_specs=[
            pl.BlockSpec(memory_space=pltpu.SMEM),
            pl.BlockSpec((t, ATTN_W), lambda b, i: (N_CTX // t + b * steps + i, 0)),
            pl.BlockSpec((DEC_SEQ, KV_W), lambda b, i: (seq_block + b, 0)),
            pl.BlockSpec((KV_W, DEC_SEQ), lambda b, i: (0, seq_block + b)),
            pl.BlockSpec((None, None, PAST_LEN, KV_W), lambda b, i: (b, layer, 0, 0)),
            pl.BlockSpec((None, None, PAST_LEN, KV_W), lambda b, i: (b, layer, 0, 0)),
            pl.BlockSpec((3, BAND, GROUP * QBLOCK), lambda b, i: (0, 0, 0)),
        ],
        out_specs=pl.BlockSpec((t, ATTN_W), lambda b, i: (b * steps + i, 0)),
        scratch_shapes=[pltpu.VMEM((PAST_LEN, KV_W), BF16),
                        pltpu.VMEM((N_KV_HEADS, HEAD_DIM + ONES_ROWS, PAST_LEN), BF16)],
        compiler_params=_params("parallel", "arbitrary"),
        name="latent_attention",
    )(sink, q, k, vt, cache_k, cache_v, band_bias)


N_GATE_CHUNKS = D_MODEL // MXU_COLS


def _merge_kernel(h_ref, atc_ref, atl_ref, ybc_ref, ybl_ref, x0_ref, x_ref, mod_ref, *rest):
    wga_refs = rest[0:N_GATE_CHUNKS]
    wgb_refs = rest[N_GATE_CHUNKS:2 * N_GATE_CHUNKS]
    wpa_ref, wpb_ref, wo_ref, g_ref, b_ref, xo_ref, ho_ref, mg_ref = rest[2 * N_GATE_CHUNKS:]
    is_ctx = pl.program_id(0) < N_CTX // ROW_TILE
    g1 = mod_ref[:, 2 * D_MODEL:3 * D_MODEL]
    sh2 = mod_ref[:, 3 * D_MODEL:4 * D_MODEL]
    sc2 = mod_ref[:, 4 * D_MODEL:5 * D_MODEL]

    def merge_branches(s):
        rs = slice(SUB_ROWS * s, SUB_ROWS * (s + 1))
        h = h_ref[rs, :]
        attn = jnp.where(is_ctx, atc_ref[rs, :], atl_ref[rs, :])
        conv = jnp.where(is_ctx, ybc_ref[rs, :], ybl_ref[rs, :])
        hy = (x0_ref[rs, :].astype(F32) * conv.astype(F32)).astype(BF16)
        for j in range(N_GATE_CHUNKS):
            cs = slice(MXU_COLS * j, MXU_COLS * (j + 1))
            ga = jax.nn.sigmoid(_dot(h, wga_refs[j][...]))
            gb = jax.nn.sigmoid(_dot(h, wgb_refs[j][...]))
            mg_ref[rs, cs] = (ga * _dot(attn, wpa_ref[:, cs])
                              + gb * _dot(hy, wpb_ref[:, cs])).astype(BF16)
        return ()

    def project_norm(s):
        rs = slice(SUB_ROWS * s, SUB_ROWS * (s + 1))
        sub = _dot(mg_ref[rs, :], wo_ref[...])
        x = _layer_norm(DEEPNORM_ALPHA * x_ref[rs, :] + g1 * sub, g_ref[...], b_ref[...])
        xo_ref[rs, :] = x
        ho_ref[rs, :] = (x * (1.0 + sc2) + sh2).astype(BF16)

    _run_chains([(s,) for s in range(ROW_TILE // SUB_ROWS)], merge_branches, project_norm)


def _merge(h, atc, atl, ybc, ybl, x0, x, mods, w_in, wpa, wpb, wo, ln_g, ln_b, layer):
    t = ROW_TILE
    nct = N_CTX // t
    row = lambda m: (m, 0)
    ctx_row = lambda m: (jnp.minimum(m, nct - 1), 0)
    lat_row = lambda m: (jnp.maximum(m - nct, 0), 0)
    gate_block0 = GATE_COL0 // MXU_COLS
    gate_specs = [_weight_spec((D_MODEL, MXU_COLS), (0, gate_block0 + j))
                  for j in range(2 * N_GATE_CHUNKS)]
    return pl.pallas_call(
        _merge_kernel,
        out_shape=(jax.ShapeDtypeStruct((N_TOK, D_MODEL), F32),
                   jax.ShapeDtypeStruct((N_TOK, D_MODEL), BF16)),
        grid=(N_TOK // t,),
        in_specs=[
            pl.BlockSpec((t, D_MODEL), row),
            pl.BlockSpec((t, ATTN_W), ctx_row),
            pl.BlockSpec((t, ATTN_W), lat_row),
            pl.BlockSpec((t, HY_W), ctx_row),
            pl.BlockSpec((t, HY_W), lat_row),
            pl.BlockSpec((t, HY_W), row),
            pl.BlockSpec((t, D_MODEL), row),
            _mod_spec(layer, t),
            *gate_specs,
            _weight_spec((ATTN_W, D_MODEL)),
            _weight_spec((HY_W, D_MODEL)),
            _weight_spec((D_MODEL, D_MODEL)),
            _layer_spec((1, D_MODEL), layer),
            _layer_spec((1, D_MODEL), layer),
        ],
        out_specs=(pl.BlockSpec((t, D_MODEL), row), pl.BlockSpec((t, D_MODEL), row)),
        scratch_shapes=[pltpu.VMEM((t, D_MODEL), BF16)],
        compiler_params=_params("parallel"),
        name="merge_ln1",
    )(h, atc, atl, ybc, ybl, x0, x, mods, *([w_in] * (2 * N_GATE_CHUNKS)), wpa, wpb, wo,
      ln_g, ln_b)


def _ffn_kernel(h_ref, hp_ref, hn_ref, x_ref, mod_ref, nmod_ref, wg_ref, wv_ref, wd_ref,
                cw_ref, cb_ref, g_ref, b_ref, *rest, last_layer):
    if last_layer:
        yp_ref, ys_ref, hx_ref, a_ref, xo_ref = rest
    else:
        xo_ref, ho_ref, hx_ref, a_ref = rest
    n_sub = ROW_TILE // SUB_ROWS
    ext = SUB_ROWS + 2 * HALO
    is_ctx, seq_start, seq_end = _tile_edges(pl.program_id(0), ROW_TILE)
    _fill_extended(hx_ref, hp_ref, h_ref, hn_ref)
    g2 = mod_ref[:, 5 * D_MODEL:6 * D_MODEL]
    sh = nmod_ref[:, 0:D_MODEL]
    sc = nmod_ref[:, D_MODEL:2 * D_MODEL]

    def up_gate(j):
        rs = slice(SUB_ROWS * j, SUB_ROWS * (j + 1))
        zero_top, zero_bot = _sub_edges(j, n_sub, is_ctx, seq_start, seq_end)
        hx = hx_ref[SUB_ROWS * j:SUB_ROWS * j + ext, :]
        hm = h_ref[rs, :]
        for c in range(D_FF // MXU_COLS):
            cs = slice(MXU_COLS * c, MXU_COLS * (c + 1))
            gate = _dwconv3(_dot(hx, wg_ref[:, cs]), zero_top, zero_bot, cw_ref, cb_ref, cs)
            gelu = 0.5 * gate * (1.0 + lax.erf(gate * math.sqrt(0.5)))
            a_ref[rs, cs] = (gelu * _dot(hm, wv_ref[:, cs])).astype(BF16)
        return ()

    def down_norm(j):
        rs = slice(SUB_ROWS * j, SUB_ROWS * (j + 1))
        sub = _dot(a_ref[rs, :], wd_ref[...])
        x = _layer_norm(DEEPNORM_ALPHA * x_ref[rs, :] + g2 * sub, g_ref[...], b_ref[...])
        xo_ref[rs, :] = x
        if not last_layer:
            ho_ref[rs, :] = (x * (1.0 + sc) + sh).astype(BF16)

    _run_chains([(j,) for j in range(n_sub)], up_gate, down_norm)

    if last_layer:
        @pl.when(is_ctx)
        def _():
            yp_ref[...] = xo_ref[...]

        @pl.when(jnp.logical_not(is_ctx))
        def _():
            ys_ref[...] = xo_ref[...]


def _ffn(h, x, mods, w_up, w_down, conv_w, conv_b, ln_g, ln_b, layer):
    t = ROW_TILE
    nct = N_CTX // t
    row = lambda m: (m, 0)
    last_layer = layer == DEPTH - 1
    if last_layer:
        out_shape = (jax.ShapeDtypeStruct((N_CTX, D_MODEL), F32),
                     jax.ShapeDtypeStruct((N_LAT, D_MODEL), F32))
        out_specs = (pl.BlockSpec((t, D_MODEL), lambda m: (jnp.minimum(m, nct - 1), 0)),
                     pl.BlockSpec((t, D_MODEL), lambda m: (jnp.maximum(m - nct, 0), 0)))
        scratch = [pltpu.VMEM((t, D_MODEL), F32)]
    else:
        out_shape = (jax.ShapeDtypeStruct((N_TOK, D_MODEL), F32),
                     jax.ShapeDtypeStruct((N_TOK, D_MODEL), BF16))
        out_specs = (pl.BlockSpec((t, D_MODEL), row), pl.BlockSpec((t, D_MODEL), row))
        scratch = []
    return pl.pallas_call(
        functools.partial(_ffn_kernel, last_layer=last_layer),
        out_shape=out_shape,
        grid=(N_TOK // t,),
        in_specs=[
            pl.BlockSpec((t, D_MODEL), row),
            *_halo_specs(t, D_MODEL),
            pl.BlockSpec((t, D_MODEL), row),
            _mod_spec(layer, t),
            _mod_spec(min(layer + 1, DEPTH - 1), t),
            _weight_spec((D_MODEL, D_FF), (0, 0)),
            _weight_spec((D_MODEL, D_FF), (0, 1)),
            _weight_spec((D_FF, D_MODEL)),
            _layer_spec((3, D_FF), layer),
            _layer_spec((1, D_FF), layer),
            _layer_spec((1, D_MODEL), layer),
            _layer_spec((1, D_MODEL), layer),
        ],
        out_specs=out_specs,
        scratch_shapes=[pltpu.VMEM((t + 2 * HALO, D_MODEL), BF16), pltpu.VMEM((t, D_FF), BF16),
                        *scratch],
        compiler_params=_params("arbitrary"),
        name="conv_ffn_ln2",
    )(h, h, h, x, mods, mods, w_up, w_up, w_down, conv_w, conv_b, ln_g, ln_b)


def kernel(x_prompt, x_sample, cache_k, cache_v, c, c_ctx, w_ada, b_ada, w_in, attn_sink,
           hy_conv_w, hy_conv_b, filt_w1, filt_b1, filt_freq1, filt_w2, filt_b2, filt_freq2,
           filt_w3, hy_bias, w_pa, w_pb, w_out, ln1_g, ln1_b, w_up, ffn_conv_w, ffn_conv_b,
           w_down, ln2_g, ln2_b):
    cond = jnp.concatenate([c_ctx[None, :], c,
                            jnp.zeros((N_MOD_ROWS - 1 - DEC_BATCH, D_MODEL), F32)], axis=0)
    mods = _modulation(cond, w_ada, b_ada).reshape(DEPTH, N_MOD_ROWS, 1, 6 * D_MODEL)

    x, h = _prologue(x_prompt.reshape(N_CTX, D_MODEL), x_sample.reshape(N_LAT, D_MODEL), mods)

    cos_tab, sin_tab = _rope_tables()
    conv_args = {}
    for seq in (SEQ, DEC_SEQ):
        tables = _dft_tables(seq)
        hs = _filters(seq, filt_w1, filt_b1, filt_freq1, filt_w2, filt_b2, filt_freq2, filt_w3)
        conv_args[seq] = (tables, _filter_spectrum(seq, tables[0], hs))

    cache_k = cache_k.reshape(DEC_BATCH, DEPTH, PAST_LEN, KV_W)
    cache_v = cache_v.reshape(DEC_BATCH, DEPTH, PAST_LEN, KV_W)
    hy_conv_b3 = hy_conv_b.reshape(DEPTH, 1, 3 * HY_W)
    ffn_conv_b3 = ffn_conv_b.reshape(DEPTH, 1, D_FF)
    ln1_g3, ln1_b3 = ln1_g.reshape(DEPTH, 1, D_MODEL), ln1_b.reshape(DEPTH, 1, D_MODEL)
    ln2_g3, ln2_b3 = ln2_g.reshape(DEPTH, 1, D_MODEL), ln2_b.reshape(DEPTH, 1, D_MODEL)

    band_bias = _band_bias()
    w_in_l = w_in[0:1].astype(BF16)
    keys, values = [], []
    for l in range(DEPTH):
        to_cast = [(w, l) for w in (w_pa, w_pb, w_out, w_up, w_down)]
        if l + 1 < DEPTH:
            to_cast.append((w_in, l + 1))
        q, k, vt, kf, vf, x0, vh, w_pa_l, w_pb_l, w_out_l, w_up_l, w_down_l, *w_in_next = _in_proj(
            h, w_in_l, cos_tab, sin_tab, hy_conv_w, hy_conv_b3, to_cast, l)
        keys.append(kf[:N_CTX])
        values.append(vf[:N_CTX])

        ybc = _long_conv(SEQ, N_CTX, 0, vh, *conv_args[SEQ], hy_bias, l)
        ybl = _long_conv(DEC_SEQ, N_LAT, N_CTX, vh, *conv_args[DEC_SEQ], hy_bias, l)
        atc = _ctx_attention(attn_sink, q, k, vt, l)
        atl = _lat_attention(attn_sink, q, k, vt, cache_k, cache_v, band_bias, l)

        x, h2 = _merge(h, atc, atl, ybc, ybl, x0, x, mods, w_in_l, w_pa_l, w_pb_l, w_out_l,
                       ln1_g3, ln1_b3, l)
        x, h = _ffn(h2, x, mods, w_up_l, w_down_l, ffn_conv_w, ffn_conv_b3, ln2_g3, ln2_b3, l)
        if w_in_next:
            w_in_l = w_in_next[0]

    y_prompt_rows, y_sample_rows = x, h
    y_prompt = y_prompt_rows.reshape(BATCH, SEQ, D_MODEL)
    y_sample = y_sample_rows.reshape(DEC_BATCH, DEC_SEQ, D_MODEL)
    to_cache = lambda ts: jnp.stack(ts, axis=1).reshape(BATCH, DEPTH, SEQ, N_KV_HEADS, HEAD_DIM)
    new_k = to_cache([t.reshape(BATCH, SEQ, KV_W) for t in keys])
    new_v = to_cache([t.reshape(BATCH, SEQ, KV_W) for t in values])
    return (y_prompt, y_sample, new_k, new_v)
```

```python
import functools
import math

import jax
import jax.numpy as jnp
from jax import lax
from jax.experimental import pallas as pl
from jax.experimental.pallas import tpu as pltpu

D_MODEL = 1024
BATCH = 16
SEQ = 256
DEPTH = 4
DEC_BATCH = 4
DEC_SEQ = 2048
PAST_LEN = 512
GRID_W = 64
HEAD_DIM = 64
N_HEADS = 8
N_KV_HEADS = 2
GROUP = N_HEADS // N_KV_HEADS
ATTN_W = N_HEADS * HEAD_DIM
KV_W = N_KV_HEADS * HEAD_DIM
WINDOW = 128
QBLOCK = 128
HY_W = 512
FILT_EMB = 33
FILT_EMB_PAD = 40
FILT_BANDS = (FILT_EMB - 1) // 2
FILT_HIDDEN = 64
D_FF = 2816
ROPE_BASE = 10000.0
LN_EPS = 1e-5
DEEPNORM_ALPHA = (2 * DEPTH) ** 0.25

N_CTX = BATCH * SEQ
N_LAT = DEC_BATCH * DEC_SEQ
N_TOK = N_CTX + N_LAT
N_MOD_ROWS = 8
QKV_W = ATTN_W + 2 * KV_W
HY_BLOCK = QKV_W
assert 3 * HY_W == 2 * HY_BLOCK
GATE_COL0 = QKV_W + 3 * HY_W
BAND = QBLOCK + 2 * WINDOW
MASK_VALUE = -1e30
LOG2_E = math.log2(math.e)

F32 = jnp.float32
BF16 = jnp.bfloat16

VMEM_LIMIT_BYTES = 56 * 1024 * 1024
MXU_COLS = 256
ROW_TILE = 1024
SUB_ROWS = SEQ
HALO = 16
RESIDENT = dict(pipeline_mode=pl.Buffered(1))
CAST_BLOCKS = 8


def _params(*semantics):
    return pltpu.CompilerParams(dimension_semantics=semantics, vmem_limit_bytes=VMEM_LIMIT_BYTES)


def _dot(a, b):
    return jnp.dot(a, b, preferred_element_type=F32)


def _dot_split(a, b):
    a_hi = a.astype(BF16)
    b_hi = b.astype(BF16)
    a_lo = (a - a_hi.astype(F32)).astype(BF16)
    b_lo = (b - b_hi.astype(F32)).astype(BF16)
    return _dot(a_hi, b_hi) + (_dot(a_hi, b_lo) + _dot(a_lo, b_hi))


def _dot_nt(a, b):
    return lax.dot_general(a, b, (((1,), (1,)), ((), ())), preferred_element_type=F32)


def _mod_row(tile_rows):
    n_ctx_tiles = N_CTX // tile_rows
    tiles_per_seq = DEC_SEQ // tile_rows

    def fn(m):
        return jnp.where(m < n_ctx_tiles, 0, 1 + (m - n_ctx_tiles) // tiles_per_seq)

    return fn


def _mod_spec(layer, tile_rows):
    grp = _mod_row(tile_rows)
    return pl.BlockSpec((None, None, 1, 6 * D_MODEL), lambda m, *_: (layer, grp(m), 0, 0))


def _layer_spec(shape, layer):
    zeros = (0,) * len(shape)
    return pl.BlockSpec((None,) + tuple(shape), lambda *_: (layer,) + zeros)


def _weight_spec(shape, block_index=None):
    block_index = block_index or (0,) * len(shape)
    return pl.BlockSpec((None,) + tuple(shape), lambda *_: (0,) + tuple(block_index), **RESIDENT)


def _layer_norm(y, g, b):
    mu = jnp.mean(y, axis=-1, keepdims=True)
    yc = y - mu
    var = jnp.mean(yc * yc, axis=-1, keepdims=True)
    return yc * lax.rsqrt(var + LN_EPS) * g + b


def _run_chains(chains, first_fn, finish_fn, ahead=1):
    pending = [first_fn(*chain) for chain in chains[:ahead]]
    for c, chain in enumerate(chains):
        if c + ahead < len(chains):
            pending.append(first_fn(*chains[c + ahead]))
        finish_fn(*chain, *pending.pop(0))


def _tile_edges(m, tile_rows):
    n_ctx_tiles = N_CTX // tile_rows
    tiles_per_seq = DEC_SEQ // tile_rows
    is_ctx = m < n_ctx_tiles
    lat_pos = (m - n_ctx_tiles) % tiles_per_seq
    return (is_ctx, jnp.logical_or(is_ctx, lat_pos == 0),
            jnp.logical_or(is_ctx, lat_pos == tiles_per_seq - 1))


def _halo_specs(tile_rows, n_cols):
    tb = tile_rows // HALO
    last = N_TOK // HALO - 1
    return (pl.BlockSpec((HALO, n_cols), lambda m: (jnp.maximum(m * tb - 1, 0), 0)),
            pl.BlockSpec((HALO, n_cols), lambda m: (jnp.minimum((m + 1) * tb, last), 0)))


def _fill_extended(hx_ref, hp_ref, h_ref, hn_ref):
    rows = h_ref.shape[0]
    hx_ref[0:HALO, :] = hp_ref[...]
    hx_ref[HALO:HALO + rows, :] = h_ref[...]
    hx_ref[HALO + rows:2 * HALO + rows, :] = hn_ref[...]


def _sub_edges(j, n_sub, is_ctx, seq_start, seq_end):
    return (seq_start if j == 0 else is_ctx), (seq_end if j == n_sub - 1 else is_ctx)


def _dwconv3(z, zero_top, zero_bot, w_ref, b_ref, cs):
    n = z.shape[0] - 2 * HALO
    top = jnp.where(zero_top, 0.0, z[HALO - 8:HALO])
    bot = jnp.where(zero_bot, 0.0, z[HALO + n:HALO + n + 8])
    z = jnp.concatenate([z[0:HALO - 8], top, z[HALO:HALO + n], bot, z[HALO + n + 8:]], axis=0)
    prev = pltpu.roll(z, 1, 0)[HALO:HALO + n]
    nxt = pltpu.roll(z, z.shape[0] - 1, 0)[HALO:HALO + n]
    return (prev * w_ref[0:1, cs] + z[HALO:HALO + n] * w_ref[1:2, cs] + nxt * w_ref[2:3, cs]
            + b_ref[:, cs])


def _mod_kernel(cond_ref, w_ref, b_ref, o_ref):
    c = cond_ref[...]
    s = (c * jax.nn.sigmoid(c)).astype(BF16)
    o_ref[...] = _dot(s, w_ref[...].astype(BF16)) + b_ref[...]


def _modulation(cond, w_ada, b_ada):
    n_col = 6 * D_MODEL // D_MODEL
    return pl.pallas_call(
        _mod_kernel,
        out_shape=jax.ShapeDtypeStruct((DEPTH, N_MOD_ROWS, 6 * D_MODEL), F32),
        grid=(DEPTH, n_col),
        in_specs=[
            pl.BlockSpec((N_MOD_ROWS, D_MODEL), lambda l, j: (0, 0)),
            pl.BlockSpec((None, D_MODEL, D_MODEL), lambda l, j: (l, 0, j)),
            pl.BlockSpec((None, 1, D_MODEL), lambda l, j: (l, 0, j)),
        ],
        out_specs=pl.BlockSpec((None, N_MOD_ROWS, D_MODEL), lambda l, j: (l, 0, j)),
        compiler_params=_params("parallel", "parallel"),
        name="modulation",
    )(cond, w_ada, b_ada.reshape(DEPTH, 1, 6 * D_MODEL))


PRO_TILE = 1024


def _prologue_kernel(xp_ref, xs_ref, mod_ref, h_ref):
    m = pl.program_id(0)

    def emit(x):
        sh = mod_ref[:, 0:D_MODEL]
        sc = mod_ref[:, D_MODEL:2 * D_MODEL]
        h_ref[...] = (x * (1.0 + sc) + sh).astype(BF16)

    @pl.when(m < N_CTX // PRO_TILE)
    def _():
        emit(xp_ref[...])

    @pl.when(m >= N_CTX // PRO_TILE)
    def _():
        emit(xs_ref[...])


def _split_stream_specs(tile_rows):
    nct = N_CTX // tile_rows
    return (pl.BlockSpec((tile_rows, D_MODEL), lambda m: (jnp.minimum(m, nct - 1), 0)),
            pl.BlockSpec((tile_rows, D_MODEL), lambda m: (jnp.maximum(m - nct, 0), 0)))


def _prologue(xp, xs, mods):
    return pl.pallas_call(
        _prologue_kernel,
        out_shape=jax.ShapeDtypeStruct((N_TOK, D_MODEL), BF16),
        grid=(N_TOK // PRO_TILE,),
        in_specs=[*_split_stream_specs(PRO_TILE), _mod_spec(0, PRO_TILE)],
        out_specs=pl.BlockSpec((PRO_TILE, D_MODEL), lambda m: (m, 0)),
        compiler_params=_params("parallel"),
        name="prologue",
    )(xp, xs, mods)


ROPE_LANES = 2 * HEAD_DIM


N_IN_PROJ_OUTS = 7


def _in_proj_kernel(h_ref, hp_ref, hn_ref, wqkv_ref, why0_ref, why1_ref, cos_ref, sin_ref,
                    cw_ref, cb_ref, *rest):
    n_cast = (len(rest) - N_IN_PROJ_OUTS - 2) // 2
    cast_src = rest[0:n_cast]
    q_ref, k_ref, vt_ref, kf_ref, vf_ref, x0_ref, vh_ref = rest[n_cast:n_cast + N_IN_PROJ_OUTS]
    cast_dst = rest[n_cast + N_IN_PROJ_OUTS:-2]
    hx_ref, kv_ref = rest[-2:]

    @pl.when(pl.program_id(0) < CAST_BLOCKS)
    def _():
        for src, dst in zip(cast_src, cast_dst):
            dst[...] = src[...].astype(BF16)

    n_sub = ROW_TILE // SUB_ROWS
    ext = SUB_ROWS + 2 * HALO
    is_ctx, seq_start, seq_end = _tile_edges(pl.program_id(0), ROW_TILE)
    _fill_extended(hx_ref, hp_ref, h_ref, hn_ref)
    lane = lax.broadcasted_iota(jnp.int32, (SUB_ROWS, ROPE_LANES), 1)
    first_half = (lane & (HEAD_DIM // 4)) == 0
    scale = HEAD_DIM ** -0.5 * LOG2_E
    for j in range(n_sub):
        rs = slice(SUB_ROWS * j, SUB_ROWS * (j + 1))
        cos = cos_ref[rs, :]
        sin = sin_ref[rs, :]

        def rope(x):
            partner = jnp.where(first_half, pltpu.roll(x, ROPE_LANES - HEAD_DIM // 4, 1),
                                pltpu.roll(x, HEAD_DIM // 4, 1))
            return x * cos + partner * sin

        hm = h_ref[rs, :]
        for jj in range(ATTN_W // MXU_COLS):
            qq = _dot(hm, wqkv_ref[:, MXU_COLS * jj:MXU_COLS * (jj + 1)])
            for i in range(MXU_COLS // ROPE_LANES):
                c0 = MXU_COLS * jj + ROPE_LANES * i
                q_ref[rs, c0:c0 + ROPE_LANES] = (
                    rope(qq[:, ROPE_LANES * i:ROPE_LANES * (i + 1)]) * scale).astype(BF16)
        kv = _dot(hm, wqkv_ref[:, ATTN_W:QKV_W])
        k = kv[:, 0:KV_W]
        v = kv[:, KV_W:2 * KV_W]
        kv_ref[0, rs, :] = k
        kv_ref[1, rs, :] = v
        k_ref[rs, :] = rope(k).astype(BF16)
        vt_ref[:, rs] = v.T.astype(BF16)

        zero_top, zero_bot = _sub_edges(j, n_sub, is_ctx, seq_start, seq_end)
        hx = hx_ref[SUB_ROWS * j:SUB_ROWS * j + ext, :]

        def proj_conv(c0):
            w_ref = why0_ref if c0 < HY_BLOCK else why1_ref
            w = w_ref[:, c0 % HY_BLOCK:c0 % HY_BLOCK + MXU_COLS]
            return _dwconv3(_dot(hx, w), zero_top, zero_bot, cw_ref, cb_ref,
                            slice(c0, c0 + MXU_COLS))

        for jj in range(HY_W // MXU_COLS):
            c0 = MXU_COLS * jj
            x0_ref[rs, c0:c0 + MXU_COLS] = proj_conv(c0).astype(BF16)
            x1 = proj_conv(HY_W + c0)
            u = proj_conv(2 * HY_W + c0)
            vh_ref[rs, c0:c0 + MXU_COLS] = (x1 * u).astype(BF16)

    @pl.when(is_ctx)
    def _():
        kf_ref[...] = kv_ref[0]
        vf_ref[...] = kv_ref[1]


def _in_proj(h, w_in, cos_tab, sin_tab, conv_w, conv_b, to_cast, layer):
    t = ROW_TILE
    nct = N_CTX // t
    tps = DEC_SEQ // t
    tab = lambda m: (jnp.where(m < nct, 0, 1 + (m - nct) % tps), 0, 0)
    row = lambda m: (m, 0)
    ctx_row = lambda m: (jnp.minimum(m, nct - 1), 0)
    bf = lambda w: jax.ShapeDtypeStruct((N_TOK, w), BF16)
    cast_block = lambda m: jnp.minimum(m, CAST_BLOCKS - 1)
    cast_in, cast_out, cast_shapes = [], [], []
    for w, w_layer in to_cast:
        rows, cols = w.shape[1] // CAST_BLOCKS, w.shape[2]
        cast_in.append(pl.BlockSpec((None, rows, cols),
                                    lambda m, w_layer=w_layer: (w_layer, cast_block(m), 0)))
        cast_out.append(pl.BlockSpec((None, rows, cols), lambda m: (0, cast_block(m), 0)))
        cast_shapes.append(jax.ShapeDtypeStruct((1,) + w.shape[1:], BF16))
    return pl.pallas_call(
        _in_proj_kernel,
        out_shape=(bf(ATTN_W), bf(KV_W),
                   jax.ShapeDtypeStruct((KV_W, N_TOK), BF16),
                   jax.ShapeDtypeStruct((N_CTX, KV_W), F32),
                   jax.ShapeDtypeStruct((N_CTX, KV_W), F32),
                   bf(HY_W), bf(HY_W), *cast_shapes),
        grid=(N_TOK // t,),
        in_specs=[
            pl.BlockSpec((t, D_MODEL), row),
            *_halo_specs(t, D_MODEL),
            _weight_spec((D_MODEL, QKV_W), (0, 0)),
            _weight_spec((D_MODEL, HY_BLOCK), (0, 1)),
            _weight_spec((D_MODEL, HY_BLOCK), (0, 2)),
            pl.BlockSpec((None, t, ROPE_LANES), tab),
            pl.BlockSpec((None, t, ROPE_LANES), tab),
            _layer_spec((3, 3 * HY_W), layer),
            _layer_spec((1, 3 * HY_W), layer),
            *cast_in,
        ],
        out_specs=(pl.BlockSpec((t, ATTN_W), row),
                   pl.BlockSpec((t, KV_W), row),
                   pl.BlockSpec((KV_W, t), lambda m: (0, m)),
                   pl.BlockSpec((t, KV_W), ctx_row),
                   pl.BlockSpec((t, KV_W), ctx_row),
                   pl.BlockSpec((t, HY_W), row),
                   pl.BlockSpec((t, HY_W), row),
                   *cast_out),
        scratch_shapes=[pltpu.VMEM((t + 2 * HALO, D_MODEL), BF16),
                        pltpu.VMEM((2, t, KV_W), F32)],
        compiler_params=_params("arbitrary"),
        name="in_proj",
    )(h, h, h, w_in, w_in, w_in, cos_tab, sin_tab, conv_w, conv_b, *(w for w, _ in to_cast))


def _rope_tables():
    rows = DEC_SEQ // GRID_W
    r, col = jnp.meshgrid(jnp.arange(rows), jnp.arange(GRID_W), indexing='ij')
    pos = jnp.stack([r.reshape(-1), col.reshape(-1)], axis=-1).astype(F32)
    half = HEAD_DIM // 2
    inv_freq = 1.0 / (ROPE_BASE ** (jnp.arange(0, half, 2, dtype=F32) / half))
    ang = pos[:, :, None] * inv_freq
    ang = jnp.stack([ang, ang], axis=-2).reshape(DEC_SEQ, HEAD_DIM)
    cos = jnp.tile(jnp.cos(ang), (1, ROPE_LANES // HEAD_DIM))
    sin = jnp.tile(jnp.sin(ang), (1, ROPE_LANES // HEAD_DIM))
    lane = jnp.arange(ROPE_LANES)
    sin = jnp.where((lane & (HEAD_DIM // 4)) == 0, -sin, sin)
    n_slab = DEC_SEQ // ROW_TILE
    cos = jnp.concatenate([jnp.ones((1, ROW_TILE, ROPE_LANES), F32),
                           cos.reshape(n_slab, ROW_TILE, ROPE_LANES)], axis=0)
    sin = jnp.concatenate([jnp.zeros((1, ROW_TILE, ROPE_LANES), F32),
                           sin.reshape(n_slab, ROW_TILE, ROPE_LANES)], axis=0)
    return cos, sin


def _filter_kernel(feat_ref, t_ref, dec_ref, w1_ref, b1_ref, f1_ref, w2_ref, b2_ref, f2_ref,
                   w3_ref, o_ref):
    a = _dot_split(feat_ref[...], w1_ref[...]) + b1_ref[...]
    a = jnp.sin(f1_ref[...] * a)
    a = _dot_split(a, w2_ref[...]) + b2_ref[...]
    a = jnp.sin(f2_ref[...] * a)
    hh = _dot_split(a, w3_ref[...])
    window = jnp.exp(-t_ref[...] * dec_ref[...])
    h_fwd = hh[:, 0:HY_W] * window
    h_bwd = hh[:, HY_W:2 * HY_W] * window
    row = lax.broadcasted_iota(jnp.int32, h_bwd.shape, 0)
    h_bwd = jnp.where(row == 0, 0.0, h_bwd)
    o_ref[:, 0:HY_W] = (h_fwd + h_bwd).astype(BF16)
    o_ref[:, HY_W:2 * HY_W] = (h_fwd - h_bwd).astype(BF16)


def _filters(seq, w1, b1, f1, w2, b2, f2, w3):
    t = jnp.linspace(0.0, 1.0, seq, dtype=F32)[:, None]
    w = 2.0 * math.pi * jnp.arange(seq, dtype=F32) / seq
    f = jnp.linspace(1e-4, FILT_BANDS - 1, FILT_BANDS, dtype=F32)
    zr = w[:, None] * f[None, :]
    feats = jnp.concatenate([t, jnp.cos(zr), -jnp.sin(zr),
                             jnp.zeros((seq, FILT_EMB_PAD - FILT_EMB), F32)], axis=-1)
    target = 1e-2
    decay = jnp.abs(jnp.linspace(math.log(target) / 1.5, math.log(target) / 0.3, HY_W,
                                 dtype=F32))[None, :]
    w1p = jnp.pad(w1, ((0, 0), (0, FILT_EMB_PAD - FILT_EMB), (0, 0)))
    vec = lambda a: a.reshape(DEPTH, 1, FILT_HIDDEN)
    return pl.pallas_call(
        _filter_kernel,
        out_shape=jax.ShapeDtypeStruct((DEPTH, seq, 2 * HY_W), BF16),
        grid=(DEPTH,),
        in_specs=[
            pl.BlockSpec((seq, FILT_EMB_PAD), lambda l: (0, 0)),
            pl.BlockSpec((seq, 1), lambda l: (0, 0)),
            pl.BlockSpec((1, HY_W), lambda l: (0, 0)),
            pl.BlockSpec((None, FILT_EMB_PAD, FILT_HIDDEN), lambda l: (l, 0, 0)),
            pl.BlockSpec((None, 1, FILT_HIDDEN), lambda l: (l, 0, 0)),
            pl.BlockSpec((None, 1, FILT_HIDDEN), lambda l: (l, 0, 0)),
            pl.BlockSpec((None, FILT_HIDDEN, FILT_HIDDEN), lambda l: (l, 0, 0)),
            pl.BlockSpec((None, 1, FILT_HIDDEN), lambda l: (l, 0, 0)),
            pl.BlockSpec((None, 1, FILT_HIDDEN), lambda l: (l, 0, 0)),
            pl.BlockSpec((None, FILT_HIDDEN, 2 * HY_W), lambda l: (l, 0, 0)),
        ],
        out_specs=pl.BlockSpec((None, seq, 2 * HY_W), lambda l: (l, 0, 0)),
        compiler_params=_params("parallel"),
        name=f"hyena_filter_{seq}",
    )(feats, t, decay, w1p, vec(b1), vec(f1), w2, vec(b2), vec(f2), w3)


LONGCONV_RADIX = {SEQ: 2, DEC_SEQ: 4}
LONGCONV_SEQS_PER_STEP = {SEQ: 4, DEC_SEQ: 1}
CHAINS_IN_FLIGHT = 2
TABLE_ROWS = 16
LANES = 128


def _freq_classes(radix):
    return [(k, m) for k in range(radix // 2) for m in range(2)]


def _rot_power(k, m, r):
    return (3 * k * r + 2 * m * r) % 4


def _add_rotated(acc, z, p):
    zr, zi = z
    re, im, s_re, s_im = ((zr, zi, 1, 1), (zi, zr, -1, 1), (zr, zi, -1, -1), (zi, zr, 1, -1))[p]
    if acc is None:
        return (re if s_re > 0 else -re, im if s_im > 0 else -im)
    return (acc[0] + re if s_re > 0 else acc[0] - re, acc[1] + im if s_im > 0 else acc[1] - im)


def _class_spectra(parts, radix):
    out = []
    for k, m in _freq_classes(radix):
        acc = None
        for r in range(radix):
            acc = _add_rotated(acc, parts[r], _rot_power(k, m, r))
        out.append((acc[0], -acc[1]) if m else acc)
    return out


def _phase(a, b, period):
    k = (a * b) & (period - 1)
    ang = k.astype(F32) * (2.0 * math.pi / period)
    return jnp.cos(ang), jnp.sin(ang)


def _table_kernel(o_ref, *, n, period, row_mult, row_add, col_mult, col_add, stack_rows, scale):
    groups = n // TABLE_ROWS
    b0 = col_mult * lax.broadcasted_iota(jnp.int32, (TABLE_ROWS, n), 1) + col_add
    a0 = row_mult * lax.broadcasted_iota(jnp.int32, (TABLE_ROWS, n), 0) + row_add
    cos_a, sin_a = _phase(a0, b0, period)
    b1 = col_mult * lax.broadcasted_iota(jnp.int32, (groups, n), 1) + col_add
    a1 = row_mult * TABLE_ROWS * lax.broadcasted_iota(jnp.int32, (groups, n), 0)
    cos_b, sin_b = _phase(a1, b1, period)
    cos = cos_a[None] * cos_b[:, None, :] - sin_a[None] * sin_b[:, None, :]
    msin = -(sin_a[None] * cos_b[:, None, :] + cos_a[None] * sin_b[:, None, :])
    cos = (cos.reshape(n, n) * scale).astype(BF16)
    msin = (msin.reshape(n, n) * scale).astype(BF16)
    if stack_rows:
        o_ref[0:n, :] = cos
        o_ref[n:2 * n, :] = msin
    else:
        o_ref[:, 0:n] = cos
        o_ref[:, n:2 * n] = msin


def _dft_tables(seq):
    radix = LONGCONV_RADIX[seq]
    n = seq // radix

    def table(stack_rows, **kw):
        shape = (2 * n, n) if stack_rows else (n, 2 * n)
        return pl.pallas_call(
            functools.partial(_table_kernel, n=n, period=4 * seq, stack_rows=stack_rows, **kw),
            out_shape=jax.ShapeDtypeStruct(shape, BF16),
            compiler_params=_params(),
            name=f"dft_table_{seq}",
        )()

    fwd = [table(True, row_mult=2, row_add=1, col_mult=radix, col_add=r, scale=1.0)
           for r in range(radix)]
    inv = [table(False, row_mult=radix, row_add=r, col_mult=2, col_add=1, scale=1.0 / seq)
           for r in range(radix)]
    return fwd, inv


def _split_rows(x, nat_ref, radix):
    n = x.shape[0] // radix
    k = x.shape[1] // LANES
    for i in range(k):
        nat_ref[i] = x[:, LANES * i:LANES * (i + 1)]
    return [jnp.concatenate([nat_ref[i, pl.ds(r, n, stride=radix), :] for i in range(k)], axis=1)
            for r in range(radix)]


def _interleave_rows(parts, nat_ref):
    radix = len(parts)
    n = parts[0].shape[0]
    k = parts[0].shape[1] // LANES
    for i in range(k):
        for r in range(radix):
            nat_ref[i, pl.ds(r, n, stride=radix), :] = parts[r][:, LANES * i:LANES * (i + 1)]
    return jnp.concatenate([nat_ref[i] for i in range(k)], axis=1)


def _transform_parts(fwd_refs, parts, n):
    out = []
    for tab_ref, x in zip(fwd_refs, parts):
        p = _dot(tab_ref[...], x.astype(BF16))
        out.append((p[0:n], p[n:2 * n]))
    return out


def _spectrum_kernel(*refs, radix, n):
    fwd_refs, (h_ref, g_ref, nat_ref) = refs[0:radix], refs[radix:]
    parts = _split_rows(h_ref[...].astype(F32), nat_ref, radix)
    w = HY_W
    sum_spec = _class_spectra(_transform_parts(fwd_refs, [x[:, 0:w] for x in parts], n), radix)
    diff_spec = _class_spectra(_transform_parts(fwd_refs, [x[:, w:2 * w] for x in parts], n), radix)
    for c in range(radix):
        g_ref[2 * c] = sum_spec[c][0]
        g_ref[2 * c + 1] = diff_spec[c][1]


def _filter_spectrum(seq, fwd, hs):
    radix = LONGCONV_RADIX[seq]
    n = seq // radix
    table = pl.BlockSpec((2 * n, n), lambda l: (0, 0))
    return pl.pallas_call(
        functools.partial(_spectrum_kernel, radix=radix, n=n),
        out_shape=jax.ShapeDtypeStruct((DEPTH, 2 * radix, n, HY_W), F32),
        grid=(DEPTH,),
        in_specs=[*([table] * radix), pl.BlockSpec((None, seq, 2 * HY_W), lambda l: (l, 0, 0))],
        out_specs=pl.BlockSpec((None, 2 * radix, n, HY_W), lambda l: (l, 0, 0, 0)),
        scratch_shapes=[pltpu.VMEM((2 * HY_W // LANES, seq, LANES), F32)],
        compiler_params=_params("parallel"),
        name=f"hyena_spectrum_{seq}",
    )(*fwd, hs)


def _longconv_kernel(*refs, radix, n, n_seq):
    v_ref = refs[0]
    fwd_refs = refs[1:1 + radix]
    inv_refs = refs[1 + radix:1 + 2 * radix]
    g_ref, bias_ref, o_ref, vin_ref, out_ref = refs[1 + 2 * radix:]
    seq = radix * n
    classes = _freq_classes(radix)

    def forward(s, c, slot):
        v = v_ref[seq * s:seq * (s + 1), MXU_COLS * c:MXU_COLS * (c + 1)].astype(F32)
        return (_transform_parts(fwd_refs, _split_rows(v, vin_ref.at[slot], radix), n),)

    def finish(s, c, slot, parts):
        rows = slice(seq * s, seq * (s + 1))
        cs = slice(MXU_COLS * c, MXU_COLS * (c + 1))
        prods = []
        for i, (ur, ui) in enumerate(_class_spectra(parts, radix)):
            gr, gi = g_ref[2 * i, :, cs], g_ref[2 * i + 1, :, cs]
            yr, yi = ur * gr - ui * gi, ur * gi + ui * gr
            prods.append((yr, -yi) if classes[i][1] else (yr, yi))
        outs = []
        for r in range(radix):
            acc = None
            for (k, m), y in zip(classes, prods):
                acc = _add_rotated(acc, y, (-_rot_power(k, m, r)) % 4)
            spec = jnp.concatenate(acc, axis=0).astype(BF16)
            outs.append(_dot(inv_refs[r][...], spec))
        y = _interleave_rows(outs, out_ref.at[slot])
        v = jnp.concatenate([vin_ref[slot, i] for i in range(MXU_COLS // LANES)], axis=1)
        o_ref[rows, cs] = (y + bias_ref[:, cs] * v).astype(BF16)

    chains = [(s, c) for s in range(n_seq) for c in range(HY_W // MXU_COLS)]
    _run_chains([(s, c, i % CHAINS_IN_FLIGHT) for i, (s, c) in enumerate(chains)], forward, finish)


def _long_conv(seq, n_rows, row0, vh, tables, spectrum, hy_bias, layer):
    radix = LONGCONV_RADIX[seq]
    n = seq // radix
    n_seq = LONGCONV_SEQS_PER_STEP[seq]
    t = n_seq * seq
    fwd, inv = tables
    scratch = pltpu.VMEM((CHAINS_IN_FLIGHT, MXU_COLS // LANES, seq, LANES), F32)
    return pl.pallas_call(
        functools.partial(_longconv_kernel, radix=radix, n=n, n_seq=n_seq),
        out_shape=jax.ShapeDtypeStruct((n_rows, HY_W), BF16),
        grid=(n_rows // t,),
        in_specs=[
            pl.BlockSpec((t, HY_W), lambda b: (row0 // t + b, 0)),
            *([pl.BlockSpec((2 * n, n), lambda b: (0, 0), **RESIDENT)] * radix),
            *([pl.BlockSpec((n, 2 * n), lambda b: (0, 0), **RESIDENT)] * radix),
            pl.BlockSpec((None, 2 * radix, n, HY_W), lambda b: (layer, 0, 0, 0), **RESIDENT),
            pl.BlockSpec((None, 1, HY_W), lambda b: (layer, 0, 0)),
        ],
        out_specs=pl.BlockSpec((t, HY_W), lambda b: (b, 0)),
        scratch_shapes=[scratch, scratch],
        compiler_params=_params("parallel"),
        name=f"hyena_longconv_{seq}",
    )(vh, *fwd, *inv, spectrum, hy_bias.reshape(DEPTH, 1, HY_W))


ONES_ROWS = 16
CTX_SEQS_PER_STEP = 4
QBLOCKS_PER_STEP = 8


def _head_rows(q, j):
    return jnp.concatenate([q[:, HEAD_DIM * hd:HEAD_DIM * (hd + 1)]
                            for hd in range(GROUP * j, GROUP * (j + 1))], axis=0)


def _lane_sink(sink_ref, layer, j, block):
    head = lax.broadcasted_iota(jnp.int32, (1, GROUP * block), 1) // block
    sink = jnp.zeros((1, GROUP * block), F32)
    for g in range(GROUP):
        sink = jnp.where(head == g, sink_ref[layer, GROUP * j + g] * LOG2_E, sink)
    return sink


def _with_ones(vt):
    return jnp.concatenate([vt, jnp.ones((ONES_ROWS, vt.shape[1]), BF16)], axis=0)


def _softmax_pv_t(parts, sink):
    m = sink
    for lg, _ in parts:
        m = jnp.maximum(m, jnp.max(lg, axis=0, keepdims=True))
    out = None
    for lg, vt in parts:
        pv = _dot(vt, jnp.exp2(lg - m).astype(BF16))
        out = pv if out is None else out + pv
    den = out[HEAD_DIM:HEAD_DIM + 1, :] + jnp.exp2(sink - m)
    return out[0:HEAD_DIM, :] / den


def _store_heads(o_ref, rows, j, block, o_t):
    for g in range(GROUP):
        hd = GROUP * j + g
        o_ref[rows, HEAD_DIM * hd:HEAD_DIM * (hd + 1)] = (
            o_t[:, block * g:block * (g + 1)].T.astype(BF16))


def _ctx_attn_kernel(sink_ref, q_ref, k_ref, vt_ref, o_ref, *, layer):
    def logits(n, j):
        rows = slice(SEQ * n, SEQ * (n + 1))
        hs = slice(HEAD_DIM * j, HEAD_DIM * (j + 1))
        return _dot_nt(k_ref[rows, hs], _head_rows(q_ref[rows, :], j)), _with_ones(vt_ref[hs, rows])

    def finish(n, j, s, vt):
        o_t = _softmax_pv_t([(s, vt)], _lane_sink(sink_ref, layer, j, SEQ))
        _store_heads(o_ref, slice(SEQ * n, SEQ * (n + 1)), j, SEQ, o_t)

    _run_chains([(n, j) for n in range(CTX_SEQS_PER_STEP) for j in range(N_KV_HEADS)],
                logits, finish)


def _ctx_attention(sink, q, k, vt, layer):
    t = CTX_SEQS_PER_STEP * SEQ
    row = lambda b: (b, 0)
    return pl.pallas_call(
        functools.partial(_ctx_attn_kernel, layer=layer),
        out_shape=jax.ShapeDtypeStruct((N_CTX, ATTN_W), BF16),
        grid=(N_CTX // t,),
        in_specs=[
            pl.BlockSpec(memory_space=pltpu.SMEM),
            pl.BlockSpec((t, ATTN_W), row),
            pl.BlockSpec((t, KV_W), row),
            pl.BlockSpec((KV_W, t), lambda b: (0, b)),
        ],
        out_specs=pl.BlockSpec((t, ATTN_W), row),
        compiler_params=_params("parallel"),
        name="context_attention",
    )(sink, q, k, vt)


def _band_bias():
    kpos = jnp.arange(BAND)[None, :, None] - jnp.arange(3)[:, None, None] * WINDOW
    qpos = (jnp.arange(GROUP * QBLOCK) & (QBLOCK - 1))[None, None, :]
    return jnp.where(jnp.abs(qpos - kpos) <= WINDOW, 0.0, MASK_VALUE).astype(F32)


def _lat_attn_kernel(sink_ref, q_ref, k_ref, vt_ref, ck_ref, cv_ref, bias_ref, o_ref,
                     kc_ref, vct_ref, *, layer):
    i = pl.program_id(1)

    @pl.when(i == 0)
    def _():
        kc_ref[...] = ck_ref[...].astype(BF16)
        vct = cv_ref[...].T.astype(BF16)
        for j in range(N_KV_HEADS):
            vct_ref[j] = _with_ones(vct[HEAD_DIM * j:HEAD_DIM * (j + 1), :])

    def logits(n, j):
        start = (i * QBLOCKS_PER_STEP + n) * QBLOCK
        ws = pl.multiple_of(jnp.clip(start - WINDOW, 0, DEC_SEQ - BAND), QBLOCK)
        placement = jnp.where(start < WINDOW, 0, jnp.where(start + QBLOCK + WINDOW > DEC_SEQ, 2, 1))
        q4 = _head_rows(q_ref[QBLOCK * n:QBLOCK * (n + 1), :], j)
        hs = slice(HEAD_DIM * j, HEAD_DIM * (j + 1))
        s_ctx = _dot_nt(kc_ref[:, hs], q4)
        s_band = _dot_nt(k_ref[pl.ds(ws, BAND), hs], q4) + bias_ref[placement]
        return s_ctx, s_band, _with_ones(vt_ref[hs, pl.ds(ws, BAND)])

    def finish(n, j, s_ctx, s_band, vbt):
        o_t = _softmax_pv_t([(s_ctx, vct_ref[j]), (s_band, vbt)],
                            _lane_sink(sink_ref, layer, j, QBLOCK))
        _store_heads(o_ref, slice(QBLOCK * n, QBLOCK * (n + 1)), j, QBLOCK, o_t)

    _run_chains([(n, j) for n in range(QBLOCKS_PER_STEP) for j in range(N_KV_HEADS)],
                logits, finish, ahead=2)


def _lat_attention(sink, q, k, vt, cache_k, cache_v, band_bias, layer):
    t = QBLOCKS_PER_STEP * QBLOCK
    steps = DEC_SEQ // t
    seq_block = N_CTX // DEC_SEQ
    return pl.pallas_call(
        functools.partial(_lat_attn_kernel, layer=layer),
        out_shape=jax.ShapeDtypeStruct((N_LAT, ATTN_W), BF16),
        grid=(DEC_BATCH, steps),
        in_specs=[
            pl.BlockSpec(memory_space=pltpu.SMEM),
            pl.BlockSpec((t, ATTN_W), lambda b, i: (N_CTX // t + b * steps + i, 0)),
            pl.BlockSpec((DEC_SEQ, KV_W), lambda b, i: (seq_block + b, 0)),
            pl.BlockSpec((KV_W, DEC_SEQ), lambda b, i: (0, seq_block + b)),
            pl.BlockSpec((None, None, PAST_LEN, KV_W), lambda b, i: (b, layer, 0, 0)),
            pl.BlockSpec((None, None, PAST_LEN, KV_W), lambda b, i: (b, layer, 0, 0)),
            pl.BlockSpec((3, BAND, GROUP * QBLOCK), lambda b, i: (0, 0, 0)),
        ],
        out_specs=pl.BlockSpec((t, ATTN_W), lambda b, i: (b * steps + i, 0)),
        scratch_shapes=[pltpu.VMEM((PAST_LEN, KV_W), BF16),
                        pltpu.VMEM((N_KV_HEADS, HEAD_DIM + ONES_ROWS, PAST_LEN), BF16)],
        compiler_params=_params("parallel", "arbitrary"),
        name="latent_attention",
    )(sink, q, k, vt, cache_k, cache_v, band_bias)


N_GATE_CHUNKS = D_MODEL // MXU_COLS


def _merge_kernel(h_ref, atc_ref, atl_ref, ybc_ref, ybl_ref, x0_ref, *rest, n_x):
    x_refs, mod_ref, rest = rest[0:n_x], rest[n_x], rest[n_x + 1:]
    wga_refs = rest[0:N_GATE_CHUNKS]
    wgb_refs = rest[N_GATE_CHUNKS:2 * N_GATE_CHUNKS]
    wpa_ref, wpb_ref, wo_ref, g_ref, b_ref, xo_ref, ho_ref, mg_ref = rest[2 * N_GATE_CHUNKS:]
    is_ctx = pl.program_id(0) < N_CTX // ROW_TILE
    g1 = mod_ref[:, 2 * D_MODEL:3 * D_MODEL]
    sh2 = mod_ref[:, 3 * D_MODEL:4 * D_MODEL]
    sc2 = mod_ref[:, 4 * D_MODEL:5 * D_MODEL]

    def merge_branches(s):
        rs = slice(SUB_ROWS * s, SUB_ROWS * (s + 1))
        h = h_ref[rs, :]
        attn = jnp.where(is_ctx, atc_ref[rs, :], atl_ref[rs, :])
        conv = jnp.where(is_ctx, ybc_ref[rs, :], ybl_ref[rs, :])
        hy = (x0_ref[rs, :].astype(F32) * conv.astype(F32)).astype(BF16)
        for j in range(N_GATE_CHUNKS):
            cs = slice(MXU_COLS * j, MXU_COLS * (j + 1))
            ga = jax.nn.sigmoid(_dot(h, wga_refs[j][...]))
            gb = jax.nn.sigmoid(_dot(h, wgb_refs[j][...]))
            mg_ref[rs, cs] = (ga * _dot(attn, wpa_ref[:, cs])
                              + gb * _dot(hy, wpb_ref[:, cs])).astype(BF16)
        return ()

    def project_norm(s):
        rs = slice(SUB_ROWS * s, SUB_ROWS * (s + 1))
        sub = _dot(mg_ref[rs, :], wo_ref[...])
        x_in = x_refs[0][rs, :] if n_x == 1 else jnp.where(is_ctx, x_refs[0][rs, :], x_refs[1][rs, :])
        x = _layer_norm(DEEPNORM_ALPHA * x_in + g1 * sub, g_ref[...], b_ref[...])
        xo_ref[rs, :] = x
        ho_ref[rs, :] = (x * (1.0 + sc2) + sh2).astype(BF16)

    _run_chains([(s,) for s in range(ROW_TILE // SUB_ROWS)], merge_branches, project_norm)


def _merge(h, atc, atl, ybc, ybl, x0, x, mods, w_in, wpa, wpb, wo, ln_g, ln_b, layer):
    t = ROW_TILE
    nct = N_CTX // t
    row = lambda m: (m, 0)
    ctx_row = lambda m: (jnp.minimum(m, nct - 1), 0)
    lat_row = lambda m: (jnp.maximum(m - nct, 0), 0)
    gate_block0 = GATE_COL0 // MXU_COLS
    gate_specs = [_weight_spec((D_MODEL, MXU_COLS), (0, gate_block0 + j))
                  for j in range(2 * N_GATE_CHUNKS)]
    x_specs = [pl.BlockSpec((t, D_MODEL), row)] if len(x) == 1 else _split_stream_specs(t)
    return pl.pallas_call(
        functools.partial(_merge_kernel, n_x=len(x)),
        out_shape=(jax.ShapeDtypeStruct((N_TOK, D_MODEL), F32),
                   jax.ShapeDtypeStruct((N_TOK, D_MODEL), BF16)),
        grid=(N_TOK // t,),
        in_specs=[
            pl.BlockSpec((t, D_MODEL), row),
            pl.BlockSpec((t, ATTN_W), ctx_row),
            pl.BlockSpec((t, ATTN_W), lat_row),
            pl.BlockSpec((t, HY_W), ctx_row),
            pl.BlockSpec((t, HY_W), lat_row),
            pl.BlockSpec((t, HY_W), row),
            *x_specs,
            _mod_spec(layer, t),
            *gate_specs,
            _weight_spec((ATTN_W, D_MODEL)),
            _weight_spec((HY_W, D_MODEL)),
            _weight_spec((D_MODEL, D_MODEL)),
            _layer_spec((1, D_MODEL), layer),
            _layer_spec((1, D_MODEL), layer),
        ],
        out_specs=(pl.BlockSpec((t, D_MODEL), row), pl.BlockSpec((t, D_MODEL), row)),
        scratch_shapes=[pltpu.VMEM((t, D_MODEL), BF16)],
        compiler_params=_params("parallel"),
        name="merge_ln1",
    )(h, atc, atl, ybc, ybl, x0, *x, mods, *([w_in] * (2 * N_GATE_CHUNKS)), wpa, wpb, wo,
      ln_g, ln_b)


def _ffn_kernel(h_ref, hp_ref, hn_ref, x_ref, mod_ref, nmod_ref, wg_ref, wv_ref, wd_ref,
                cw_ref, cb_ref, g_ref, b_ref, *rest, last_layer):
    if last_layer:
        yp_ref, ys_ref, hx_ref, a_ref, xo_ref = rest
    else:
        xo_ref, ho_ref, hx_ref, a_ref = rest
    n_sub = ROW_TILE // SUB_ROWS
    ext = SUB_ROWS + 2 * HALO
    is_ctx, seq_start, seq_end = _tile_edges(pl.program_id(0), ROW_TILE)
    _fill_extended(hx_ref, hp_ref, h_ref, hn_ref)
    g2 = mod_ref[:, 5 * D_MODEL:6 * D_MODEL]
    sh = nmod_ref[:, 0:D_MODEL]
    sc = nmod_ref[:, D_MODEL:2 * D_MODEL]

    def up_gate(j):
        rs = slice(SUB_ROWS * j, SUB_ROWS * (j + 1))
        zero_top, zero_bot = _sub_edges(j, n_sub, is_ctx, seq_start, seq_end)
        hx = hx_ref[SUB_ROWS * j:SUB_ROWS * j + ext, :]
        hm = h_ref[rs, :]
        for c in range(D_FF // MXU_COLS):
            cs = slice(MXU_COLS * c, MXU_COLS * (c + 1))
            gate = _dwconv3(_dot(hx, wg_ref[:, cs]), zero_top, zero_bot, cw_ref, cb_ref, cs)
            gelu = 0.5 * gate * (1.0 + lax.erf(gate * math.sqrt(0.5)))
            a_ref[rs, cs] = (gelu * _dot(hm, wv_ref[:, cs])).astype(BF16)
        return ()

    def down_norm(j):
        rs = slice(SUB_ROWS * j, SUB_ROWS * (j + 1))
        sub = _dot(a_ref[rs, :], wd_ref[...])
        x = _layer_norm(DEEPNORM_ALPHA * x_ref[rs, :] + g2 * sub, g_ref[...], b_ref[...])
        xo_ref[rs, :] = x
        if not last_layer:
            ho_ref[rs, :] = (x * (1.0 + sc) + sh).astype(BF16)

    _run_chains([(j,) for j in range(n_sub)], up_gate, down_norm)

    if last_layer:
        @pl.when(is_ctx)
        def _():
            yp_ref[...] = xo_ref[...]

        @pl.when(jnp.logical_not(is_ctx))
        def _():
            ys_ref[...] = xo_ref[...]


def _ffn(h, x, mods, w_up, w_down, conv_w, conv_b, ln_g, ln_b, layer):
    t = ROW_TILE
    nct = N_CTX // t
    row = lambda m: (m, 0)
    last_layer = layer == DEPTH - 1
    if last_layer:
        out_shape = (jax.ShapeDtypeStruct((N_CTX, D_MODEL), F32),
                     jax.ShapeDtypeStruct((N_LAT, D_MODEL), F32))
        out_specs = (pl.BlockSpec((t, D_MODEL), lambda m: (jnp.minimum(m, nct - 1), 0)),
                     pl.BlockSpec((t, D_MODEL), lambda m: (jnp.maximum(m - nct, 0), 0)))
        scratch = [pltpu.VMEM((t, D_MODEL), F32)]
    else:
        out_shape = (jax.ShapeDtypeStruct((N_TOK, D_MODEL), F32),
                     jax.ShapeDtypeStruct((N_TOK, D_MODEL), BF16))
        out_specs = (pl.BlockSpec((t, D_MODEL), row), pl.BlockSpec((t, D_MODEL), row))
        scratch = []
    return pl.pallas_call(
        functools.partial(_ffn_kernel, last_layer=last_layer),
        out_shape=out_shape,
        grid=(N_TOK // t,),
        in_specs=[
            pl.BlockSpec((t, D_MODEL), row),
            *_halo_specs(t, D_MODEL),
            pl.BlockSpec((t, D_MODEL), row),
            _mod_spec(layer, t),
            _mod_spec(min(layer + 1, DEPTH - 1), t),
            _weight_spec((D_MODEL, D_FF), (0, 0)),
            _weight_spec((D_MODEL, D_FF), (0, 1)),
            _weight_spec((D_FF, D_MODEL)),
            _layer_spec((3, D_FF), layer),
            _layer_spec((1, D_FF), layer),
            _layer_spec((1, D_MODEL), layer),
            _layer_spec((1, D_MODEL), layer),
        ],
        out_specs=out_specs,
        scratch_shapes=[pltpu.VMEM((t + 2 * HALO, D_MODEL), BF16), pltpu.VMEM((t, D_FF), BF16),
                        *scratch],
        compiler_params=_params("arbitrary"),
        name="conv_ffn_ln2",
    )(h, h, h, x, mods, mods, w_up, w_up, w_down, conv_w, conv_b, ln_g, ln_b)


def kernel(x_prompt, x_sample, cache_k, cache_v, c, c_ctx, w_ada, b_ada, w_in, attn_sink,
           hy_conv_w, hy_conv_b, filt_w1, filt_b1, filt_freq1, filt_w2, filt_b2, filt_freq2,
           filt_w3, hy_bias, w_pa, w_pb, w_out, ln1_g, ln1_b, w_up, ffn_conv_w, ffn_conv_b,
           w_down, ln2_g, ln2_b):
    cond = jnp.concatenate([c_ctx[None, :], c,
                            jnp.zeros((N_MOD_ROWS - 1 - DEC_BATCH, D_MODEL), F32)], axis=0)
    mods = _modulation(cond, w_ada, b_ada).reshape(DEPTH, N_MOD_ROWS, 1, 6 * D_MODEL)

    x = (x_prompt.reshape(N_CTX, D_MODEL), x_sample.reshape(N_LAT, D_MODEL))
    h = _prologue(*x, mods)

    cos_tab, sin_tab = _rope_tables()
    conv_args = {}
    for seq in (SEQ, DEC_SEQ):
        tables = _dft_tables(seq)
        hs = _filters(seq, filt_w1, filt_b1, filt_freq1, filt_w2, filt_b2, filt_freq2, filt_w3)
        conv_args[seq] = (tables, _filter_spectrum(seq, tables[0], hs))

    cache_k = cache_k.reshape(DEC_BATCH, DEPTH, PAST_LEN, KV_W)
    cache_v = cache_v.reshape(DEC_BATCH, DEPTH, PAST_LEN, KV_W)
    hy_conv_b3 = hy_conv_b.reshape(DEPTH, 1, 3 * HY_W)
    ffn_conv_b3 = ffn_conv_b.reshape(DEPTH, 1, D_FF)
    ln1_g3, ln1_b3 = ln1_g.reshape(DEPTH, 1, D_MODEL), ln1_b.reshape(DEPTH, 1, D_MODEL)
    ln2_g3, ln2_b3 = ln2_g.reshape(DEPTH, 1, D_MODEL), ln2_b.reshape(DEPTH, 1, D_MODEL)

    band_bias = _band_bias()
    w_in_l = w_in[0:1].astype(BF16)
    keys, values = [], []
    for l in range(DEPTH):
        to_cast = [(w, l) for w in (w_pa, w_pb, w_out, w_up, w_down)]
        if l + 1 < DEPTH:
            to_cast.append((w_in, l + 1))
        q, k, vt, kf, vf, x0, vh, w_pa_l, w_pb_l, w_out_l, w_up_l, w_down_l, *w_in_next = _in_proj(
            h, w_in_l, cos_tab, sin_tab, hy_conv_w, hy_conv_b3, to_cast, l)
        keys.append(kf)
        values.append(vf)

        ybc = _long_conv(SEQ, N_CTX, 0, vh, *conv_args[SEQ], hy_bias, l)
        ybl = _long_conv(DEC_SEQ, N_LAT, N_CTX, vh, *conv_args[DEC_SEQ], hy_bias, l)
        atc = _ctx_attention(attn_sink, q, k, vt, l)
        atl = _lat_attention(attn_sink, q, k, vt, cache_k, cache_v, band_bias, l)

        x1, h2 = _merge(h, atc, atl, ybc, ybl, x0, x, mods, w_in_l, w_pa_l, w_pb_l, w_out_l,
                        ln1_g3, ln1_b3, l)
        x2, h = _ffn(h2, x1, mods, w_up_l, w_down_l, ffn_conv_w, ffn_conv_b3, ln2_g3, ln2_b3, l)
        x = (x2,)
        if w_in_next:
            w_in_l = w_in_next[0]

    y_prompt_rows, y_sample_rows = x2, h
    y_prompt = y_prompt_rows.reshape(BATCH, SEQ, D_MODEL)
    y_sample = y_sample_rows.reshape(DEC_BATCH, DEC_SEQ, D_MODEL)
    to_cache = lambda ts: jnp.stack(ts, axis=1).reshape(BATCH, DEPTH, SEQ, N_KV_HEADS, HEAD_DIM)
    new_k = to_cache([t.reshape(BATCH, SEQ, KV_W) for t in keys])
    new_v = to_cache([t.reshape(BATCH, SEQ, KV_W) for t in values])
    return (y_prompt, y_sample, new_k, new_v)
```

```python
import functools
import math

import jax
import jax.numpy as jnp
from jax import lax
from jax.experimental import pallas as pl
from jax.experimental.pallas import tpu as pltpu

D_MODEL = 1024
BATCH = 16
SEQ = 256
DEPTH = 4
DEC_BATCH = 4
DEC_SEQ = 2048
PAST_LEN = 512
GRID_W = 64
HEAD_DIM = 64
N_HEADS = 8
N_KV_HEADS = 2
GROUP = N_HEADS // N_KV_HEADS
ATTN_W = N_HEADS * HEAD_DIM
KV_W = N_KV_HEADS * HEAD_DIM
WINDOW = 128
QBLOCK = 128
HY_W = 512
FILT_EMB = 33
FILT_EMB_PAD = 40
FILT_BANDS = (FILT_EMB - 1) // 2
FILT_HIDDEN = 64
D_FF = 2816
ROPE_BASE = 10000.0
LN_EPS = 1e-5
DEEPNORM_ALPHA = (2 * DEPTH) ** 0.25

N_CTX = BATCH * SEQ
N_LAT = DEC_BATCH * DEC_SEQ
N_TOK = N_CTX + N_LAT
N_MOD_ROWS = 8
QKV_W = ATTN_W + 2 * KV_W
HY_BLOCK = QKV_W
assert 3 * HY_W == 2 * HY_BLOCK
GATE_COL0 = QKV_W + 3 * HY_W
BAND = QBLOCK + 2 * WINDOW
MASK_VALUE = -1e30
LOG2_E = math.log2(math.e)

F32 = jnp.float32
BF16 = jnp.bfloat16

VMEM_LIMIT_BYTES = 56 * 1024 * 1024
MXU_COLS = 256
ROW_TILE = 1024
SUB_ROWS = SEQ
HALO = 16
RESIDENT = dict(pipeline_mode=pl.Buffered(1))
CAST_BLOCKS = 8


def _params(*semantics):
    return pltpu.CompilerParams(dimension_semantics=semantics, vmem_limit_bytes=VMEM_LIMIT_BYTES)


def _dot(a, b):
    return jnp.dot(a, b, preferred_element_type=F32)


def _dot_split(a, b):
    a_hi = a.astype(BF16)
    b_hi = b.astype(BF16)
    a_lo = (a - a_hi.astype(F32)).astype(BF16)
    b_lo = (b - b_hi.astype(F32)).astype(BF16)
    return _dot(a_hi, b_hi) + (_dot(a_hi, b_lo) + _dot(a_lo, b_hi))


def _dot_nt(a, b):
    return lax.dot_general(a, b, (((1,), (1,)), ((), ())), preferred_element_type=F32)


def _mod_row(tile_rows):
    n_ctx_tiles = N_CTX // tile_rows
    tiles_per_seq = DEC_SEQ // tile_rows

    def fn(m):
        return jnp.where(m < n_ctx_tiles, 0, 1 + (m - n_ctx_tiles) // tiles_per_seq)

    return fn


def _mod_spec(layer, tile_rows):
    grp = _mod_row(tile_rows)
    return pl.BlockSpec((None, None, 1, 6 * D_MODEL), lambda m, *_: (layer, grp(m), 0, 0))


def _layer_spec(shape, layer):
    zeros = (0,) * len(shape)
    return pl.BlockSpec((None,) + tuple(shape), lambda *_: (layer,) + zeros)


def _weight_spec(shape, block_index=None):
    block_index = block_index or (0,) * len(shape)
    return pl.BlockSpec((None,) + tuple(shape), lambda *_: (0,) + tuple(block_index), **RESIDENT)


def _layer_norm(y, g, b):
    mu = jnp.mean(y, axis=-1, keepdims=True)
    yc = y - mu
    var = jnp.mean(yc * yc, axis=-1, keepdims=True)
    return yc * lax.rsqrt(var + LN_EPS) * g + b


def _run_chains(chains, first_fn, finish_fn, ahead=1):
    pending = [first_fn(*chain) for chain in chains[:ahead]]
    for c, chain in enumerate(chains):
        if c + ahead < len(chains):
            pending.append(first_fn(*chains[c + ahead]))
        finish_fn(*chain, *pending.pop(0))


def _tile_edges(m, tile_rows):
    n_ctx_tiles = N_CTX // tile_rows
    tiles_per_seq = DEC_SEQ // tile_rows
    is_ctx = m < n_ctx_tiles
    lat_pos = (m - n_ctx_tiles) % tiles_per_seq
    return (is_ctx, jnp.logical_or(is_ctx, lat_pos == 0),
            jnp.logical_or(is_ctx, lat_pos == tiles_per_seq - 1))


def _halo_specs(tile_rows, n_cols):
    tb = tile_rows // HALO
    last = N_TOK // HALO - 1
    return (pl.BlockSpec((HALO, n_cols), lambda m: (jnp.maximum(m * tb - 1, 0), 0)),
            pl.BlockSpec((HALO, n_cols), lambda m: (jnp.minimum((m + 1) * tb, last), 0)))


def _fill_extended(hx_ref, hp_ref, h_ref, hn_ref):
    rows = h_ref.shape[0]
    hx_ref[0:HALO, :] = hp_ref[...]
    hx_ref[HALO:HALO + rows, :] = h_ref[...]
    hx_ref[HALO + rows:2 * HALO + rows, :] = hn_ref[...]


def _sub_edges(j, n_sub, is_ctx, seq_start, seq_end):
    return (seq_start if j == 0 else is_ctx), (seq_end if j == n_sub - 1 else is_ctx)


def _dwconv3(z, zero_top, zero_bot, w_ref, b_ref, cs):
    n = z.shape[0] - 2 * HALO
    top = jnp.where(zero_top, 0.0, z[HALO - 8:HALO])
    bot = jnp.where(zero_bot, 0.0, z[HALO + n:HALO + n + 8])
    z = jnp.concatenate([z[0:HALO - 8], top, z[HALO:HALO + n], bot, z[HALO + n + 8:]], axis=0)
    prev = pltpu.roll(z, 1, 0)[HALO:HALO + n]
    nxt = pltpu.roll(z, z.shape[0] - 1, 0)[HALO:HALO + n]
    return (prev * w_ref[0:1, cs] + z[HALO:HALO + n] * w_ref[1:2, cs] + nxt * w_ref[2:3, cs]
            + b_ref[:, cs])


def _mod_kernel(cond_ref, w_ref, b_ref, o_ref):
    c = cond_ref[...]
    s = (c * jax.nn.sigmoid(c)).astype(BF16)
    o_ref[...] = _dot(s, w_ref[...].astype(BF16)) + b_ref[...]


def _modulation(cond, w_ada, b_ada):
    n_col = 6 * D_MODEL // D_MODEL
    return pl.pallas_call(
        _mod_kernel,
        out_shape=jax.ShapeDtypeStruct((DEPTH, N_MOD_ROWS, 6 * D_MODEL), F32),
        grid=(DEPTH, n_col),
        in_specs=[
            pl.BlockSpec((N_MOD_ROWS, D_MODEL), lambda l, j: (0, 0)),
            pl.BlockSpec((None, D_MODEL, D_MODEL), lambda l, j: (l, 0, j)),
            pl.BlockSpec((None, 1, D_MODEL), lambda l, j: (l, 0, j)),
        ],
        out_specs=pl.BlockSpec((None, N_MOD_ROWS, D_MODEL), lambda l, j: (l, 0, j)),
        compiler_params=_params("parallel", "parallel"),
        name="modulation",
    )(cond, w_ada, b_ada.reshape(DEPTH, 1, 6 * D_MODEL))


PRO_TILE = 1024


def _prologue_kernel(xp_ref, xs_ref, mod_ref, h_ref):
    m = pl.program_id(0)

    def emit(x):
        sh = mod_ref[:, 0:D_MODEL]
        sc = mod_ref[:, D_MODEL:2 * D_MODEL]
        h_ref[...] = (x * (1.0 + sc) + sh).astype(BF16)

    @pl.when(m < N_CTX // PRO_TILE)
    def _():
        emit(xp_ref[...])

    @pl.when(m >= N_CTX // PRO_TILE)
    def _():
        emit(xs_ref[...])


def _split_stream_specs(tile_rows):
    nct = N_CTX // tile_rows
    return (pl.BlockSpec((tile_rows, D_MODEL), lambda m: (jnp.minimum(m, nct - 1), 0)),
            pl.BlockSpec((tile_rows, D_MODEL), lambda m: (jnp.maximum(m - nct, 0), 0)))


def _prologue(xp, xs, mods):
    return pl.pallas_call(
        _prologue_kernel,
        out_shape=jax.ShapeDtypeStruct((N_TOK, D_MODEL), BF16),
        grid=(N_TOK // PRO_TILE,),
        in_specs=[*_split_stream_specs(PRO_TILE), _mod_spec(0, PRO_TILE)],
        out_specs=pl.BlockSpec((PRO_TILE, D_MODEL), lambda m: (m, 0)),
        compiler_params=_params("parallel"),
        name="prologue",
    )(xp, xs, mods)


ROPE_LANES = 2 * HEAD_DIM


N_IN_PROJ_OUTS = 7


def _in_proj_kernel(h_ref, hp_ref, hn_ref, wqkv_ref, why0_ref, why1_ref, cos_ref, sin_ref,
                    cw_ref, cb_ref, *rest):
    n_cast = (len(rest) - N_IN_PROJ_OUTS - 2) // 2
    cast_src = rest[0:n_cast]
    q_ref, k_ref, vt_ref, kf_ref, vf_ref, x0_ref, vh_ref = rest[n_cast:n_cast + N_IN_PROJ_OUTS]
    cast_dst = rest[n_cast + N_IN_PROJ_OUTS:-2]
    hx_ref, kv_ref = rest[-2:]

    @pl.when(pl.program_id(0) < CAST_BLOCKS)
    def _():
        for src, dst in zip(cast_src, cast_dst):
            dst[...] = src[...].astype(BF16)

    n_sub = ROW_TILE // SUB_ROWS
    ext = SUB_ROWS + 2 * HALO
    is_ctx, seq_start, seq_end = _tile_edges(pl.program_id(0), ROW_TILE)
    _fill_extended(hx_ref, hp_ref, h_ref, hn_ref)
    lane = lax.broadcasted_iota(jnp.int32, (SUB_ROWS, ROPE_LANES), 1)
    first_half = (lane & (HEAD_DIM // 4)) == 0
    scale = HEAD_DIM ** -0.5 * LOG2_E
    for j in range(n_sub):
        rs = slice(SUB_ROWS * j, SUB_ROWS * (j + 1))
        cos = cos_ref[rs, :]
        sin = sin_ref[rs, :]

        def rope(x):
            partner = jnp.where(first_half, pltpu.roll(x, ROPE_LANES - HEAD_DIM // 4, 1),
                                pltpu.roll(x, HEAD_DIM // 4, 1))
            return x * cos + partner * sin

        hm = h_ref[rs, :]
        for jj in range(ATTN_W // MXU_COLS):
            qq = _dot(hm, wqkv_ref[:, MXU_COLS * jj:MXU_COLS * (jj + 1)])
            for i in range(MXU_COLS // ROPE_LANES):
                c0 = MXU_COLS * jj + ROPE_LANES * i
                q_ref[rs, c0:c0 + ROPE_LANES] = (
                    rope(qq[:, ROPE_LANES * i:ROPE_LANES * (i + 1)]) * scale).astype(BF16)
        kv = _dot(hm, wqkv_ref[:, ATTN_W:QKV_W])
        k = kv[:, 0:KV_W]
        v = kv[:, KV_W:2 * KV_W]
        kv_ref[0, rs, :] = k
        kv_ref[1, rs, :] = v
        k_ref[rs, :] = rope(k).astype(BF16)
        vt_ref[:, rs] = v.T.astype(BF16)

        zero_top, zero_bot = _sub_edges(j, n_sub, is_ctx, seq_start, seq_end)
        hx = hx_ref[SUB_ROWS * j:SUB_ROWS * j + ext, :]

        def proj_conv(c0):
            w_ref = why0_ref if c0 < HY_BLOCK else why1_ref
            w = w_ref[:, c0 % HY_BLOCK:c0 % HY_BLOCK + MXU_COLS]
            return _dwconv3(_dot(hx, w), zero_top, zero_bot, cw_ref, cb_ref,
                            slice(c0, c0 + MXU_COLS))

        for jj in range(HY_W // MXU_COLS):
            c0 = MXU_COLS * jj
            x0_ref[rs, c0:c0 + MXU_COLS] = proj_conv(c0).astype(BF16)
            x1 = proj_conv(HY_W + c0)
            u = proj_conv(2 * HY_W + c0)
            vh_ref[rs, c0:c0 + MXU_COLS] = (x1 * u).astype(BF16)

    @pl.when(is_ctx)
    def _():
        kf_ref[...] = kv_ref[0]
        vf_ref[...] = kv_ref[1]


def _in_proj(h, w_in, cos_tab, sin_tab, conv_w, conv_b, to_cast, layer):
    t = ROW_TILE
    nct = N_CTX // t
    tps = DEC_SEQ // t
    tab = lambda m: (jnp.where(m < nct, 0, 1 + (m - nct) % tps), 0, 0)
    row = lambda m: (m, 0)
    ctx_row = lambda m: (jnp.minimum(m, nct - 1), 0)
    bf = lambda w: jax.ShapeDtypeStruct((N_TOK, w), BF16)
    cast_block = lambda m: jnp.minimum(m, CAST_BLOCKS - 1)
    cast_in, cast_out, cast_shapes = [], [], []
    for w, w_layer in to_cast:
        rows, cols = w.shape[1] // CAST_BLOCKS, w.shape[2]
        cast_in.append(pl.BlockSpec((None, rows, cols),
                                    lambda m, w_layer=w_layer: (w_layer, cast_block(m), 0)))
        cast_out.append(pl.BlockSpec((None, rows, cols), lambda m: (0, cast_block(m), 0)))
        cast_shapes.append(jax.ShapeDtypeStruct((1,) + w.shape[1:], BF16))
    return pl.pallas_call(
        _in_proj_kernel,
        out_shape=(bf(ATTN_W), bf(KV_W),
                   jax.ShapeDtypeStruct((KV_W, N_TOK), BF16),
                   jax.ShapeDtypeStruct((N_CTX, KV_W), F32),
                   jax.ShapeDtypeStruct((N_CTX, KV_W), F32),
                   bf(HY_W), bf(HY_W), *cast_shapes),
        grid=(N_TOK // t,),
        in_specs=[
            pl.BlockSpec((t, D_MODEL), row),
            *_halo_specs(t, D_MODEL),
            _weight_spec((D_MODEL, QKV_W), (0, 0)),
            _weight_spec((D_MODEL, HY_BLOCK), (0, 1)),
            _weight_spec((D_MODEL, HY_BLOCK), (0, 2)),
            pl.BlockSpec((None, t, ROPE_LANES), tab),
            pl.BlockSpec((None, t, ROPE_LANES), tab),
            _layer_spec((3, 3 * HY_W), layer),
            _layer_spec((1, 3 * HY_W), layer),
            *cast_in,
        ],
        out_specs=(pl.BlockSpec((t, ATTN_W), row),
                   pl.BlockSpec((t, KV_W), row),
                   pl.BlockSpec((KV_W, t), lambda m: (0, m)),
                   pl.BlockSpec((t, KV_W), ctx_row),
                   pl.BlockSpec((t, KV_W), ctx_row),
                   pl.BlockSpec((t, HY_W), row),
                   pl.BlockSpec((t, HY_W), row),
                   *cast_out),
        scratch_shapes=[pltpu.VMEM((t + 2 * HALO, D_MODEL), BF16),
                        pltpu.VMEM((2, t, KV_W), F32)],
        compiler_params=_params("arbitrary"),
        name="in_proj",
    )(h, h, h, w_in, w_in, w_in, cos_tab, sin_tab, conv_w, conv_b, *(w for w, _ in to_cast))


def _rope_tables():
    rows = DEC_SEQ // GRID_W
    r, col = jnp.meshgrid(jnp.arange(rows), jnp.arange(GRID_W), indexing='ij')
    pos = jnp.stack([r.reshape(-1), col.reshape(-1)], axis=-1).astype(F32)
    half = HEAD_DIM // 2
    inv_freq = 1.0 / (ROPE_BASE ** (jnp.arange(0, half, 2, dtype=F32) / half))
    ang = pos[:, :, None] * inv_freq
    ang = jnp.stack([ang, ang], axis=-2).reshape(DEC_SEQ, HEAD_DIM)
    cos = jnp.tile(jnp.cos(ang), (1, ROPE_LANES // HEAD_DIM))
    sin = jnp.tile(jnp.sin(ang), (1, ROPE_LANES // HEAD_DIM))
    lane = jnp.arange(ROPE_LANES)
    sin = jnp.where((lane & (HEAD_DIM // 4)) == 0, -sin, sin)
    n_slab = DEC_SEQ // ROW_TILE
    cos = jnp.concatenate([jnp.ones((1, ROW_TILE, ROPE_LANES), F32),
                           cos.reshape(n_slab, ROW_TILE, ROPE_LANES)], axis=0)
    sin = jnp.concatenate([jnp.zeros((1, ROW_TILE, ROPE_LANES), F32),
                           sin.reshape(n_slab, ROW_TILE, ROPE_LANES)], axis=0)
    return cos, sin


def _filter_kernel(feat_ref, t_ref, dec_ref, w1_ref, b1_ref, f1_ref, w2_ref, b2_ref, f2_ref,
                   w3_ref, o_ref):
    a = _dot_split(feat_ref[...], w1_ref[...]) + b1_ref[...]
    a = jnp.sin(f1_ref[...] * a)
    a = _dot_split(a, w2_ref[...]) + b2_ref[...]
    a = jnp.sin(f2_ref[...] * a)
    hh = _dot_split(a, w3_ref[...])
    window = jnp.exp(-t_ref[...] * dec_ref[...])
    h_fwd = hh[:, 0:HY_W] * window
    h_bwd = hh[:, HY_W:2 * HY_W] * window
    row = lax.broadcasted_iota(jnp.int32, h_bwd.shape, 0)
    h_bwd = jnp.where(row == 0, 0.0, h_bwd)
    o_ref[:, 0:HY_W] = (h_fwd + h_bwd).astype(BF16)
    o_ref[:, HY_W:2 * HY_W] = (h_fwd - h_bwd).astype(BF16)


def _filters(seq, w1, b1, f1, w2, b2, f2, w3):
    t = jnp.linspace(0.0, 1.0, seq, dtype=F32)[:, None]
    w = 2.0 * math.pi * jnp.arange(seq, dtype=F32) / seq
    f = jnp.linspace(1e-4, FILT_BANDS - 1, FILT_BANDS, dtype=F32)
    zr = w[:, None] * f[None, :]
    feats = jnp.concatenate([t, jnp.cos(zr), -jnp.sin(zr),
                             jnp.zeros((seq, FILT_EMB_PAD - FILT_EMB), F32)], axis=-1)
    target = 1e-2
    decay = jnp.abs(jnp.linspace(math.log(target) / 1.5, math.log(target) / 0.3, HY_W,
                                 dtype=F32))[None, :]
    w1p = jnp.pad(w1, ((0, 0), (0, FILT_EMB_PAD - FILT_EMB), (0, 0)))
    vec = lambda a: a.reshape(DEPTH, 1, FILT_HIDDEN)
    return pl.pallas_call(
        _filter_kernel,
        out_shape=jax.ShapeDtypeStruct((DEPTH, seq, 2 * HY_W), BF16),
        grid=(DEPTH,),
        in_specs=[
            pl.BlockSpec((seq, FILT_EMB_PAD), lambda l: (0, 0)),
            pl.BlockSpec((seq, 1), lambda l: (0, 0)),
            pl.BlockSpec((1, HY_W), lambda l: (0, 0)),
            pl.BlockSpec((None, FILT_EMB_PAD, FILT_HIDDEN), lambda l: (l, 0, 0)),
            pl.BlockSpec((None, 1, FILT_HIDDEN), lambda l: (l, 0, 0)),
            pl.BlockSpec((None, 1, FILT_HIDDEN), lambda l: (l, 0, 0)),
            pl.BlockSpec((None, FILT_HIDDEN, FILT_HIDDEN), lambda l: (l, 0, 0)),
            pl.BlockSpec((None, 1, FILT_HIDDEN), lambda l: (l, 0, 0)),
            pl.BlockSpec((None, 1, FILT_HIDDEN), lambda l: (l, 0, 0)),
            pl.BlockSpec((None, FILT_HIDDEN, 2 * HY_W), lambda l: (l, 0, 0)),
        ],
        out_specs=pl.BlockSpec((None, seq, 2 * HY_W), lambda l: (l, 0, 0)),
        compiler_params=_params("parallel"),
        name=f"hyena_filter_{seq}",
    )(feats, t, decay, w1p, vec(b1), vec(f1), w2, vec(b2), vec(f2), w3)


LONGCONV_RADIX = {SEQ: 2, DEC_SEQ: 4}
LONGCONV_SEQS_PER_STEP = {SEQ: 4, DEC_SEQ: 1}
CHAINS_IN_FLIGHT = 2
TABLE_ROWS = 16
LANES = 128


def _freq_classes(radix):
    return [(k, m) for k in range(radix // 2) for m in range(2)]


def _rot_power(k, m, r):
    return (3 * k * r + 2 * m * r) % 4


def _add_rotated(acc, z, p):
    zr, zi = z
    re, im, s_re, s_im = ((zr, zi, 1, 1), (zi, zr, -1, 1), (zr, zi, -1, -1), (zi, zr, 1, -1))[p]
    if acc is None:
        return (re if s_re > 0 else -re, im if s_im > 0 else -im)
    return (acc[0] + re if s_re > 0 else acc[0] - re, acc[1] + im if s_im > 0 else acc[1] - im)


def _class_spectra(parts, radix):
    out = []
    for k, m in _freq_classes(radix):
        acc = None
        for r in range(radix):
            acc = _add_rotated(acc, parts[r], _rot_power(k, m, r))
        out.append((acc[0], -acc[1]) if m else acc)
    return out


def _phase(a, b, period):
    k = (a * b) & (period - 1)
    ang = k.astype(F32) * (2.0 * math.pi / period)
    return jnp.cos(ang), jnp.sin(ang)


def _table_kernel(o_ref, *, n, period, row_mult, row_add, col_mult, col_add, stack_rows, scale):
    groups = n // TABLE_ROWS
    b0 = col_mult * lax.broadcasted_iota(jnp.int32, (TABLE_ROWS, n), 1) + col_add
    a0 = row_mult * lax.broadcasted_iota(jnp.int32, (TABLE_ROWS, n), 0) + row_add
    cos_a, sin_a = _phase(a0, b0, period)
    b1 = col_mult * lax.broadcasted_iota(jnp.int32, (groups, n), 1) + col_add
    a1 = row_mult * TABLE_ROWS * lax.broadcasted_iota(jnp.int32, (groups, n), 0)
    cos_b, sin_b = _phase(a1, b1, period)
    cos = cos_a[None] * cos_b[:, None, :] - sin_a[None] * sin_b[:, None, :]
    msin = -(sin_a[None] * cos_b[:, None, :] + cos_a[None] * sin_b[:, None, :])
    cos = (cos.reshape(n, n) * scale).astype(BF16)
    msin = (msin.reshape(n, n) * scale).astype(BF16)
    if stack_rows:
        o_ref[0:n, :] = cos
        o_ref[n:2 * n, :] = msin
    else:
        o_ref[:, 0:n] = cos
        o_ref[:, n:2 * n] = msin


def _dft_tables(seq):
    radix = LONGCONV_RADIX[seq]
    n = seq // radix

    def table(stack_rows, **kw):
        shape = (2 * n, n) if stack_rows else (n, 2 * n)
        return pl.pallas_call(
            functools.partial(_table_kernel, n=n, period=4 * seq, stack_rows=stack_rows, **kw),
            out_shape=jax.ShapeDtypeStruct(shape, BF16),
            compiler_params=_params(),
            name=f"dft_table_{seq}",
        )()

    fwd = [table(True, row_mult=2, row_add=1, col_mult=radix, col_add=r, scale=1.0)
           for r in range(radix)]
    inv = [table(False, row_mult=radix, row_add=r, col_mult=2, col_add=1, scale=1.0 / seq)
           for r in range(radix)]
    return fwd, inv


def _split_rows(x, nat_ref, radix):
    n = x.shape[0] // radix
    k = x.shape[1] // LANES
    for i in range(k):
        nat_ref[i] = x[:, LANES * i:LANES * (i + 1)]
    return [jnp.concatenate([nat_ref[i, pl.ds(r, n, stride=radix), :] for i in range(k)], axis=1)
            for r in range(radix)]


def _interleave_rows(parts, nat_ref):
    radix = len(parts)
    n = parts[0].shape[0]
    k = parts[0].shape[1] // LANES
    for i in range(k):
        for r in range(radix):
            nat_ref[i, pl.ds(r, n, stride=radix), :] = parts[r][:, LANES * i:LANES * (i + 1)]
    return jnp.concatenate([nat_ref[i] for i in range(k)], axis=1)


def _transform_parts(fwd_refs, parts, n):
    out = []
    for tab_ref, x in zip(fwd_refs, parts):
        p = _dot(tab_ref[...], x.astype(BF16))
        out.append((p[0:n], p[n:2 * n]))
    return out


def _spectrum_kernel(*refs, radix, n):
    fwd_refs, (h_ref, g_ref, nat_ref) = refs[0:radix], refs[radix:]
    parts = _split_rows(h_ref[...].astype(F32), nat_ref, radix)
    w = HY_W
    sum_spec = _class_spectra(_transform_parts(fwd_refs, [x[:, 0:w] for x in parts], n), radix)
    diff_spec = _class_spectra(_transform_parts(fwd_refs, [x[:, w:2 * w] for x in parts], n), radix)
    for c in range(radix):
        g_ref[2 * c] = sum_spec[c][0]
        g_ref[2 * c + 1] = diff_spec[c][1]


def _filter_spectrum(seq, fwd, hs):
    radix = LONGCONV_RADIX[seq]
    n = seq // radix
    table = pl.BlockSpec((2 * n, n), lambda l: (0, 0))
    return pl.pallas_call(
        functools.partial(_spectrum_kernel, radix=radix, n=n),
        out_shape=jax.ShapeDtypeStruct((DEPTH, 2 * radix, n, HY_W), F32),
        grid=(DEPTH,),
        in_specs=[*([table] * radix), pl.BlockSpec((None, seq, 2 * HY_W), lambda l: (l, 0, 0))],
        out_specs=pl.BlockSpec((None, 2 * radix, n, HY_W), lambda l: (l, 0, 0, 0)),
        scratch_shapes=[pltpu.VMEM((2 * HY_W // LANES, seq, LANES), F32)],
        compiler_params=_params("parallel"),
        name=f"hyena_spectrum_{seq}",
    )(*fwd, hs)


def _longconv_kernel(*refs, radix, n, n_seq):
    v_ref = refs[0]
    fwd_refs = refs[1:1 + radix]
    inv_refs = refs[1 + radix:1 + 2 * radix]
    g_ref, bias_ref, o_ref, vin_ref, out_ref = refs[1 + 2 * radix:]
    seq = radix * n
    classes = _freq_classes(radix)

    def forward(s, c, slot):
        v = v_ref[seq * s:seq * (s + 1), MXU_COLS * c:MXU_COLS * (c + 1)].astype(F32)
        return (_transform_parts(fwd_refs, _split_rows(v, vin_ref.at[slot], radix), n),)

    def finish(s, c, slot, parts):
        rows = slice(seq * s, seq * (s + 1))
        cs = slice(MXU_COLS * c, MXU_COLS * (c + 1))
        prods = []
        for i, (ur, ui) in enumerate(_class_spectra(parts, radix)):
            gr, gi = g_ref[2 * i, :, cs], g_ref[2 * i + 1, :, cs]
            yr, yi = ur * gr - ui * gi, ur * gi + ui * gr
            prods.append((yr, -yi) if classes[i][1] else (yr, yi))
        outs = []
        for r in range(radix):
            acc = None
            for (k, m), y in zip(classes, prods):
                acc = _add_rotated(acc, y, (-_rot_power(k, m, r)) % 4)
            spec = jnp.concatenate(acc, axis=0).astype(BF16)
            outs.append(_dot(inv_refs[r][...], spec))
        y = _interleave_rows(outs, out_ref.at[slot])
        v = jnp.concatenate([vin_ref[slot, i] for i in range(MXU_COLS // LANES)], axis=1)
        o_ref[rows, cs] = (y + bias_ref[:, cs] * v).astype(BF16)

    chains = [(s, c) for s in range(n_seq) for c in range(HY_W // MXU_COLS)]
    _run_chains([(s, c, i % CHAINS_IN_FLIGHT) for i, (s, c) in enumerate(chains)], forward, finish)


def _long_conv(seq, n_rows, row0, vh, tables, spectrum, hy_bias, layer):
    radix = LONGCONV_RADIX[seq]
    n = seq // radix
    n_seq = LONGCONV_SEQS_PER_STEP[seq]
    t = n_seq * seq
    fwd, inv = tables
    scratch = pltpu.VMEM((CHAINS_IN_FLIGHT, MXU_COLS // LANES, seq, LANES), F32)
    return pl.pallas_call(
        functools.partial(_longconv_kernel, radix=radix, n=n, n_seq=n_seq),
        out_shape=jax.ShapeDtypeStruct((n_rows, HY_W), BF16),
        grid=(n_rows // t,),
        in_specs=[
            pl.BlockSpec((t, HY_W), lambda b: (row0 // t + b, 0)),
            *([pl.BlockSpec((2 * n, n), lambda b: (0, 0), **RESIDENT)] * radix),
            *([pl.BlockSpec((n, 2 * n), lambda b: (0, 0), **RESIDENT)] * radix),
            pl.BlockSpec((None, 2 * radix, n, HY_W), lambda b: (layer, 0, 0, 0), **RESIDENT),
            pl.BlockSpec((None, 1, HY_W), lambda b: (layer, 0, 0)),
        ],
        out_specs=pl.BlockSpec((t, HY_W), lambda b: (b, 0)),
        scratch_shapes=[scratch, scratch],
        compiler_params=_params("parallel"),
        name=f"hyena_longconv_{seq}",
    )(vh, *fwd, *inv, spectrum, hy_bias.reshape(DEPTH, 1, HY_W))


ONES_ROWS = 16
CTX_SEQS_PER_STEP = 4
QBLOCKS_PER_STEP = DEC_SEQ // QBLOCK


def _head_rows(q, j):
    return jnp.concatenate([q[:, HEAD_DIM * hd:HEAD_DIM * (hd + 1)]
                            for hd in range(GROUP * j, GROUP * (j + 1))], axis=0)


def _lane_sink(sink_ref, layer, j, block):
    head = lax.broadcasted_iota(jnp.int32, (1, GROUP * block), 1) // block
    sink = jnp.zeros((1, GROUP * block), F32)
    for g in range(GROUP):
        sink = jnp.where(head == g, sink_ref[layer, GROUP * j + g] * LOG2_E, sink)
    return sink


def _with_ones(vt):
    return jnp.concatenate([vt, jnp.ones((ONES_ROWS, vt.shape[1]), BF16)], axis=0)


def _softmax_pv_t(parts, sink):
    m = sink
    for lg, _ in parts:
        m = jnp.maximum(m, jnp.max(lg, axis=0, keepdims=True))
    out = None
    for lg, vt in parts:
        pv = _dot(vt, jnp.exp2(lg - m).astype(BF16))
        out = pv if out is None else out + pv
    den = out[HEAD_DIM:HEAD_DIM + 1, :] + jnp.exp2(sink - m)
    return out[0:HEAD_DIM, :] / den


def _store_heads(o_ref, rows, j, block, o_t):
    for g in range(GROUP):
        hd = GROUP * j + g
        o_ref[rows, HEAD_DIM * hd:HEAD_DIM * (hd + 1)] = (
            o_t[:, block * g:block * (g + 1)].T.astype(BF16))


def _ctx_attn_kernel(sink_ref, q_ref, k_ref, vt_ref, o_ref, *, layer):
    def logits(n, j):
        rows = slice(SEQ * n, SEQ * (n + 1))
        hs = slice(HEAD_DIM * j, HEAD_DIM * (j + 1))
        return _dot_nt(k_ref[rows, hs], _head_rows(q_ref[rows, :], j)), _with_ones(vt_ref[hs, rows])

    def finish(n, j, s, vt):
        o_t = _softmax_pv_t([(s, vt)], _lane_sink(sink_ref, layer, j, SEQ))
        _store_heads(o_ref, slice(SEQ * n, SEQ * (n + 1)), j, SEQ, o_t)

    _run_chains([(n, j) for n in range(CTX_SEQS_PER_STEP) for j in range(N_KV_HEADS)],
                logits, finish)


def _ctx_attention(sink, q, k, vt, layer):
    t = CTX_SEQS_PER_STEP * SEQ
    row = lambda b: (b, 0)
    return pl.pallas_call(
        functools.partial(_ctx_attn_kernel, layer=layer),
        out_shape=jax.ShapeDtypeStruct((N_CTX, ATTN_W), BF16),
        grid=(N_CTX // t,),
        in_specs=[
            pl.BlockSpec(memory_space=pltpu.SMEM),
            pl.BlockSpec((t, ATTN_W), row),
            pl.BlockSpec((t, KV_W), row),
            pl.BlockSpec((KV_W, t), lambda b: (0, b)),
        ],
        out_specs=pl.BlockSpec((t, ATTN_W), row),
        compiler_params=_params("parallel"),
        name="context_attention",
    )(sink, q, k, vt)


def _band_bias():
    kpos = jnp.arange(BAND)[None, :, None] - jnp.arange(3)[:, None, None] * WINDOW
    qpos = (jnp.arange(GROUP * QBLOCK) & (QBLOCK - 1))[None, None, :]
    return jnp.where(jnp.abs(qpos - kpos) <= WINDOW, 0.0, MASK_VALUE).astype(F32)


def _lat_attn_kernel(sink_ref, q_ref, k_ref, vt_ref, ck_ref, cv_ref, bias_ref, o_ref,
                     kc_ref, vct_ref, *, layer):
    i = pl.program_id(1)

    @pl.when(i == 0)
    def _():
        kc_ref[...] = ck_ref[...].astype(BF16)
        vct = cv_ref[...].T.astype(BF16)
        for j in range(N_KV_HEADS):
            vct_ref[j] = _with_ones(vct[HEAD_DIM * j:HEAD_DIM * (j + 1), :])

    def logits(n, j):
        start = (i * QBLOCKS_PER_STEP + n) * QBLOCK
        ws = pl.multiple_of(jnp.clip(start - WINDOW, 0, DEC_SEQ - BAND), QBLOCK)
        placement = jnp.where(start < WINDOW, 0, jnp.where(start + QBLOCK + WINDOW > DEC_SEQ, 2, 1))
        q4 = _head_rows(q_ref[QBLOCK * n:QBLOCK * (n + 1), :], j)
        hs = slice(HEAD_DIM * j, HEAD_DIM * (j + 1))
        s_ctx = _dot_nt(kc_ref[:, hs], q4)
        s_band = _dot_nt(k_ref[pl.ds(ws, BAND), hs], q4) + bias_ref[placement]
        return s_ctx, s_band, _with_ones(vt_ref[hs, pl.ds(ws, BAND)])

    def finish(n, j, s_ctx, s_band, vbt):
        o_t = _softmax_pv_t([(s_ctx, vct_ref[j]), (s_band, vbt)],
                            _lane_sink(sink_ref, layer, j, QBLOCK))
        _store_heads(o_ref, slice(QBLOCK * n, QBLOCK * (n + 1)), j, QBLOCK, o_t)

    _run_chains([(n, j) for n in range(QBLOCKS_PER_STEP) for j in range(N_KV_HEADS)],
                logits, finish, ahead=2)


def _lat_attention(sink, q, k, vt, cache_k, cache_v, band_bias, layer):
    t = QBLOCKS_PER_STEP * QBLOCK
    steps = DEC_SEQ // t
    seq_block = N_CTX // DEC_SEQ
    return pl.pallas_call(
        functools.partial(_lat_attn_kernel, layer=layer),
        out_shape=jax.ShapeDtypeStruct((N_LAT, ATTN_W), BF16),
        grid=(DEC_BATCH, steps),
        in_specs=[
            pl.BlockSpec(memory_space=pltpu.SMEM),
            pl.BlockSpec((t, ATTN_W), lambda b, i: (N_CTX // t + b * steps + i, 0)),
            pl.BlockSpec((DEC_SEQ, KV_W), lambda b, i: (seq_block + b, 0)),
            pl.BlockSpec((KV_W, DEC_SEQ), lambda b, i: (0, seq_block + b)),
            pl.BlockSpec((None, None, PAST_LEN, KV_W), lambda b, i: (b, layer, 0, 0)),
            pl.BlockSpec((None, None, PAST_LEN, KV_W), lambda b, i: (b, layer, 0, 0)),
            pl.BlockSpec((3, BAND, GROUP * QBLOCK), lambda b, i: (0, 0, 0)),
        ],
        out_specs=pl.BlockSpec((t, ATTN_W), lambda b, i: (b * steps + i, 0)),
        scratch_shapes=[pltpu.VMEM((PAST_LEN, KV_W), BF16),
                        pltpu.VMEM((N_KV_HEADS, HEAD_DIM + ONES_ROWS, PAST_LEN), BF16)],
        compiler_params=_params("parallel", "arbitrary"),
        name="latent_attention",
    )(sink, q, k, vt, cache_k, cache_v, band_bias)


N_GATE_CHUNKS = D_MODEL // MXU_COLS


def _merge_kernel(h_ref, atc_ref, atl_ref, ybc_ref, ybl_ref, x0_ref, *rest, n_x):
    x_refs, mod_ref, rest = rest[0:n_x], rest[n_x], rest[n_x + 1:]
    wga_refs = rest[0:N_GATE_CHUNKS]
    wgb_refs = rest[N_GATE_CHUNKS:2 * N_GATE_CHUNKS]
    wpa_ref, wpb_ref, wo_ref, g_ref, b_ref, xo_ref, ho_ref, mg_ref = rest[2 * N_GATE_CHUNKS:]
    is_ctx = pl.program_id(0) < N_CTX // ROW_TILE
    g1 = mod_ref[:, 2 * D_MODEL:3 * D_MODEL]
    sh2 = mod_ref[:, 3 * D_MODEL:4 * D_MODEL]
    sc2 = mod_ref[:, 4 * D_MODEL:5 * D_MODEL]

    def merge_branches(s):
        rs = slice(SUB_ROWS * s, SUB_ROWS * (s + 1))
        h = h_ref[rs, :]
        attn = jnp.where(is_ctx, atc_ref[rs, :], atl_ref[rs, :])
        conv = jnp.where(is_ctx, ybc_ref[rs, :], ybl_ref[rs, :])
        hy = (x0_ref[rs, :].astype(F32) * conv.astype(F32)).astype(BF16)
        for j in range(N_GATE_CHUNKS):
            cs = slice(MXU_COLS * j, MXU_COLS * (j + 1))
            ga = jax.nn.sigmoid(_dot(h, wga_refs[j][...]))
            gb = jax.nn.sigmoid(_dot(h, wgb_refs[j][...]))
            mg_ref[rs, cs] = (ga * _dot(attn, wpa_ref[:, cs])
                              + gb * _dot(hy, wpb_ref[:, cs])).astype(BF16)
        return ()

    def project_norm(s):
        rs = slice(SUB_ROWS * s, SUB_ROWS * (s + 1))
        sub = _dot(mg_ref[rs, :], wo_ref[...])
        x_in = x_refs[0][rs, :] if n_x == 1 else jnp.where(is_ctx, x_refs[0][rs, :], x_refs[1][rs, :])
        x = _layer_norm(DEEPNORM_ALPHA * x_in + g1 * sub, g_ref[...], b_ref[...])
        xo_ref[rs, :] = x
        ho_ref[rs, :] = (x * (1.0 + sc2) + sh2).astype(BF16)

    _run_chains([(s,) for s in range(ROW_TILE // SUB_ROWS)], merge_branches, project_norm)


def _merge(h, atc, atl, ybc, ybl, x0, x, mods, w_in, wpa, wpb, wo, ln_g, ln_b, layer):
    t = ROW_TILE
    nct = N_CTX // t
    row = lambda m: (m, 0)
    ctx_row = lambda m: (jnp.minimum(m, nct - 1), 0)
    lat_row = lambda m: (jnp.maximum(m - nct, 0), 0)
    gate_block0 = GATE_COL0 // MXU_COLS
    gate_specs = [_weight_spec((D_MODEL, MXU_COLS), (0, gate_block0 + j))
                  for j in range(2 * N_GATE_CHUNKS)]
    x_specs = [pl.BlockSpec((t, D_MODEL), row)] if len(x) == 1 else _split_stream_specs(t)
    return pl.pallas_call(
        functools.partial(_merge_kernel, n_x=len(x)),
        out_shape=(jax.ShapeDtypeStruct((N_TOK, D_MODEL), F32),
                   jax.ShapeDtypeStruct((N_TOK, D_MODEL), BF16)),
        grid=(N_TOK // t,),
        in_specs=[
            pl.BlockSpec((t, D_MODEL), row),
            pl.BlockSpec((t, ATTN_W), ctx_row),
            pl.BlockSpec((t, ATTN_W), lat_row),
            pl.BlockSpec((t, HY_W), ctx_row),
            pl.BlockSpec((t, HY_W), lat_row),
            pl.BlockSpec((t, HY_W), row),
            *x_specs,
            _mod_spec(layer, t),
            *gate_specs,
            _weight_spec((ATTN_W, D_MODEL)),
            _weight_spec((HY_W, D_MODEL)),
            _weight_spec((D_MODEL, D_MODEL)),
            _layer_spec((1, D_MODEL), layer),
            _layer_spec((1, D_MODEL), layer),
        ],
        out_specs=(pl.BlockSpec((t, D_MODEL), row), pl.BlockSpec((t, D_MODEL), row)),
        scratch_shapes=[pltpu.VMEM((t, D_MODEL), BF16)],
        compiler_params=_params("parallel"),
        name="merge_ln1",
    )(h, atc, atl, ybc, ybl, x0, *x, mods, *([w_in] * (2 * N_GATE_CHUNKS)), wpa, wpb, wo,
      ln_g, ln_b)


def _ffn_kernel(h_ref, hp_ref, hn_ref, x_ref, mod_ref, nmod_ref, wg_ref, wv_ref, wd_ref,
                cw_ref, cb_ref, g_ref, b_ref, *rest, last_layer):
    if last_layer:
        yp_ref, ys_ref, hx_ref, a_ref, xo_ref = rest
    else:
        xo_ref, ho_ref, hx_ref, a_ref = rest
    n_sub = ROW_TILE // SUB_ROWS
    ext = SUB_ROWS + 2 * HALO
    is_ctx, seq_start, seq_end = _tile_edges(pl.program_id(0), ROW_TILE)
    _fill_extended(hx_ref, hp_ref, h_ref, hn_ref)
    g2 = mod_ref[:, 5 * D_MODEL:6 * D_MODEL]
    sh = nmod_ref[:, 0:D_MODEL]
    sc = nmod_ref[:, D_MODEL:2 * D_MODEL]

    def up_gate(j):
        rs = slice(SUB_ROWS * j, SUB_ROWS * (j + 1))
        zero_top, zero_bot = _sub_edges(j, n_sub, is_ctx, seq_start, seq_end)
        hx = hx_ref[SUB_ROWS * j:SUB_ROWS * j + ext, :]
        hm = h_ref[rs, :]
        for c in range(D_FF // MXU_COLS):
            cs = slice(MXU_COLS * c, MXU_COLS * (c + 1))
            gate = _dwconv3(_dot(hx, wg_ref[:, cs]), zero_top, zero_bot, cw_ref, cb_ref, cs)
            gelu = 0.5 * gate * (1.0 + lax.erf(gate * math.sqrt(0.5)))
            a_ref[rs, cs] = (gelu * _dot(hm, wv_ref[:, cs])).astype(BF16)
        return ()

    def down_norm(j):
        rs = slice(SUB_ROWS * j, SUB_ROWS * (j + 1))
        sub = _dot(a_ref[rs, :], wd_ref[...])
        x = _layer_norm(DEEPNORM_ALPHA * x_ref[rs, :] + g2 * sub, g_ref[...], b_ref[...])
        xo_ref[rs, :] = x
        if not last_layer:
            ho_ref[rs, :] = (x * (1.0 + sc) + sh).astype(BF16)

    _run_chains([(j,) for j in range(n_sub)], up_gate, down_norm)

    if last_layer:
        @pl.when(is_ctx)
        def _():
            yp_ref[...] = xo_ref[...]

        @pl.when(jnp.logical_not(is_ctx))
        def _():
            ys_ref[...] = xo_ref[...]


def _ffn(h, x, mods, w_up, w_down, conv_w, conv_b, ln_g, ln_b, layer):
    t = ROW_TILE
    nct = N_CTX // t
    row = lambda m: (m, 0)
    last_layer = layer == DEPTH - 1
    if last_layer:
        out_shape = (jax.ShapeDtypeStruct((N_CTX, D_MODEL), F32),
                     jax.ShapeDtypeStruct((N_LAT, D_MODEL), F32))
        out_specs = (pl.BlockSpec((t, D_MODEL), lambda m: (jnp.minimum(m, nct - 1), 0)),
                     pl.BlockSpec((t, D_MODEL), lambda m: (jnp.maximum(m - nct, 0), 0)))
        scratch = [pltpu.VMEM((t, D_MODEL), F32)]
    else:
        out_shape = (jax.ShapeDtypeStruct((N_TOK, D_MODEL), F32),
                     jax.ShapeDtypeStruct((N_TOK, D_MODEL), BF16))
        out_specs = (pl.BlockSpec((t, D_MODEL), row), pl.BlockSpec((t, D_MODEL), row))
        scratch = []
    return pl.pallas_call(
        functools.partial(_ffn_kernel, last_layer=last_layer),
        out_shape=out_shape,
        grid=(N_TOK // t,),
        in_specs=[
            pl.BlockSpec((t, D_MODEL), row),
            *_halo_specs(t, D_MODEL),
            pl.BlockSpec((t, D_MODEL), row),
            _mod_spec(layer, t),
            _mod_spec(min(layer + 1, DEPTH - 1), t),
            _weight_spec((D_MODEL, D_FF), (0, 0)),
            _weight_spec((D_MODEL, D_FF), (0, 1)),
            _weight_spec((D_FF, D_MODEL)),
            _layer_spec((3, D_FF), layer),
            _layer_spec((1, D_FF), layer),
            _layer_spec((1, D_MODEL), layer),
            _layer_spec((1, D_MODEL), layer),
        ],
        out_specs=out_specs,
        scratch_shapes=[pltpu.VMEM((t + 2 * HALO, D_MODEL), BF16), pltpu.VMEM((t, D_FF), BF16),
                        *scratch],
        compiler_params=_params("arbitrary"),
        name="conv_ffn_ln2",
    )(h, h, h, x, mods, mods, w_up, w_up, w_down, conv_w, conv_b, ln_g, ln_b)


def kernel(x_prompt, x_sample, cache_k, cache_v, c, c_ctx, w_ada, b_ada, w_in, attn_sink,
           hy_conv_w, hy_conv_b, filt_w1, filt_b1, filt_freq1, filt_w2, filt_b2, filt_freq2,
           filt_w3, hy_bias, w_pa, w_pb, w_out, ln1_g, ln1_b, w_up, ffn_conv_w, ffn_conv_b,
           w_down, ln2_g, ln2_b):
    cond = jnp.concatenate([c_ctx[None, :], c,
                            jnp.zeros((N_MOD_ROWS - 1 - DEC_BATCH, D_MODEL), F32)], axis=0)
    mods = _modulation(cond, w_ada, b_ada).reshape(DEPTH, N_MOD_ROWS, 1, 6 * D_MODEL)

    x = (x_prompt.reshape(N_CTX, D_MODEL), x_sample.reshape(N_LAT, D_MODEL))
    h = _prologue(*x, mods)

    cos_tab, sin_tab = _rope_tables()
    conv_args = {}
    for seq in (SEQ, DEC_SEQ):
        tables = _dft_tables(seq)
        hs = _filters(seq, filt_w1, filt_b1, filt_freq1, filt_w2, filt_b2, filt_freq2, filt_w3)
        conv_args[seq] = (tables, _filter_spectrum(seq, tables[0], hs))

    cache_k = cache_k.reshape(DEC_BATCH, DEPTH, PAST_LEN, KV_W)
    cache_v = cache_v.reshape(DEC_BATCH, DEPTH, PAST_LEN, KV_W)
    hy_conv_b3 = hy_conv_b.reshape(DEPTH, 1, 3 * HY_W)
    ffn_conv_b3 = ffn_conv_b.reshape(DEPTH, 1, D_FF)
    ln1_g3, ln1_b3 = ln1_g.reshape(DEPTH, 1, D_MODEL), ln1_b.reshape(DEPTH, 1, D_MODEL)
    ln2_g3, ln2_b3 = ln2_g.reshape(DEPTH, 1, D_MODEL), ln2_b.reshape(DEPTH, 1, D_MODEL)

    band_bias = _band_bias()
    w_in_l = w_in[0:1].astype(BF16)
    keys, values = [], []
    for l in range(DEPTH):
        to_cast = [(w, l) for w in (w_pa, w_pb, w_out, w_up, w_down)]
        if l + 1 < DEPTH:
            to_cast.append((w_in, l + 1))
        q, k, vt, kf, vf, x0, vh, w_pa_l, w_pb_l, w_out_l, w_up_l, w_down_l, *w_in_next = _in_proj(
            h, w_in_l, cos_tab, sin_tab, hy_conv_w, hy_conv_b3, to_cast, l)
        keys.append(kf)
        values.append(vf)

        ybc = _long_conv(SEQ, N_CTX, 0, vh, *conv_args[SEQ], hy_bias, l)
        ybl = _long_conv(DEC_SEQ, N_LAT, N_CTX, vh, *conv_args[DEC_SEQ], hy_bias, l)
        atc = _ctx_attention(attn_sink, q, k, vt, l)
        atl = _lat_attention(attn_sink, q, k, vt, cache_k, cache_v, band_bias, l)

        x1, h2 = _merge(h, atc, atl, ybc, ybl, x0, x, mods, w_in_l, w_pa_l, w_pb_l, w_out_l,
                        ln1_g3, ln1_b3, l)
        x2, h = _ffn(h2, x1, mods, w_up_l, w_down_l, ffn_conv_w, ffn_conv_b3, ln2_g3, ln2_b3, l)
        x = (x2,)
        if w_in_next:
            w_in_l = w_in_next[0]

    y_prompt_rows, y_sample_rows = x2, h
    y_prompt = y_prompt_rows.reshape(BATCH, SEQ, D_MODEL)
    y_sample = y_sample_rows.reshape(DEC_BATCH, DEC_SEQ, D_MODEL)
    to_cache = lambda ts: jnp.stack(ts, axis=1).reshape(BATCH, DEPTH, SEQ, N_KV_HEADS, HEAD_DIM)
    new_k = to_cache([t.reshape(BATCH, SEQ, KV_W) for t in keys])
    new_v = to_cache([t.reshape(BATCH, SEQ, KV_W) for t in values])
    return (y_prompt, y_sample, new_k, new_v)
```

```python
import functools
import math

import jax
import jax.numpy as jnp
from jax import lax
from jax.experimental import pallas as pl
from jax.experimental.pallas import tpu as pltpu

D_MODEL = 1024
BATCH = 16
SEQ = 256
DEPTH = 4
DEC_BATCH = 4
DEC_SEQ = 2048
PAST_LEN = 512
GRID_W = 64
HEAD_DIM = 64
N_HEADS = 8
N_KV_HEADS = 2
GROUP = N_HEADS // N_KV_HEADS
ATTN_W = N_HEADS * HEAD_DIM
KV_W = N_KV_HEADS * HEAD_DIM
WINDOW = 128
QBLOCK = 128
HY_W = 512
FILT_EMB = 33
FILT_EMB_PAD = 40
FILT_BANDS = (FILT_EMB - 1) // 2
FILT_HIDDEN = 64
D_FF = 2816
ROPE_BASE = 10000.0
LN_EPS = 1e-5
DEEPNORM_ALPHA = (2 * DEPTH) ** 0.25

N_CTX = BATCH * SEQ
N_LAT = DEC_BATCH * DEC_SEQ
N_TOK = N_CTX + N_LAT
N_MOD_ROWS = 8
QKV_W = ATTN_W + 2 * KV_W
HY_BLOCK = QKV_W
assert 3 * HY_W == 2 * HY_BLOCK
GATE_COL0 = QKV_W + 3 * HY_W
BAND = QBLOCK + 2 * WINDOW
MASK_VALUE = -1e30
LOG2_E = math.log2(math.e)

F32 = jnp.float32
BF16 = jnp.bfloat16

VMEM_LIMIT_BYTES = 56 * 1024 * 1024
MXU_COLS = 256
ROW_TILE = 1024
SUB_ROWS = SEQ
HALO = 16
RESIDENT = dict(pipeline_mode=pl.Buffered(1))
CAST_BLOCKS = 8


def _params(*semantics):
    return pltpu.CompilerParams(dimension_semantics=semantics, vmem_limit_bytes=VMEM_LIMIT_BYTES)


def _dot(a, b):
    return jnp.dot(a, b, preferred_element_type=F32)


def _dot_split(a, b):
    a_hi = a.astype(BF16)
    b_hi = b.astype(BF16)
    a_lo = (a - a_hi.astype(F32)).astype(BF16)
    b_lo = (b - b_hi.astype(F32)).astype(BF16)
    return _dot(a_hi, b_hi) + (_dot(a_hi, b_lo) + _dot(a_lo, b_hi))


def _dot_nt(a, b):
    return lax.dot_general(a, b, (((1,), (1,)), ((), ())), preferred_element_type=F32)


def _mod_row(tile_rows):
    n_ctx_tiles = N_CTX // tile_rows
    tiles_per_seq = DEC_SEQ // tile_rows

    def fn(m):
        return jnp.where(m < n_ctx_tiles, 0, 1 + (m - n_ctx_tiles) // tiles_per_seq)

    return fn


def _mod_spec(layer, tile_rows):
    grp = _mod_row(tile_rows)
    return pl.BlockSpec((None, None, 1, 6 * D_MODEL), lambda m, *_: (layer, grp(m), 0, 0))


def _layer_spec(shape, layer):
    zeros = (0,) * len(shape)
    return pl.BlockSpec((None,) + tuple(shape), lambda *_: (layer,) + zeros)


def _weight_spec(shape, block_index=None):
    block_index = block_index or (0,) * len(shape)
    return pl.BlockSpec((None,) + tuple(shape), lambda *_: (0,) + tuple(block_index), **RESIDENT)


def _layer_norm(y, g, b):
    mu = jnp.mean(y, axis=-1, keepdims=True)
    yc = y - mu
    var = jnp.mean(yc * yc, axis=-1, keepdims=True)
    return yc * lax.rsqrt(var + LN_EPS) * g + b


def _run_chains(chains, first_fn, finish_fn, ahead=1):
    pending = [first_fn(*chain) for chain in chains[:ahead]]
    for c, chain in enumerate(chains):
        if c + ahead < len(chains):
            pending.append(first_fn(*chains[c + ahead]))
        finish_fn(*chain, *pending.pop(0))


def _tile_edges(m, tile_rows):
    n_ctx_tiles = N_CTX // tile_rows
    tiles_per_seq = DEC_SEQ // tile_rows
    is_ctx = m < n_ctx_tiles
    lat_pos = (m - n_ctx_tiles) % tiles_per_seq
    return (is_ctx, jnp.logical_or(is_ctx, lat_pos == 0),
            jnp.logical_or(is_ctx, lat_pos == tiles_per_seq - 1))


def _halo_specs(tile_rows, n_cols):
    tb = tile_rows // HALO
    last = N_TOK // HALO - 1
    return (pl.BlockSpec((HALO, n_cols), lambda m: (jnp.maximum(m * tb - 1, 0), 0)),
            pl.BlockSpec((HALO, n_cols), lambda m: (jnp.minimum((m + 1) * tb, last), 0)))


def _fill_extended(hx_ref, hp_ref, h_ref, hn_ref):
    rows = h_ref.shape[0]
    hx_ref[0:HALO, :] = hp_ref[...]
    hx_ref[HALO:HALO + rows, :] = h_ref[...]
    hx_ref[HALO + rows:2 * HALO + rows, :] = hn_ref[...]


def _sub_edges(j, n_sub, is_ctx, seq_start, seq_end):
    return (seq_start if j == 0 else is_ctx), (seq_end if j == n_sub - 1 else is_ctx)


def _dwconv3(z, zero_top, zero_bot, w_ref, b_ref, cs):
    n = z.shape[0] - 2 * HALO
    top = jnp.where(zero_top, 0.0, z[HALO - 8:HALO])
    bot = jnp.where(zero_bot, 0.0, z[HALO + n:HALO + n + 8])
    z = jnp.concatenate([z[0:HALO - 8], top, z[HALO:HALO + n], bot, z[HALO + n + 8:]], axis=0)
    prev = pltpu.roll(z, 1, 0)[HALO:HALO + n]
    nxt = pltpu.roll(z, z.shape[0] - 1, 0)[HALO:HALO + n]
    return (prev * w_ref[0:1, cs] + z[HALO:HALO + n] * w_ref[1:2, cs] + nxt * w_ref[2:3, cs]
            + b_ref[:, cs])


def _mod_kernel(cond_ref, w_ref, b_ref, o_ref):
    c = cond_ref[...]
    s = (c * jax.nn.sigmoid(c)).astype(BF16)
    o_ref[...] = _dot(s, w_ref[...].astype(BF16)) + b_ref[...]


def _modulation(cond, w_ada, b_ada):
    n_col = 6 * D_MODEL // D_MODEL
    return pl.pallas_call(
        _mod_kernel,
        out_shape=jax.ShapeDtypeStruct((DEPTH, N_MOD_ROWS, 6 * D_MODEL), F32),
        grid=(DEPTH, n_col),
        in_specs=[
            pl.BlockSpec((N_MOD_ROWS, D_MODEL), lambda l, j: (0, 0)),
            pl.BlockSpec((None, D_MODEL, D_MODEL), lambda l, j: (l, 0, j)),
            pl.BlockSpec((None, 1, D_MODEL), lambda l, j: (l, 0, j)),
        ],
        out_specs=pl.BlockSpec((None, N_MOD_ROWS, D_MODEL), lambda l, j: (l, 0, j)),
        compiler_params=_params("parallel", "parallel"),
        name="modulation",
    )(cond, w_ada, b_ada.reshape(DEPTH, 1, 6 * D_MODEL))


PRO_TILE = 1024


def _prologue_kernel(xp_ref, xs_ref, mod_ref, h_ref):
    m = pl.program_id(0)

    def emit(x):
        sh = mod_ref[:, 0:D_MODEL]
        sc = mod_ref[:, D_MODEL:2 * D_MODEL]
        h_ref[...] = (x * (1.0 + sc) + sh).astype(BF16)

    @pl.when(m < N_CTX // PRO_TILE)
    def _():
        emit(xp_ref[...])

    @pl.when(m >= N_CTX // PRO_TILE)
    def _():
        emit(xs_ref[...])


def _split_stream_specs(tile_rows):
    nct = N_CTX // tile_rows
    return (pl.BlockSpec((tile_rows, D_MODEL), lambda m: (jnp.minimum(m, nct - 1), 0)),
            pl.BlockSpec((tile_rows, D_MODEL), lambda m: (jnp.maximum(m - nct, 0), 0)))


def _prologue(xp, xs, mods):
    return pl.pallas_call(
        _prologue_kernel,
        out_shape=jax.ShapeDtypeStruct((N_TOK, D_MODEL), BF16),
        grid=(N_TOK // PRO_TILE,),
        in_specs=[*_split_stream_specs(PRO_TILE), _mod_spec(0, PRO_TILE)],
        out_specs=pl.BlockSpec((PRO_TILE, D_MODEL), lambda m: (m, 0)),
        compiler_params=_params("parallel"),
        name="prologue",
    )(xp, xs, mods)


ROPE_LANES = 2 * HEAD_DIM


N_IN_PROJ_OUTS = 7


def _in_proj_kernel(h_ref, hp_ref, hn_ref, wqkv_ref, why0_ref, why1_ref, cos_ref, sin_ref,
                    cw_ref, cb_ref, *rest):
    n_cast = (len(rest) - N_IN_PROJ_OUTS - 2) // 2
    cast_src = rest[0:n_cast]
    q_ref, k_ref, vt_ref, kf_ref, vf_ref, x0_ref, vh_ref = rest[n_cast:n_cast + N_IN_PROJ_OUTS]
    cast_dst = rest[n_cast + N_IN_PROJ_OUTS:-2]
    hx_ref, kv_ref = rest[-2:]

    @pl.when(pl.program_id(0) < CAST_BLOCKS)
    def _():
        for src, dst in zip(cast_src, cast_dst):
            dst[...] = src[...].astype(BF16)

    n_sub = ROW_TILE // SUB_ROWS
    ext = SUB_ROWS + 2 * HALO
    is_ctx, seq_start, seq_end = _tile_edges(pl.program_id(0), ROW_TILE)
    _fill_extended(hx_ref, hp_ref, h_ref, hn_ref)
    lane = lax.broadcasted_iota(jnp.int32, (SUB_ROWS, ROPE_LANES), 1)
    first_half = (lane & (HEAD_DIM // 4)) == 0
    scale = HEAD_DIM ** -0.5 * LOG2_E
    for j in range(n_sub):
        rs = slice(SUB_ROWS * j, SUB_ROWS * (j + 1))
        cos = cos_ref[rs, :]
        sin = sin_ref[rs, :]

        def rope(x):
            partner = jnp.where(first_half, pltpu.roll(x, ROPE_LANES - HEAD_DIM // 4, 1),
                                pltpu.roll(x, HEAD_DIM // 4, 1))
            return x * cos + partner * sin

        hm = h_ref[rs, :]
        for jj in range(ATTN_W // MXU_COLS):
            qq = _dot(hm, wqkv_ref[:, MXU_COLS * jj:MXU_COLS * (jj + 1)])
            for i in range(MXU_COLS // ROPE_LANES):
                c0 = MXU_COLS * jj + ROPE_LANES * i
                q_ref[rs, c0:c0 + ROPE_LANES] = (
                    rope(qq[:, ROPE_LANES * i:ROPE_LANES * (i + 1)]) * scale).astype(BF16)
        kv = _dot(hm, wqkv_ref[:, ATTN_W:QKV_W])
        k = kv[:, 0:KV_W]
        v = kv[:, KV_W:2 * KV_W]
        kv_ref[0, rs, :] = k
        kv_ref[1, rs, :] = v
        k_ref[rs, :] = rope(k).astype(BF16)
        vt_ref[:, rs] = v.T.astype(BF16)

        zero_top, zero_bot = _sub_edges(j, n_sub, is_ctx, seq_start, seq_end)
        hx = hx_ref[SUB_ROWS * j:SUB_ROWS * j + ext, :]

        def proj_conv(c0):
            w_ref = why0_ref if c0 < HY_BLOCK else why1_ref
            w = w_ref[:, c0 % HY_BLOCK:c0 % HY_BLOCK + MXU_COLS]
            return _dwconv3(_dot(hx, w), zero_top, zero_bot, cw_ref, cb_ref,
                            slice(c0, c0 + MXU_COLS))

        for jj in range(HY_W // MXU_COLS):
            c0 = MXU_COLS * jj
            x0_ref[rs, c0:c0 + MXU_COLS] = proj_conv(c0).astype(BF16)
            x1 = proj_conv(HY_W + c0)
            u = proj_conv(2 * HY_W + c0)
            vh_ref[rs, c0:c0 + MXU_COLS] = (x1 * u).astype(BF16)

    @pl.when(is_ctx)
    def _():
        kf_ref[...] = kv_ref[0]
        vf_ref[...] = kv_ref[1]


def _in_proj(h, w_in, cos_tab, sin_tab, conv_w, conv_b, to_cast, layer):
    t = ROW_TILE
    nct = N_CTX // t
    tps = DEC_SEQ // t
    tab = lambda m: (jnp.where(m < nct, 0, 1 + (m - nct) % tps), 0, 0)
    row = lambda m: (m, 0)
    ctx_row = lambda m: (jnp.minimum(m, nct - 1), 0)
    bf = lambda w: jax.ShapeDtypeStruct((N_TOK, w), BF16)
    cast_block = lambda m: jnp.minimum(m, CAST_BLOCKS - 1)
    cast_in, cast_out, cast_shapes = [], [], []
    for w, w_layer in to_cast:
        rows, cols = w.shape[1] // CAST_BLOCKS, w.shape[2]
        cast_in.append(pl.BlockSpec((None, rows, cols),
                                    lambda m, w_layer=w_layer: (w_layer, cast_block(m), 0)))
        cast_out.append(pl.BlockSpec((None, rows, cols), lambda m: (0, cast_block(m), 0)))
        cast_shapes.append(jax.ShapeDtypeStruct((1,) + w.shape[1:], BF16))
    return pl.pallas_call(
        _in_proj_kernel,
        out_shape=(bf(ATTN_W), bf(KV_W),
                   jax.ShapeDtypeStruct((KV_W, N_TOK), BF16),
                   jax.ShapeDtypeStruct((N_CTX, KV_W), F32),
                   jax.ShapeDtypeStruct((N_CTX, KV_W), F32),
                   bf(HY_W), bf(HY_W), *cast_shapes),
        grid=(N_TOK // t,),
        in_specs=[
            pl.BlockSpec((t, D_MODEL), row),
            *_halo_specs(t, D_MODEL),
            _weight_spec((D_MODEL, QKV_W), (0, 0)),
            _weight_spec((D_MODEL, HY_BLOCK), (0, 1)),
            _weight_spec((D_MODEL, HY_BLOCK), (0, 2)),
            pl.BlockSpec((None, t, ROPE_LANES), tab),
            pl.BlockSpec((None, t, ROPE_LANES), tab),
            _layer_spec((3, 3 * HY_W), layer),
            _layer_spec((1, 3 * HY_W), layer),
            *cast_in,
        ],
        out_specs=(pl.BlockSpec((t, ATTN_W), row),
                   pl.BlockSpec((t, KV_W), row),
                   pl.BlockSpec((KV_W, t), lambda m: (0, m)),
                   pl.BlockSpec((t, KV_W), ctx_row),
                   pl.BlockSpec((t, KV_W), ctx_row),
                   pl.BlockSpec((t, HY_W), row),
                   pl.BlockSpec((t, HY_W), row),
                   *cast_out),
        scratch_shapes=[pltpu.VMEM((t + 2 * HALO, D_MODEL), BF16),
                        pltpu.VMEM((2, t, KV_W), F32)],
        compiler_params=_params("arbitrary"),
        name="in_proj",
    )(h, h, h, w_in, w_in, w_in, cos_tab, sin_tab, conv_w, conv_b, *(w for w, _ in to_cast))


def _rope_tables():
    rows = DEC_SEQ // GRID_W
    r, col = jnp.meshgrid(jnp.arange(rows), jnp.arange(GRID_W), indexing='ij')
    pos = jnp.stack([r.reshape(-1), col.reshape(-1)], axis=-1).astype(F32)
    half = HEAD_DIM // 2
    inv_freq = 1.0 / (ROPE_BASE ** (jnp.arange(0, half, 2, dtype=F32) / half))
    ang = pos[:, :, None] * inv_freq
    ang = jnp.stack([ang, ang], axis=-2).reshape(DEC_SEQ, HEAD_DIM)
    cos = jnp.tile(jnp.cos(ang), (1, ROPE_LANES // HEAD_DIM))
    sin = jnp.tile(jnp.sin(ang), (1, ROPE_LANES // HEAD_DIM))
    lane = jnp.arange(ROPE_LANES)
    sin = jnp.where((lane & (HEAD_DIM // 4)) == 0, -sin, sin)
    n_slab = DEC_SEQ // ROW_TILE
    cos = jnp.concatenate([jnp.ones((1, ROW_TILE, ROPE_LANES), F32),
                           cos.reshape(n_slab, ROW_TILE, ROPE_LANES)], axis=0)
    sin = jnp.concatenate([jnp.zeros((1, ROW_TILE, ROPE_LANES), F32),
                           sin.reshape(n_slab, ROW_TILE, ROPE_LANES)], axis=0)
    return cos, sin


def _filter_kernel(feat_ref, t_ref, dec_ref, w1_ref, b1_ref, f1_ref, w2_ref, b2_ref, f2_ref,
                   w3_ref, o_ref):
    a = _dot_split(w1_ref[...], feat_ref[...]) + b1_ref[...]
    a = jnp.sin(f1_ref[...] * a)
    a = _dot_split(w2_ref[...], a) + b2_ref[...]
    a = jnp.sin(f2_ref[...] * a)
    hh = _dot_split(a.T, w3_ref[...])
    window = jnp.exp(-t_ref[...] * dec_ref[...])
    h_fwd = hh[:, 0:HY_W] * window
    h_bwd = hh[:, HY_W:2 * HY_W] * window
    row = lax.broadcasted_iota(jnp.int32, h_bwd.shape, 0)
    h_bwd = jnp.where(row == 0, 0.0, h_bwd)
    o_ref[:, 0:HY_W] = (h_fwd + h_bwd).astype(BF16)
    o_ref[:, HY_W:2 * HY_W] = (h_fwd - h_bwd).astype(BF16)


def _filters(seq, w1, b1, f1, w2, b2, f2, w3):
    t = jnp.linspace(0.0, 1.0, seq, dtype=F32)
    w = 2.0 * math.pi * jnp.arange(seq, dtype=F32) / seq
    f = jnp.linspace(1e-4, FILT_BANDS - 1, FILT_BANDS, dtype=F32)
    zr = f[:, None] * w[None, :]
    feats_t = jnp.concatenate([t[None, :], jnp.cos(zr), -jnp.sin(zr),
                               jnp.zeros((FILT_EMB_PAD - FILT_EMB, seq), F32)], axis=0)
    target = 1e-2
    decay = jnp.abs(jnp.linspace(math.log(target) / 1.5, math.log(target) / 0.3, HY_W,
                                 dtype=F32))[None, :]
    w1t = jnp.pad(w1, ((0, 0), (0, FILT_EMB_PAD - FILT_EMB), (0, 0))).transpose(0, 2, 1)
    col = lambda a: a.reshape(DEPTH, FILT_HIDDEN, 1)
    col_spec = pl.BlockSpec((None, FILT_HIDDEN, 1), lambda l: (l, 0, 0))
    return pl.pallas_call(
        _filter_kernel,
        out_shape=jax.ShapeDtypeStruct((DEPTH, seq, 2 * HY_W), BF16),
        grid=(DEPTH,),
        in_specs=[
            pl.BlockSpec((FILT_EMB_PAD, seq), lambda l: (0, 0)),
            pl.BlockSpec((seq, 1), lambda l: (0, 0)),
            pl.BlockSpec((1, HY_W), lambda l: (0, 0)),
            pl.BlockSpec((None, FILT_HIDDEN, FILT_EMB_PAD), lambda l: (l, 0, 0)),
            col_spec,
            col_spec,
            pl.BlockSpec((None, FILT_HIDDEN, FILT_HIDDEN), lambda l: (l, 0, 0)),
            col_spec,
            col_spec,
            pl.BlockSpec((None, FILT_HIDDEN, 2 * HY_W), lambda l: (l, 0, 0)),
        ],
        out_specs=pl.BlockSpec((None, seq, 2 * HY_W), lambda l: (l, 0, 0)),
        compiler_params=_params("parallel"),
        name=f"hyena_filter_{seq}",
    )(feats_t, t[:, None], decay, w1t, col(b1), col(f1), w2.transpose(0, 2, 1), col(b2), col(f2), w3)


LONGCONV_RADIX = {SEQ: 2, DEC_SEQ: 4}
LONGCONV_SEQS_PER_STEP = {SEQ: 4, DEC_SEQ: 1}
CHAINS_IN_FLIGHT = 2
TABLE_ROWS = 16
LANES = 128


def _freq_classes(radix):
    return [(k, m) for k in range(radix // 2) for m in range(2)]


def _rot_power(k, m, r):
    return (3 * k * r + 2 * m * r) % 4


def _add_rotated(acc, z, p):
    zr, zi = z
    re, im, s_re, s_im = ((zr, zi, 1, 1), (zi, zr, -1, 1), (zr, zi, -1, -1), (zi, zr, 1, -1))[p]
    if acc is None:
        return (re if s_re > 0 else -re, im if s_im > 0 else -im)
    return (acc[0] + re if s_re > 0 else acc[0] - re, acc[1] + im if s_im > 0 else acc[1] - im)


def _class_spectra(parts, radix):
    out = []
    for k, m in _freq_classes(radix):
        acc = None
        for r in range(radix):
            acc = _add_rotated(acc, parts[r], _rot_power(k, m, r))
        out.append((acc[0], -acc[1]) if m else acc)
    return out


def _phase(a, b, period):
    k = (a * b) & (period - 1)
    ang = k.astype(F32) * (2.0 * math.pi / period)
    return jnp.cos(ang), jnp.sin(ang)


def _table_kernel(o_ref, *, n, period, row_mult, row_add, col_mult, col_add, stack_rows, scale):
    groups = n // TABLE_ROWS
    b0 = col_mult * lax.broadcasted_iota(jnp.int32, (TABLE_ROWS, n), 1) + col_add
    a0 = row_mult * lax.broadcasted_iota(jnp.int32, (TABLE_ROWS, n), 0) + row_add
    cos_a, sin_a = _phase(a0, b0, period)
    b1 = col_mult * lax.broadcasted_iota(jnp.int32, (groups, n), 1) + col_add
    a1 = row_mult * TABLE_ROWS * lax.broadcasted_iota(jnp.int32, (groups, n), 0)
    cos_b, sin_b = _phase(a1, b1, period)
    cos = cos_a[None] * cos_b[:, None, :] - sin_a[None] * sin_b[:, None, :]
    msin = -(sin_a[None] * cos_b[:, None, :] + cos_a[None] * sin_b[:, None, :])
    cos = (cos.reshape(n, n) * scale).astype(BF16)
    msin = (msin.reshape(n, n) * scale).astype(BF16)
    if stack_rows:
        o_ref[0:n, :] = cos
        o_ref[n:2 * n, :] = msin
    else:
        o_ref[:, 0:n] = cos
        o_ref[:, n:2 * n] = msin


def _dft_tables(seq):
    radix = LONGCONV_RADIX[seq]
    n = seq // radix

    def table(stack_rows, **kw):
        shape = (2 * n, n) if stack_rows else (n, 2 * n)
        return pl.pallas_call(
            functools.partial(_table_kernel, n=n, period=4 * seq, stack_rows=stack_rows, **kw),
            out_shape=jax.ShapeDtypeStruct(shape, BF16),
            compiler_params=_params(),
            name=f"dft_table_{seq}",
        )()

    fwd = [table(True, row_mult=2, row_add=1, col_mult=radix, col_add=r, scale=1.0)
           for r in range(radix)]
    inv = [table(False, row_mult=radix, row_add=r, col_mult=2, col_add=1, scale=1.0 / seq)
           for r in range(radix)]
    return fwd, inv


def _split_rows(x, nat_ref, radix):
    n = x.shape[0] // radix
    k = x.shape[1] // LANES
    for i in range(k):
        nat_ref[i] = x[:, LANES * i:LANES * (i + 1)]
    return [jnp.concatenate([nat_ref[i, pl.ds(r, n, stride=radix), :] for i in range(k)], axis=1)
            for r in range(radix)]


def _interleave_rows(parts, nat_ref):
    radix = len(parts)
    n = parts[0].shape[0]
    k = parts[0].shape[1] // LANES
    for i in range(k):
        for r in range(radix):
            nat_ref[i, pl.ds(r, n, stride=radix), :] = parts[r][:, LANES * i:LANES * (i + 1)]
    return jnp.concatenate([nat_ref[i] for i in range(k)], axis=1)


def _transform_parts(fwd_refs, parts, n):
    out = []
    for tab_ref, x in zip(fwd_refs, parts):
        p = _dot(tab_ref[...], x.astype(BF16))
        out.append((p[0:n], p[n:2 * n]))
    return out


def _spectrum_kernel(*refs, radix, n):
    fwd_refs, (h_ref, g_ref, nat_ref) = refs[0:radix], refs[radix:]
    parts = _split_rows(h_ref[...].astype(F32), nat_ref, radix)
    w = HY_W
    sum_spec = _class_spectra(_transform_parts(fwd_refs, [x[:, 0:w] for x in parts], n), radix)
    diff_spec = _class_spectra(_transform_parts(fwd_refs, [x[:, w:2 * w] for x in parts], n), radix)
    for c in range(radix):
        g_ref[2 * c] = sum_spec[c][0]
        g_ref[2 * c + 1] = diff_spec[c][1]


def _filter_spectrum(seq, fwd, hs):
    radix = LONGCONV_RADIX[seq]
    n = seq // radix
    table = pl.BlockSpec((2 * n, n), lambda l: (0, 0))
    return pl.pallas_call(
        functools.partial(_spectrum_kernel, radix=radix, n=n),
        out_shape=jax.ShapeDtypeStruct((DEPTH, 2 * radix, n, HY_W), F32),
        grid=(DEPTH,),
        in_specs=[*([table] * radix), pl.BlockSpec((None, seq, 2 * HY_W), lambda l: (l, 0, 0))],
        out_specs=pl.BlockSpec((None, 2 * radix, n, HY_W), lambda l: (l, 0, 0, 0)),
        scratch_shapes=[pltpu.VMEM((2 * HY_W // LANES, seq, LANES), F32)],
        compiler_params=_params("parallel"),
        name=f"hyena_spectrum_{seq}",
    )(*fwd, hs)


def _longconv_kernel(*refs, radix, n, n_seq):
    v_ref = refs[0]
    fwd_refs = refs[1:1 + radix]
    inv_refs = refs[1 + radix:1 + 2 * radix]
    g_ref, bias_ref, o_ref, vin_ref, out_ref = refs[1 + 2 * radix:]
    seq = radix * n
    classes = _freq_classes(radix)

    def forward(s, c, slot):
        v = v_ref[seq * s:seq * (s + 1), MXU_COLS * c:MXU_COLS * (c + 1)].astype(F32)
        return (_transform_parts(fwd_refs, _split_rows(v, vin_ref.at[slot], radix), n),)

    def finish(s, c, slot, parts):
        rows = slice(seq * s, seq * (s + 1))
        cs = slice(MXU_COLS * c, MXU_COLS * (c + 1))
        prods = []
        for i, (ur, ui) in enumerate(_class_spectra(parts, radix)):
            gr, gi = g_ref[2 * i, :, cs], g_ref[2 * i + 1, :, cs]
            yr, yi = ur * gr - ui * gi, ur * gi + ui * gr
            prods.append((yr, -yi) if classes[i][1] else (yr, yi))
        outs = []
        for r in range(radix):
            acc = None
            for (k, m), y in zip(classes, prods):
                acc = _add_rotated(acc, y, (-_rot_power(k, m, r)) % 4)
            spec = jnp.concatenate(acc, axis=0).astype(BF16)
            outs.append(_dot(inv_refs[r][...], spec))
        y = _interleave_rows(outs, out_ref.at[slot])
        v = jnp.concatenate([vin_ref[slot, i] for i in range(MXU_COLS // LANES)], axis=1)
        o_ref[rows, cs] = (y + bias_ref[:, cs] * v).astype(BF16)

    chains = [(s, c) for s in range(n_seq) for c in range(HY_W // MXU_COLS)]
    _run_chains([(s, c, i % CHAINS_IN_FLIGHT) for i, (s, c) in enumerate(chains)], forward, finish)


def _long_conv(seq, n_rows, row0, vh, tables, spectrum, hy_bias, layer):
    radix = LONGCONV_RADIX[seq]
    n = seq // radix
    n_seq = LONGCONV_SEQS_PER_STEP[seq]
    t = n_seq * seq
    fwd, inv = tables
    scratch = pltpu.VMEM((CHAINS_IN_FLIGHT, MXU_COLS // LANES, seq, LANES), F32)
    return pl.pallas_call(
        functools.partial(_longconv_kernel, radix=radix, n=n, n_seq=n_seq),
        out_shape=jax.ShapeDtypeStruct((n_rows, HY_W), BF16),
        grid=(n_rows // t,),
        in_specs=[
            pl.BlockSpec((t, HY_W), lambda b: (row0 // t + b, 0)),
            *([pl.BlockSpec((2 * n, n), lambda b: (0, 0), **RESIDENT)] * radix),
            *([pl.BlockSpec((n, 2 * n), lambda b: (0, 0), **RESIDENT)] * radix),
            pl.BlockSpec((None, 2 * radix, n, HY_W), lambda b: (layer, 0, 0, 0), **RESIDENT),
            pl.BlockSpec((None, 1, HY_W), lambda b: (layer, 0, 0)),
        ],
        out_specs=pl.BlockSpec((t, HY_W), lambda b: (b, 0)),
        scratch_shapes=[scratch, scratch],
        compiler_params=_params("parallel"),
        name=f"hyena_longconv_{seq}",
    )(vh, *fwd, *inv, spectrum, hy_bias.reshape(DEPTH, 1, HY_W))


ONES_ROWS = 16
CTX_SEQS_PER_STEP = 4
QBLOCKS_PER_STEP = DEC_SEQ // QBLOCK


def _head_rows(q, j):
    return jnp.concatenate([q[:, HEAD_DIM * hd:HEAD_DIM * (hd + 1)]
                            for hd in range(GROUP * j, GROUP * (j + 1))], axis=0)


def _lane_sink(sink_ref, layer, j, block):
    head = lax.broadcasted_iota(jnp.int32, (1, GROUP * block), 1) // block
    sink = jnp.zeros((1, GROUP * block), F32)
    for g in range(GROUP):
        sink = jnp.where(head == g, sink_ref[layer, GROUP * j + g] * LOG2_E, sink)
    return sink


def _with_ones(vt):
    return jnp.concatenate([vt, jnp.ones((ONES_ROWS, vt.shape[1]), BF16)], axis=0)


def _softmax_pv_t(parts, sink):
    m = sink
    for lg, _ in parts:
        m = jnp.maximum(m, jnp.max(lg, axis=0, keepdims=True))
    out = None
    for lg, vt in parts:
        pv = _dot(vt, jnp.exp2(lg - m).astype(BF16))
        out = pv if out is None else out + pv
    den = out[HEAD_DIM:HEAD_DIM + 1, :] + jnp.exp2(sink - m)
    return out[0:HEAD_DIM, :] / den


def _store_heads(o_ref, rows, j, block, o_t):
    for g in range(GROUP):
        hd = GROUP * j + g
        o_ref[rows, HEAD_DIM * hd:HEAD_DIM * (hd + 1)] = (
            o_t[:, block * g:block * (g + 1)].T.astype(BF16))


def _ctx_attn_kernel(sink_ref, q_ref, k_ref, vt_ref, o_ref, *, layer):
    def logits(n, j):
        rows = slice(SEQ * n, SEQ * (n + 1))
        hs = slice(HEAD_DIM * j, HEAD_DIM * (j + 1))
        return _dot_nt(k_ref[rows, hs], _head_rows(q_ref[rows, :], j)), _with_ones(vt_ref[hs, rows])

    def finish(n, j, s, vt):
        o_t = _softmax_pv_t([(s, vt)], _lane_sink(sink_ref, layer, j, SEQ))
        _store_heads(o_ref, slice(SEQ * n, SEQ * (n + 1)), j, SEQ, o_t)

    _run_chains([(n, j) for n in range(CTX_SEQS_PER_STEP) for j in range(N_KV_HEADS)],
                logits, finish)


def _ctx_attention(sink, q, k, vt, layer):
    t = CTX_SEQS_PER_STEP * SEQ
    row = lambda b: (b, 0)
    return pl.pallas_call(
        functools.partial(_ctx_attn_kernel, layer=layer),
        out_shape=jax.ShapeDtypeStruct((N_CTX, ATTN_W), BF16),
        grid=(N_CTX // t,),
        in_specs=[
            pl.BlockSpec(memory_space=pltpu.SMEM),
            pl.BlockSpec((t, ATTN_W), row),
            pl.BlockSpec((t, KV_W), row),
            pl.BlockSpec((KV_W, t), lambda b: (0, b)),
        ],
        out_specs=pl.BlockSpec((t, ATTN_W), row),
        compiler_params=_params("parallel"),
        name="context_attention",
    )(sink, q, k, vt)


def _band_bias():
    kpos = jnp.arange(BAND)[None, :, None] - jnp.arange(3)[:, None, None] * WINDOW
    qpos = (jnp.arange(GROUP * QBLOCK) & (QBLOCK - 1))[None, None, :]
    return jnp.where(jnp.abs(qpos - kpos) <= WINDOW, 0.0, MASK_VALUE).astype(F32)


def _lat_attn_kernel(sink_ref, q_ref, k_ref, vt_ref, ck_ref, cv_ref, bias_ref, o_ref,
                     kc_ref, vct_ref, *, layer):
    i = pl.program_id(1)

    @pl.when(i == 0)
    def _():
        kc_ref[...] = ck_ref[...].astype(BF16)
        vct = cv_ref[...].T.astype(BF16)
        for j in range(N_KV_HEADS):
            vct_ref[j] = _with_ones(vct[HEAD_DIM * j:HEAD_DIM * (j + 1), :])

    def logits(n, j):
        start = (i * QBLOCKS_PER_STEP + n) * QBLOCK
        ws = pl.multiple_of(jnp.clip(start - WINDOW, 0, DEC_SEQ - BAND), QBLOCK)
        placement = jnp.where(start < WINDOW, 0, jnp.where(start + QBLOCK + WINDOW > DEC_SEQ, 2, 1))
        q4 = _head_rows(q_ref[QBLOCK * n:QBLOCK * (n + 1), :], j)
        hs = slice(HEAD_DIM * j, HEAD_DIM * (j + 1))
        s_ctx = _dot_nt(kc_ref[:, hs], q4)
        s_band = _dot_nt(k_ref[pl.ds(ws, BAND), hs], q4) + bias_ref[placement]
        return s_ctx, s_band, _with_ones(vt_ref[hs, pl.ds(ws, BAND)])

    def finish(n, j, s_ctx, s_band, vbt):
        o_t = _softmax_pv_t([(s_ctx, vct_ref[j]), (s_band, vbt)],
                            _lane_sink(sink_ref, layer, j, QBLOCK))
        _store_heads(o_ref, slice(QBLOCK * n, QBLOCK * (n + 1)), j, QBLOCK, o_t)

    _run_chains([(n, j) for n in range(QBLOCKS_PER_STEP) for j in range(N_KV_HEADS)],
                logits, finish, ahead=2)


def _lat_attention(sink, q, k, vt, cache_k, cache_v, band_bias, layer):
    t = QBLOCKS_PER_STEP * QBLOCK
    steps = DEC_SEQ // t
    seq_block = N_CTX // DEC_SEQ
    return pl.pallas_call(
        functools.partial(_lat_attn_kernel, layer=layer),
        out_shape=jax.ShapeDtypeStruct((N_LAT, ATTN_W), BF16),
        grid=(DEC_BATCH, steps),
        in_specs=[
            pl.BlockSpec(memory_space=pltpu.SMEM),
            pl.BlockSpec((t, ATTN_W), lambda b, i: (N_CTX // t + b * steps + i, 0)),
            pl.BlockSpec((DEC_SEQ, KV_W), lambda b, i: (seq_block + b, 0)),
            pl.BlockSpec((KV_W, DEC_SEQ), lambda b, i: (0, seq_block + b)),
            pl.BlockSpec((None, None, PAST_LEN, KV_W), lambda b, i: (b, layer, 0, 0)),
            pl.BlockSpec((None, None, PAST_LEN, KV_W), lambda b, i: (b, layer, 0, 0)),
            pl.BlockSpec((3, BAND, GROUP * QBLOCK), lambda b, i: (0, 0, 0)),
        ],
        out_specs=pl.BlockSpec((t, ATTN_W), lambda b, i: (b * steps + i, 0)),
        scratch_shapes=[pltpu.VMEM((PAST_LEN, KV_W), BF16),
                        pltpu.VMEM((N_KV_HEADS, HEAD_DIM + ONES_ROWS, PAST_LEN), BF16)],
        compiler_params=_params("parallel", "arbitrary"),
        name="latent_attention",
    )(sink, q, k, vt, cache_k, cache_v, band_bias)


N_GATE_CHUNKS = D_MODEL // MXU_COLS


def _merge_kernel(h_ref, atc_ref, atl_ref, ybc_ref, ybl_ref, x0_ref, *rest, n_x):
    x_refs, mod_ref, rest = rest[0:n_x], rest[n_x], rest[n_x + 1:]
    wga_refs = rest[0:N_GATE_CHUNKS]
    wgb_refs = rest[N_GATE_CHUNKS:2 * N_GATE_CHUNKS]
    wpa_ref, wpb_ref, wo_ref, g_ref, b_ref, xo_ref, ho_ref, mg_ref = rest[2 * N_GATE_CHUNKS:]
    is_ctx = pl.program_id(0) < N_CTX // ROW_TILE
    g1 = mod_ref[:, 2 * D_MODEL:3 * D_MODEL]
    sh2 = mod_ref[:, 3 * D_MODEL:4 * D_MODEL]
    sc2 = mod_ref[:, 4 * D_MODEL:5 * D_MODEL]

    def merge_branches(s):
        rs = slice(SUB_ROWS * s, SUB_ROWS * (s + 1))
        h = h_ref[rs, :]
        attn = jnp.where(is_ctx, atc_ref[rs, :], atl_ref[rs, :])
        conv = jnp.where(is_ctx, ybc_ref[rs, :], ybl_ref[rs, :])
        hy = (x0_ref[rs, :].astype(F32) * conv.astype(F32)).astype(BF16)
        for j in range(N_GATE_CHUNKS):
            cs = slice(MXU_COLS * j, MXU_COLS * (j + 1))
            ga = jax.nn.sigmoid(_dot(h, wga_refs[j][...]))
            gb = jax.nn.sigmoid(_dot(h, wgb_refs[j][...]))
            mg_ref[rs, cs] = (ga * _dot(attn, wpa_ref[:, cs])
                              + gb * _dot(hy, wpb_ref[:, cs])).astype(BF16)
        return ()

    def project_norm(s):
        rs = slice(SUB_ROWS * s, SUB_ROWS * (s + 1))
        sub = _dot(mg_ref[rs, :], wo_ref[...])
        x_in = x_refs[0][rs, :] if n_x == 1 else jnp.where(is_ctx, x_refs[0][rs, :], x_refs[1][rs, :])
        x = _layer_norm(DEEPNORM_ALPHA * x_in + g1 * sub, g_ref[...], b_ref[...])
        xo_ref[rs, :] = x
        ho_ref[rs, :] = (x * (1.0 + sc2) + sh2).astype(BF16)

    _run_chains([(s,) for s in range(ROW_TILE // SUB_ROWS)], merge_branches, project_norm)


def _merge(h, atc, atl, ybc, ybl, x0, x, mods, w_in, wpa, wpb, wo, ln_g, ln_b, layer):
    t = ROW_TILE
    nct = N_CTX // t
    row = lambda m: (m, 0)
    ctx_row = lambda m: (jnp.minimum(m, nct - 1), 0)
    lat_row = lambda m: (jnp.maximum(m - nct, 0), 0)
    gate_block0 = GATE_COL0 // MXU_COLS
    gate_specs = [_weight_spec((D_MODEL, MXU_COLS), (0, gate_block0 + j))
                  for j in range(2 * N_GATE_CHUNKS)]
    x_specs = [pl.BlockSpec((t, D_MODEL), row)] if len(x) == 1 else _split_stream_specs(t)
    return pl.pallas_call(
        functools.partial(_merge_kernel, n_x=len(x)),
        out_shape=(jax.ShapeDtypeStruct((N_TOK, D_MODEL), F32),
                   jax.ShapeDtypeStruct((N_TOK, D_MODEL), BF16)),
        grid=(N_TOK // t,),
        in_specs=[
            pl.BlockSpec((t, D_MODEL), row),
            pl.BlockSpec((t, ATTN_W), ctx_row),
            pl.BlockSpec((t, ATTN_W), lat_row),
            pl.BlockSpec((t, HY_W), ctx_row),
            pl.BlockSpec((t, HY_W), lat_row),
            pl.BlockSpec((t, HY_W), row),
            *x_specs,
            _mod_spec(layer, t),
            *gate_specs,
            _weight_spec((ATTN_W, D_MODEL)),
            _weight_spec((HY_W, D_MODEL)),
            _weight_spec((D_MODEL, D_MODEL)),
            _layer_spec((1, D_MODEL), layer),
            _layer_spec((1, D_MODEL), layer),
        ],
        out_specs=(pl.BlockSpec((t, D_MODEL), row), pl.BlockSpec((t, D_MODEL), row)),
        scratch_shapes=[pltpu.VMEM((t, D_MODEL), BF16)],
        compiler_params=_params("parallel"),
        name="merge_ln1",
    )(h, atc, atl, ybc, ybl, x0, *x, mods, *([w_in] * (2 * N_GATE_CHUNKS)), wpa, wpb, wo,
      ln_g, ln_b)


def _ffn_kernel(h_ref, hp_ref, hn_ref, x_ref, mod_ref, nmod_ref, wg_ref, wv_ref, wd_ref,
                cw_ref, cb_ref, g_ref, b_ref, *rest, last_layer):
    if last_layer:
        yp_ref, ys_ref, hx_ref, a_ref, xo_ref = rest
    else:
        xo_ref, ho_ref, hx_ref, a_ref = rest
    n_sub = ROW_TILE // SUB_ROWS
    ext = SUB_ROWS + 2 * HALO
    is_ctx, seq_start, seq_end = _tile_edges(pl.program_id(0), ROW_TILE)
    _fill_extended(hx_ref, hp_ref, h_ref, hn_ref)
    g2 = mod_ref[:, 5 * D_MODEL:6 * D_MODEL]
    sh = nmod_ref[:, 0:D_MODEL]
    sc = nmod_ref[:, D_MODEL:2 * D_MODEL]

    def up_gate(j):
        rs = slice(SUB_ROWS * j, SUB_ROWS * (j + 1))
        zero_top, zero_bot = _sub_edges(j, n_sub, is_ctx, seq_start, seq_end)
        hx = hx_ref[SUB_ROWS * j:SUB_ROWS * j + ext, :]
        hm = h_ref[rs, :]
        for c in range(D_FF // MXU_COLS):
            cs = slice(MXU_COLS * c, MXU_COLS * (c + 1))
            gate = _dwconv3(_dot(hx, wg_ref[:, cs]), zero_top, zero_bot, cw_ref, cb_ref, cs)
            gelu = 0.5 * gate * (1.0 + lax.erf(gate * math.sqrt(0.5)))
            a_ref[rs, cs] = (gelu * _dot(hm, wv_ref[:, cs])).astype(BF16)
        return ()

    def down_norm(j):
        rs = slice(SUB_ROWS * j, SUB_ROWS * (j + 1))
        sub = _dot(a_ref[rs, :], wd_ref[...])
        x = _layer_norm(DEEPNORM_ALPHA * x_ref[rs, :] + g2 * sub, g_ref[...], b_ref[...])
        xo_ref[rs, :] = x
        if not last_layer:
            ho_ref[rs, :] = (x * (1.0 + sc) + sh).astype(BF16)

    _run_chains([(j,) for j in range(n_sub)], up_gate, down_norm)

    if last_layer:
        @pl.when(is_ctx)
        def _():
            yp_ref[...] = xo_ref[...]

        @pl.when(jnp.logical_not(is_ctx))
        def _():
            ys_ref[...] = xo_ref[...]


def _ffn(h, x, mods, w_up, w_down, conv_w, conv_b, ln_g, ln_b, layer):
    t = ROW_TILE
    nct = N_CTX // t
    row = lambda m: (m, 0)
    last_layer = layer == DEPTH - 1
    if last_layer:
        out_shape = (jax.ShapeDtypeStruct((N_CTX, D_MODEL), F32),
                     jax.ShapeDtypeStruct((N_LAT, D_MODEL), F32))
        out_specs = (pl.BlockSpec((t, D_MODEL), lambda m: (jnp.minimum(m, nct - 1), 0)),
                     pl.BlockSpec((t, D_MODEL), lambda m: (jnp.maximum(m - nct, 0), 0)))
        scratch = [pltpu.VMEM((t, D_MODEL), F32)]
    else:
        out_shape = (jax.ShapeDtypeStruct((N_TOK, D_MODEL), F32),
                     jax.ShapeDtypeStruct((N_TOK, D_MODEL), BF16))
        out_specs = (pl.BlockSpec((t, D_MODEL), row), pl.BlockSpec((t, D_MODEL), row))
        scratch = []
    return pl.pallas_call(
        functools.partial(_ffn_kernel, last_layer=last_layer),
        out_shape=out_shape,
        grid=(N_TOK // t,),
        in_specs=[
            pl.BlockSpec((t, D_MODEL), row),
            *_halo_specs(t, D_MODEL),
            pl.BlockSpec((t, D_MODEL), row),
            _mod_spec(layer, t),
            _mod_spec(min(layer + 1, DEPTH - 1), t),
            _weight_spec((D_MODEL, D_FF), (0, 0)),
            _weight_spec((D_MODEL, D_FF), (0, 1)),
            _weight_spec((D_FF, D_MODEL)),
            _layer_spec((3, D_FF), layer),
            _layer_spec((1, D_FF), layer),
            _layer_spec((1, D_MODEL), layer),
            _layer_spec((1, D_MODEL), layer),
        ],
        out_specs=out_specs,
        scratch_shapes=[pltpu.VMEM((t + 2 * HALO, D_MODEL), BF16), pltpu.VMEM((t, D_FF), BF16),
                        *scratch],
        compiler_params=_params("arbitrary"),
        name="conv_ffn_ln2",
    )(h, h, h, x, mods, mods, w_up, w_up, w_down, conv_w, conv_b, ln_g, ln_b)


def kernel(x_prompt, x_sample, cache_k, cache_v, c, c_ctx, w_ada, b_ada, w_in, attn_sink,
           hy_conv_w, hy_conv_b, filt_w1, filt_b1, filt_freq1, filt_w2, filt_b2, filt_freq2,
           filt_w3, hy_bias, w_pa, w_pb, w_out, ln1_g, ln1_b, w_up, ffn_conv_w, ffn_conv_b,
           w_down, ln2_g, ln2_b):
    cond = jnp.concatenate([c_ctx[None, :], c,
                            jnp.zeros((N_MOD_ROWS - 1 - DEC_BATCH, D_MODEL), F32)], axis=0)
    mods = _modulation(cond, w_ada, b_ada).reshape(DEPTH, N_MOD_ROWS, 1, 6 * D_MODEL)

    x = (x_prompt.reshape(N_CTX, D_MODEL), x_sample.reshape(N_LAT, D_MODEL))
    h = _prologue(*x, mods)

    cos_tab, sin_tab = _rope_tables()
    conv_args = {}
    for seq in (SEQ, DEC_SEQ):
        tables = _dft_tables(seq)
        hs = _filters(seq, filt_w1, filt_b1, filt_freq1, filt_w2, filt_b2, filt_freq2, filt_w3)
        conv_args[seq] = (tables, _filter_spectrum(seq, tables[0], hs))

    cache_k = cache_k.reshape(DEC_BATCH, DEPTH, PAST_LEN, KV_W)
    cache_v = cache_v.reshape(DEC_BATCH, DEPTH, PAST_LEN, KV_W)
    hy_conv_b3 = hy_conv_b.reshape(DEPTH, 1, 3 * HY_W)
    ffn_conv_b3 = ffn_conv_b.reshape(DEPTH, 1, D_FF)
    ln1_g3, ln1_b3 = ln1_g.reshape(DEPTH, 1, D_MODEL), ln1_b.reshape(DEPTH, 1, D_MODEL)
    ln2_g3, ln2_b3 = ln2_g.reshape(DEPTH, 1, D_MODEL), ln2_b.reshape(DEPTH, 1, D_MODEL)

    band_bias = _band_bias()
    w_in_l = w_in[0:1].astype(BF16)
    keys, values = [], []
    for l in range(DEPTH):
        to_cast = [(w, l) for w in (w_pa, w_pb, w_out, w_up, w_down)]
        if l + 1 < DEPTH:
            to_cast.append((w_in, l + 1))
        q, k, vt, kf, vf, x0, vh, w_pa_l, w_pb_l, w_out_l, w_up_l, w_down_l, *w_in_next = _in_proj(
            h, w_in_l, cos_tab, sin_tab, hy_conv_w, hy_conv_b3, to_cast, l)
        keys.append(kf)
        values.append(vf)

        ybc = _long_conv(SEQ, N_CTX, 0, vh, *conv_args[SEQ], hy_bias, l)
        ybl = _long_conv(DEC_SEQ, N_LAT, N_CTX, vh, *conv_args[DEC_SEQ], hy_bias, l)
        atc = _ctx_attention(attn_sink, q, k, vt, l)
        atl = _lat_attention(attn_sink, q, k, vt, cache_k, cache_v, band_bias, l)

        x1, h2 = _merge(h, atc, atl, ybc, ybl, x0, x, mods, w_in_l, w_pa_l, w_pb_l, w_out_l,
                        ln1_g3, ln1_b3, l)
        x2, h = _ffn(h2, x1, mods, w_up_l, w_down_l, ffn_conv_w, ffn_conv_b3, ln2_g3, ln2_b3, l)
        x = (x2,)
        if w_in_next:
            w_in_l = w_in_next[0]

    y_prompt_rows, y_sample_rows = x2, h
    y_prompt = y_prompt_rows.reshape(BATCH, SEQ, D_MODEL)
    y_sample = y_sample_rows.reshape(DEC_BATCH, DEC_SEQ, D_MODEL)
    to_cache = lambda ts: jnp.stack(ts, axis=1).reshape(BATCH, DEPTH, SEQ, N_KV_HEADS, HEAD_DIM)
    new_k = to_cache([t.reshape(BATCH, SEQ, KV_W) for t in keys])
    new_v = to_cache([t.reshape(BATCH, SEQ, KV_W) for t in values])
    return (y_prompt, y_sample, new_k, new_v)
```
